```python
import math
import jax
import jax.numpy as jnp
from jax import lax
import numpy as np

D_MODEL = 1024
BATCH = 8
SEQ = 4096
DEPTH = 2

GRID_W = 64
CTX_LEN = 256
NORM_EPS = 1e-6
DN_HEADS = 4
DN_DK = 128
DN_DV = 128
DN_CONV = 3
DN_CHUNK = 64
SC_WIDTH = 512
SC_CONV = 3
MLA_HEADS = 4
MLA_Q_LORA = 256
MLA_KV_LORA = 128
MLA_NOPE = 128
MLA_ROPE = 64
MLA_V = 128
MLA_SCALE = (MLA_NOPE + MLA_ROPE) ** -0.5
Q_BLOCK = 128
ROPE_BASE = 10000.0
N_EXPERTS = 32
TOP_K = 4
EXPERT_FF = 1024
SWIGLU_ALPHA = 1.702
SWIGLU_LIMIT = 7.0
EXPERT_BLOCK = 256
IN_WIDTHS = (DN_HEADS * DN_DK, DN_HEADS * DN_DK, DN_HEADS * DN_DV, DN_HEADS * DN_DV, 2 * DN_HEADS, 2 * DN_HEADS,
             SC_WIDTH, SC_WIDTH, SC_WIDTH, MLA_Q_LORA, MLA_KV_LORA + MLA_ROPE)
IN_COLS = sum(IN_WIDTHS)

kernel_name = "hybrid_gdn_shortconv_mla_moe_dit"

F32 = jnp.float32


def rms_norm(x, g):
    xf = x.astype(F32)
    y = xf * lax.rsqrt(jnp.mean(xf * xf, axis=-1, keepdims=True) + NORM_EPS)
    return (y * g.astype(F32)).astype(x.dtype)


def l2_norm(x):
    xf = x.astype(F32)
    return xf * lax.rsqrt(jnp.sum(xf * xf, axis=-1, keepdims=True) + 1e-6)


def dw_conv(x, w):
    k = w.shape[0]
    return lax.conv_general_dilated(x, w[:, None, :].astype(x.dtype), (1,), [(k // 2, k // 2)],
                                    dimension_numbers=('NWC', 'WIO', 'NWC'), feature_group_count=x.shape[-1])


def split_in_proj(p):
    idx = []
    s = 0
    for wd in IN_WIDTHS[:-1]:
        s += wd
        idx.append(s)
    return jnp.split(p, idx, axis=-1)


def axial_rope_tables(rows):
    row = jnp.repeat(jnp.arange(rows, dtype=F32), GRID_W)
    col = jnp.tile(jnp.arange(GRID_W, dtype=F32), rows)
    axis_dims = MLA_ROPE // 2
    inv = ROPE_BASE ** (-jnp.arange(0, axis_dims, 2, dtype=F32) / axis_dims)
    ang = jnp.stack([row[:, None] * inv, col[:, None] * inv], axis=1)
    return jnp.cos(ang), jnp.sin(ang)


def apply_axial_rope(x, cos, sin):
    xf = x.astype(F32)
    xa = xf.reshape(x.shape[:-1] + (2, MLA_ROPE // 2))
    bshape = (1, x.shape[1]) + (1,) * (x.ndim - 3) + (2, cos.shape[-1])
    c, s = cos.reshape(bshape), sin.reshape(bshape)
    x1, x2 = jnp.split(xa, 2, axis=-1)
    out = jnp.concatenate([x1 * c - x2 * s, x2 * c + x1 * s], axis=-1)
    return out.reshape(x.shape).astype(x.dtype)


def chunk_gated_delta(q, k, v, beta, g, s0):
    bsz, h, t, dk = q.shape
    dv = v.shape[-1]
    n = t // DN_CHUNK

    def chunks(a):
        return a.reshape((bsz, h, n, DN_CHUNK) + a.shape[3:])

    q = chunks(q.astype(F32) * dk ** -0.5)
    k = chunks(k.astype(F32))
    v = chunks(v.astype(F32))
    beta = chunks(beta.astype(F32))
    gc = jnp.cumsum(chunks(g.astype(F32)), axis=-1)
    causal = jnp.tril(jnp.ones((DN_CHUNK, DN_CHUNK), bool))
    strict = jnp.tril(jnp.ones((DN_CHUNK, DN_CHUNK), bool), -1)
    decay = jnp.exp(jnp.where(causal, gc[..., :, None] - gc[..., None, :], -jnp.inf))
    kb = k * beta[..., None]
    lower = jnp.where(strict, jnp.einsum('bhnid,bhnjd->bhnij', kb, k) * decay, 0.0)
    a_mat = lower + jnp.eye(DN_CHUNK, dtype=F32)
    rhs = jnp.concatenate([v * beta[..., None], kb * jnp.exp(gc)[..., None]], axis=-1)
    sol = lax.linalg.triangular_solve(a_mat, rhs, left_side=True, lower=True, unit_diagonal=True)
    u, w = sol[..., :dv], sol[..., dv:]
    intra = jnp.einsum('bhnid,bhnjd->bhnij', q, k) * decay
    q_dec = q * jnp.exp(gc)[..., None]
    k_dec = k * jnp.exp(gc[..., -1:] - gc)[..., None]
    g_tot = jnp.exp(gc[..., -1])

    def step(s, xs):
        q_c, k_c, u_c, w_c, a_c, gt = xs
        v_new = u_c - jnp.einsum('bhcd,bhde->bhce', w_c, s)
        o = jnp.einsum('bhcd,bhde->bhce', q_c, s) + jnp.einsum('bhij,bhje->bhie', a_c, v_new)
        s = s * gt[..., None, None] + jnp.einsum('bhcd,bhce->bhde', k_c, v_new)
        return s, o

    xs = tuple(jnp.moveaxis(a, 2, 0) for a in (q_dec, k_dec, u, w, intra, g_tot))
    s_fin, o = lax.scan(step, s0.astype(F32), xs)
    o = jnp.moveaxis(o, 0, 2).reshape(bsz, h, t, dv)
    return o, s_fin


def deltanet_inputs(q, k, v, a, b, conv_w, a_log, dt_bias):
    bsz, t, _ = q.shape
    qkv = jax.nn.silu(dw_conv(jnp.concatenate([q, k, v], axis=-1), conv_w))
    nqk = DN_HEADS * DN_DK
    q, k, v = jnp.split(qkv, [nqk, 2 * nqk], axis=-1)

    def heads(z, dim):
        return z.reshape(bsz, t, DN_HEADS, dim).transpose(0, 2, 1, 3)

    q = l2_norm(heads(q, DN_DK))
    k = l2_norm(heads(k, DN_DK))
    v = heads(v, DN_DV).astype(F32)
    a = a.astype(F32).reshape(bsz, t, 2, DN_HEADS)
    b = b.astype(F32).reshape(bsz, t, 2, DN_HEADS)
    beta = jax.nn.sigmoid(b).transpose(2, 0, 3, 1)
    g = (-jnp.exp(a_log.astype(F32))[None, None] * jax.nn.softplus(a + dt_bias.astype(F32))).transpose(2, 0, 3, 1)
    return q, k, v, beta, g


def bidirectional_deltanet(st_c, st_l):
    qc, kc, vc, bc, gc = st_c
    ql, kl, vl, bl, gl = st_l
    s0 = jnp.zeros((qc.shape[0], DN_HEADS, DN_DK, DN_DV), F32)
    outs_l = []
    outs_c = []
    for d in range(2):
        if d == 0:
            f = lambda a: a
        else:
            f = lambda a: jnp.flip(a, axis=2)
        oc, s_ctx = chunk_gated_delta(f(qc), f(kc), f(vc), f(bc[d]), f(gc[d]), s0)
        ol, _ = chunk_gated_delta(f(ql), f(kl), f(vl), f(bl[d]), f(gl[d]), s_ctx)
        outs_l.append(f(ol))
        outs_c.append(f(oc))
    return outs_l[0] + outs_l[1], outs_c[0] + outs_c[1]


def deltanet_output(o, z, norm_g):
    bsz, h, t, dv = o.shape
    o = o.transpose(0, 2, 1, 3)
    zz = z.reshape(bsz, t, h, dv)
    y = rms_norm(o, norm_g).astype(z.dtype) * jax.nn.silu(zz)
    return y.reshape(bsz, t, h * dv)


def short_conv_mixer(hh, b_gate, c_gate, conv_w):
    return b_gate * dw_conv(c_gate * hh, conv_w)


def mla_queries(qa, q_norm_g, w_qb, rope):
    bsz, t, _ = qa.shape
    q = (rms_norm(qa, q_norm_g) @ w_qb).reshape(bsz, t, MLA_HEADS, MLA_NOPE + MLA_ROPE)
    if rope is None:
        return q
    return jnp.concatenate([q[..., :MLA_NOPE], apply_axial_rope(q[..., MLA_NOPE:], *rope)], axis=-1)


def mla_keys(kva, kv_norm_g, w_kvb, rope):
    bsz, t, _ = kva.shape
    c_kv, k_rope = kva[..., :MLA_KV_LORA], kva[..., MLA_KV_LORA:]
    kv = (rms_norm(c_kv, kv_norm_g) @ w_kvb).reshape(bsz, t, MLA_HEADS, MLA_NOPE + MLA_V)
    if rope is not None:
        k_rope = apply_axial_rope(k_rope, *rope)
    k = jnp.concatenate([kv[..., :MLA_NOPE],
                         jnp.broadcast_to(k_rope[:, :, None, :], (bsz, t, MLA_HEADS, MLA_ROPE))], axis=-1)
    return k, kv[..., MLA_NOPE:]


def attend(q, k, v):
    s = jnp.einsum('bqhd,bkhd->bhqk', q, k, preferred_element_type=F32) * MLA_SCALE
    p = jax.nn.softmax(s, axis=-1).astype(v.dtype)
    return jnp.einsum('bhqk,bkhd->bqhd', p, v)


def mla_latent_attention(q_lat, k_all, v_all):
    bsz, t, h, dq = q_lat.shape
    nb = t // Q_BLOCK
    qb = jnp.moveaxis(q_lat.reshape(bsz, nb, Q_BLOCK, h, dq), 1, 0)
    out = lax.map(lambda qq: attend(qq, k_all, v_all), qb)
    return jnp.moveaxis(out, 0, 1).reshape(bsz, t, h * MLA_V)


def merge_branches(xm, y_dn, y_sc, y_mla, w_gate, b_gate, w_dn, w_sc, w_mla, w_out):
    gates = jax.nn.sigmoid(xm @ w_gate + b_gate)
    g_dn, g_sc, g_mla = jnp.split(gates, 3, axis=-1)
    merged = g_dn * (y_dn @ w_dn) + g_sc * (y_sc @ w_sc) + g_mla * (y_mla @ w_mla)
    return merged @ w_out


def mixer_sublayer(xm_lat, xm_ctx, rope, need_ctx, w_in, dn_conv_w, dn_a_log, dn_dt_bias, dn_norm_g, sc_conv_w,
                   mla_q_norm_g, mla_w_qb, mla_kv_norm_g, mla_w_kvb, w_branch_gate, b_branch_gate,
                   w_branch_dn, w_branch_sc, w_branch_mla, w_out):
    (dq_l, dk_l, dv_l, dz_l, da_l, db_l, sh_l, sb_l, scg_l, qa_l, kva_l) = split_in_proj(xm_lat @ w_in)
    (dq_c, dk_c, dv_c, dz_c, da_c, db_c, sh_c, sb_c, scg_c, qa_c, kva_c) = split_in_proj(xm_ctx @ w_in)
    st_l = deltanet_inputs(dq_l, dk_l, dv_l, da_l, db_l, dn_conv_w, dn_a_log, dn_dt_bias)
    st_c = deltanet_inputs(dq_c, dk_c, dv_c, da_c, db_c, dn_conv_w, dn_a_log, dn_dt_bias)
    o_l, o_c = bidirectional_deltanet(st_c, st_l)
    y_dn_l = deltanet_output(o_l, dz_l, dn_norm_g)
    y_sc_l = short_conv_mixer(sh_l, sb_l, scg_l, sc_conv_w)
    k_c, v_c = mla_keys(kva_c, mla_kv_norm_g, mla_w_kvb, None)
    k_l, v_l = mla_keys(kva_l, mla_kv_norm_g, mla_w_kvb, rope)
    q_l = mla_queries(qa_l, mla_q_norm_g, mla_w_qb, rope)
    y_mla_l = mla_latent_attention(q_l, jnp.concatenate([k_l, k_c], axis=1), jnp.concatenate([v_l, v_c], axis=1))
    y_lat = merge_branches(xm_lat, y_dn_l, y_sc_l, y_mla_l, w_branch_gate, b_branch_gate,
                           w_branch_dn, w_branch_sc, w_branch_mla, w_out)
    if not need_ctx:
        return y_lat, None
    bsz, tc, _ = xm_ctx.shape
    y_dn_c = deltanet_output(o_c, dz_c, dn_norm_g)
    y_sc_c = short_conv_mixer(sh_c, sb_c, scg_c, sc_conv_w)
    q_c = mla_queries(qa_c, mla_q_norm_g, mla_w_qb, None)
    y_mla_c = attend(q_c, k_c, v_c).reshape(bsz, tc, MLA_HEADS * MLA_V)
    y_ctx = merge_branches(xm_ctx, y_dn_c, y_sc_c, y_mla_c, w_branch_gate, b_branch_gate,
                           w_branch_dn, w_branch_sc, w_branch_mla, w_out)
    return y_lat, y_ctx


def moe_ffn(xt, router_w, router_b, w1, b1, w2, b2):
    n_tok, d = xt.shape
    logits = jnp.dot(xt, router_w, preferred_element_type=F32) + router_b.astype(F32)
    top_val, top_idx = lax.top_k(logits, TOP_K)
    gate = jax.nn.softmax(top_val, axis=-1)
    n_assign = n_tok * TOP_K
    flat_e = top_idx.reshape(n_assign)
    order = jnp.argsort(flat_e)
    sorted_e = flat_e[order]
    counts = jnp.bincount(flat_e, length=N_EXPERTS)
    padded = (counts + EXPERT_BLOCK - 1) // EXPERT_BLOCK * EXPERT_BLOCK
    pad_end = jnp.cumsum(padded)
    pad_start = pad_end - padded
    start = jnp.cumsum(counts) - counts
    dest = pad_start[sorted_e] + jnp.arange(n_assign) - start[sorted_e]
    n_blocks = -(-n_assign // EXPERT_BLOCK) + N_EXPERTS
    n_rows = n_blocks * EXPERT_BLOCK
    row_token = jnp.full((n_rows,), n_tok, jnp.int32).at[dest].set((order // TOP_K).astype(jnp.int32))
    row_gate = jnp.zeros((n_rows,), F32).at[dest].set(gate.reshape(n_assign)[order])
    block_expert = jnp.minimum(jnp.searchsorted(pad_end, jnp.arange(n_blocks) * EXPERT_BLOCK, side='right'),
                               N_EXPERTS - 1)
    x_pad = jnp.concatenate([xt, jnp.zeros((1, d), xt.dtype)], axis=0)

    def expert_block(args):
        tok, gw, e = args
        hdn = x_pad[tok] @ w1[e] + b1[e]
        glu = jnp.minimum(hdn[..., 0::2], SWIGLU_LIMIT)
        lin = jnp.clip(hdn[..., 1::2], -SWIGLU_LIMIT, SWIGLU_LIMIT)
        act = glu * jax.nn.sigmoid(SWIGLU_ALPHA * glu) * (lin + 1.0)
        return (act @ w2[e] + b2[e]) * gw[:, None].astype(xt.dtype)

    y_rows = lax.map(expert_block, (row_token.reshape(n_blocks, EXPERT_BLOCK),
                                    row_gate.reshape(n_blocks, EXPERT_BLOCK), block_expert))
    y = jax.ops.segment_sum(y_rows.reshape(n_rows, d), row_token, num_segments=n_tok + 1)
    return y[:n_tok]


def setup_inputs(seed: int = 0) -> dict:
    key = jax.random.key(seed)
    keys = jax.random.split(key, 40)
    counter = [0]

    def nxt():
        k = keys[counter[0]]
        counter[0] += 1
        return k

    def nrm(shape, scale):
        return jax.random.normal(nxt(), shape, F32) * scale

    def gain(shape):
        return 1.0 + 0.05 * jax.random.normal(nxt(), shape, F32)

    L, D = DEPTH, D_MODEL
    x = nrm((BATCH, SEQ, D), 1.0)
    c = nrm((BATCH, D), 1.0)
    ctx = nrm((BATCH, CTX_LEN, D), 1.0)
    c_ctx = nrm((D,), 1.0)
    ada_w = nrm((L, D, 6 * D), 0.5 * D ** -0.5)
    ada_b = nrm((L, 6 * D), 0.01)
    norm1_g = gain((L, D))
    norm2_g = gain((L, D))
    w_in = nrm((L, D, IN_COLS), D ** -0.5)
    dn_conv_w = nrm((L, DN_CONV, 2 * DN_HEADS * DN_DK + DN_HEADS * DN_DV), DN_CONV ** -0.5)
    dn_a_log = jnp.log(jax.random.uniform(nxt(), (L, 2, DN_HEADS), F32, 1.0, 16.0))
    dt = jnp.exp(jax.random.uniform(nxt(), (L, 2, DN_HEADS), F32, math.log(1e-3), math.log(1e-1)))
    dn_dt_bias = dt + jnp.log(-jnp.expm1(-dt))
    dn_norm_g = gain((L, DN_DV))
    sc_conv_w = nrm((L, SC_CONV, SC_WIDTH), SC_CONV ** -0.5)
    mla_q_norm_g = gain((L, MLA_Q_LORA))
    mla_w_qb = nrm((L, MLA_Q_LORA, MLA_HEADS * (MLA_NOPE + MLA_ROPE)), MLA_Q_LORA ** -0.5)
    mla_kv_norm_g = gain((L, MLA_KV_LORA))
    mla_w_kvb = nrm((L, MLA_KV_LORA, MLA_HEADS * (MLA_NOPE + MLA_V)), MLA_KV_LORA ** -0.5)
    w_branch_gate = nrm((L, D, 3 * D), D ** -0.5)
    b_branch_gate = nrm((L, 3 * D), 0.01)
    w_branch_dn = nrm((L, DN_HEADS * DN_DV, D), (DN_HEADS * DN_DV) ** -0.5)
    w_branch_sc = nrm((L, SC_WIDTH, D), SC_WIDTH ** -0.5)
    w_branch_mla = nrm((L, MLA_HEADS * MLA_V, D), (MLA_HEADS * MLA_V) ** -0.5)
    w_out = nrm((L, D, D), D ** -0.5)
    router_w = nrm((L, D, N_EXPERTS), D ** -0.5)
    router_b = nrm((L, N_EXPERTS), 0.01)
    expert_w1 = nrm((L, N_EXPERTS, D, 2 * EXPERT_FF), D ** -0.5)
    expert_b1 = nrm((L, N_EXPERTS, 2 * EXPERT_FF), 0.01)
    expert_w2 = nrm((L, N_EXPERTS, EXPERT_FF, D), EXPERT_FF ** -0.5)
    expert_b2 = nrm((L, N_EXPERTS, D), 0.01)
    final_norm_g = gain((D,))
    return {"x": x, "c": c, "ctx": ctx, "c_ctx": c_ctx, "ada_w": ada_w, "ada_b": ada_b,
            "norm1_g": norm1_g, "norm2_g": norm2_g, "w_in": w_in, "dn_conv_w": dn_conv_w,
            "dn_a_log": dn_a_log, "dn_dt_bias": dn_dt_bias, "dn_norm_g": dn_norm_g, "sc_conv_w": sc_conv_w,
            "mla_q_norm_g": mla_q_norm_g, "mla_w_qb": mla_w_qb, "mla_kv_norm_g": mla_kv_norm_g,
            "mla_w_kvb": mla_w_kvb, "w_branch_gate": w_branch_gate, "b_branch_gate": b_branch_gate,
            "w_branch_dn": w_branch_dn, "w_branch_sc": w_branch_sc, "w_branch_mla": w_branch_mla,
            "w_out": w_out, "router_w": router_w, "router_b": router_b, "expert_w1": expert_w1,
            "expert_b1": expert_b1, "expert_w2": expert_w2, "expert_b2": expert_b2,
            "final_norm_g": final_norm_g}


def reference(x, c, ctx, c_ctx, ada_w, ada_b, norm1_g, norm2_g, w_in, dn_conv_w, dn_a_log, dn_dt_bias, dn_norm_g,
              sc_conv_w, mla_q_norm_g, mla_w_qb, mla_kv_norm_g, mla_w_kvb, w_branch_gate, b_branch_gate,
              w_branch_dn, w_branch_sc, w_branch_mla, w_out, router_w, router_b, expert_w1, expert_b1,
              expert_w2, expert_b2, final_norm_g):
    bsz, t, d = x.shape
    rows = t // GRID_W
    rope = axial_rope_tables(rows)
    silu_c = jax.nn.silu(c)
    silu_cc = jax.nn.silu(c_ctx)[None]
    h_lat = x
    h_ctx = ctx
    for l in range(DEPTH):
        need_ctx = l < DEPTH - 1
        mod_lat = (silu_c @ ada_w[l] + ada_b[l])[:, None, :]
        mod_ctx = (silu_cc @ ada_w[l] + ada_b[l])[:, None, :]
        sh1, sc1, g1, sh2, sc2, g2 = jnp.split(mod_lat, 6, axis=-1)
        csh1, csc1, cg1, csh2, csc2, cg2 = jnp.split(mod_ctx, 6, axis=-1)
        xm_lat = rms_norm(h_lat, norm1_g[l]) * (1.0 + sc1) + sh1
        xm_ctx = rms_norm(h_ctx, norm1_g[l]) * (1.0 + csc1) + csh1
        y_lat, y_ctx = mixer_sublayer(xm_lat, xm_ctx, rope, need_ctx, w_in[l], dn_conv_w[l], dn_a_log[l],
                                      dn_dt_bias[l], dn_norm_g[l], sc_conv_w[l], mla_q_norm_g[l], mla_w_qb[l],
                                      mla_kv_norm_g[l], mla_w_kvb[l], w_branch_gate[l], b_branch_gate[l],
                                      w_branch_dn[l], w_branch_sc[l], w_branch_mla[l], w_out[l])
        h_lat = h_lat + g1 * y_lat
        xm2_lat = rms_norm(h_lat, norm2_g[l]) * (1.0 + sc2) + sh2
        if need_ctx:
            h_ctx = h_ctx + cg1 * y_ctx
            xm2_ctx = rms_norm(h_ctx, norm2_g[l]) * (1.0 + csc2) + csh2
            tokens = jnp.concatenate([xm2_lat.reshape(-1, d), xm2_ctx.reshape(-1, d)], axis=0)
            y = moe_ffn(tokens, router_w[l], router_b[l], expert_w1[l], expert_b1[l], expert_w2[l], expert_b2[l])
            n_lat = bsz * t
            h_lat = h_lat + g2 * y[:n_lat].reshape(bsz, t, d)
            h_ctx = h_ctx + cg2 * y[n_lat:].reshape(h_ctx.shape)
        else:
            y = moe_ffn(xm2_lat.reshape(-1, d), router_w[l], router_b[l], expert_w1[l], expert_b1[l],
                        expert_w2[l], expert_b2[l])
            h_lat = h_lat + g2 * y.reshape(bsz, t, d)
    return rms_norm(h_lat, final_norm_g)
```

```python
import functools
import math

import jax
import jax.numpy as jnp
from jax import lax
from jax.experimental import pallas as pl
from jax.experimental.pallas import tpu as pltpu

F32 = jnp.float32
BF16 = jnp.bfloat16

GRID_W = 64
NORM_EPS = 1e-6
DN_HEADS = 4
DN_DK = 128
DN_DV = 128
DN_CHUNK = 64
SC_WIDTH = 512
MLA_HEADS = 4
MLA_Q_LORA = 256
MLA_KV_LORA = 128
MLA_NOPE = 128
MLA_ROPE = 64
MLA_V = 128
MLA_SCALE = (MLA_NOPE + MLA_ROPE) ** -0.5
ROPE_BASE = 10000.0
N_EXPERTS = 32
TOP_K = 4
EXPERT_FF = 1024
SWIGLU_ALPHA = 1.702
SWIGLU_LIMIT = 7.0
EXPERT_BLOCK = 256

LANE = 128
ROW_TILE = 256
HALO = 16
VMEM_LIMIT = 56 * 1024 * 1024

QKV_W = 3 * DN_HEADS * DN_DK
Z_W = DN_HEADS * DN_DV
SC3_W = 3 * SC_WIDTH
KVA_W = 3 * LANE
AB_W = LANE
IN_W = QKV_W + Z_W + SC3_W + MLA_Q_LORA + KVA_W + AB_W
QH_W = 3 * LANE
QK_W = 2 * LANE


def _cparams(sem):
    return pltpu.CompilerParams(dimension_semantics=sem, vmem_limit_bytes=VMEM_LIMIT)


def _dot(a, b):
    return jnp.dot(a, b, preferred_element_type=F32)


def _dot_nt(a, b):
    return lax.dot_general(a, b, (((1,), (1,)), ((), ())), preferred_element_type=F32)


def _rms(x, g):
    return x * lax.rsqrt(jnp.mean(x * x, axis=-1, keepdims=True) + NORM_EPS) * g


def _sigmoid(x):
    return 1.0 / (1.0 + jnp.exp(-x))


def _ada_kernel(c_ref, w_ref, b_ref, o_ref):
    cv = c_ref[...]
    s = cv * _sigmoid(cv)
    o_ref[...] = _dot(s.astype(BF16), w_ref[...].astype(BF16)) + b_ref[...]


def _ada_mods(cvec, ada_w, ada_b):
    n_layers, d, d6 = ada_w.shape
    r = cvec.shape[0]
    tn = 512
    return pl.pallas_call(
        _ada_kernel,
        grid=(n_layers, d6 // tn),
        in_specs=[pl.BlockSpec((r, d), lambda l, n: (0, 0)),
                  pl.BlockSpec((None, d, tn), lambda l, n: (l, 0, n)),
                  pl.BlockSpec((None, 1, tn), lambda l, n: (l, 0, n))],
        out_specs=pl.BlockSpec((None, r, tn), lambda l, n: (l, 0, n)),
        out_shape=jax.ShapeDtypeStruct((n_layers, r, d6), F32),
        compiler_params=_cparams(("arbitrary", "arbitrary")),
    )(cvec, ada_w, ada_b.reshape(n_layers, 1, d6))


def _inproj_kernel(h_ref, mod_ref, g1_ref, w_ref, gq_ref, wq_ref, gkv_ref, wkv_ref, cos_ref, sin_ref,
                   qkv_ref, z_ref, sc_ref, ab_ref, q_ref, k_ref, v_ref):
    x = h_ref[...]
    xm = _rms(x, g1_ref[...]) * (1.0 + mod_ref[1:2, :]) + mod_ref[0:1, :]
    p = _dot(xm.astype(BF16), w_ref[...])
    o = 0
    qkv_ref[...] = p[:, o:o + QKV_W].astype(BF16)
    o += QKV_W
    z_ref[...] = p[:, o:o + Z_W].astype(BF16)
    o += Z_W
    sc_ref[...] = p[:, o:o + SC3_W].astype(BF16)
    o += SC3_W
    qa = p[:, o:o + MLA_Q_LORA]
    o += MLA_Q_LORA
    ckv = p[:, o:o + LANE]
    kr = p[:, o + LANE:o + 2 * LANE]
    krs = p[:, o + 2 * LANE:o + 3 * LANE]
    o += KVA_W
    ab_ref[...] = p[:, o:o + AB_W]

    cos = cos_ref[...]
    sin = sin_ref[...]
    k_rope = (kr * cos + krs * sin).astype(BF16)
    qn = _rms(qa, gq_ref[...]).astype(BF16)
    qf = _dot(qn, wq_ref[...])
    kvn = _rms(ckv, gkv_ref[...]).astype(BF16)
    kv = _dot(kvn, wkv_ref[...])
    for hh in range(MLA_HEADS):
        b0 = hh * QH_W
        q_ref[hh, :, 0:LANE] = (qf[:, b0:b0 + LANE] * MLA_SCALE).astype(BF16)
        q_rope = qf[:, b0 + LANE:b0 + 2 * LANE] * cos + qf[:, b0 + 2 * LANE:b0 + 3 * LANE] * sin
        q_ref[hh, :, LANE:2 * LANE] = (q_rope * MLA_SCALE).astype(BF16)
        c0 = hh * (MLA_NOPE + MLA_V)
        k_ref[hh, :, 0:LANE] = kv[:, c0:c0 + MLA_NOPE].astype(BF16)
        k_ref[hh, :, LANE:2 * LANE] = k_rope
        v_ref[hh] = kv[:, c0 + MLA_NOPE:c0 + MLA_NOPE + MLA_V].astype(BF16)


def _inproj(h, mod, g1, w_cat, gq, wq_cat, gkv, wkv, cos_t, sin_t):
    bsz, t, d = h.shape
    tm = ROW_TILE
    nj = t // tm
    row = lambda w: pl.BlockSpec((None, tm, w), lambda b, j: (b, j, 0))
    full = lambda a: pl.BlockSpec(a.shape, lambda b, j: (0,) * a.ndim)
    head = lambda w: pl.BlockSpec((None, MLA_HEADS, tm, w), lambda b, j: (b, 0, j, 0))
    sds = jax.ShapeDtypeStruct
    return pl.pallas_call(
        _inproj_kernel,
        grid=(bsz, nj),
        in_specs=[row(d),
                  pl.BlockSpec((None, None, 6, d), lambda b, j: (b, jnp.minimum(j, 1), 0, 0)),
                  full(g1), full(w_cat), full(gq), full(wq_cat), full(gkv), full(wkv),
                  pl.BlockSpec((tm, LANE), lambda b, j: (j, 0)),
                  pl.BlockSpec((tm, LANE), lambda b, j: (j, 0))],
        out_specs=[row(QKV_W), row(Z_W), row(SC3_W), row(AB_W), head(QK_W), head(QK_W), head(MLA_V)],
        out_shape=[sds((bsz, t, QKV_W), BF16), sds((bsz, t, Z_W), BF16), sds((bsz, t, SC3_W), BF16),
                   sds((bsz, t, AB_W), F32),
                   sds((bsz, MLA_HEADS, t, QK_W), BF16), sds((bsz, MLA_HEADS, t, QK_W), BF16),
                   sds((bsz, MLA_HEADS, t, MLA_V), BF16)],
        compiler_params=_cparams(("parallel", "arbitrary")),
    )(h, mod, g1, w_cat, gq, wq_cat, gkv, wkv, cos_t, sin_t)


def _shift_rows(x, prev_row, next_row):
    tm = x.shape[0]
    rid = lax.broadcasted_iota(jnp.int32, x.shape, 0)
    xp = jnp.where(rid == 0, prev_row, pltpu.roll(x, 1, 0))
    xn = jnp.where(rid == tm - 1, next_row, pltpu.roll(x, tm - 1, 0))
    return xp, xn


def _prep_kernel(qkv_ref, qkv_p_ref, qkv_n_ref, sc_ref, sc_p_ref, sc_n_ref, ab_ref,
                 dnw_ref, scw_ref, alog_ref, dtb_ref, qkvn_ref, ysc_ref, gb_ref):
    j = pl.program_id(1)
    nj = pl.num_programs(1)
    pv = (j >= 2).astype(F32)
    nv = jnp.logical_and(j >= 1, j <= nj - 2).astype(F32)

    x = qkv_ref[...].astype(F32)
    xp, xn = _shift_rows(x, qkv_p_ref[HALO - 1:HALO, :].astype(F32) * pv, qkv_n_ref[0:1, :].astype(F32) * nv)
    y = xp * dnw_ref[0:1, :] + x * dnw_ref[1:2, :] + xn * dnw_ref[2:3, :]
    y = y * _sigmoid(y)
    nqk = DN_HEADS * DN_DK
    for g in range(2 * DN_HEADS):
        yy = y[:, g * DN_DK:(g + 1) * DN_DK]
        yy = yy * lax.rsqrt(jnp.sum(yy * yy, axis=-1, keepdims=True) + 1e-6)
        if g < DN_HEADS:
            yy = yy * (DN_DK ** -0.5)
        qkvn_ref[:, g * DN_DK:(g + 1) * DN_DK] = yy.astype(BF16)
    qkvn_ref[:, 2 * nqk:] = y[:, 2 * nqk:].astype(BF16)

    s = sc_ref[...].astype(F32)
    sp = sc_p_ref[HALO - 1:HALO, :].astype(F32) * pv
    sn = sc_n_ref[0:1, :].astype(F32) * nv
    w = SC_WIDTH
    u = s[:, 2 * w:3 * w] * s[:, 0:w]
    up, un = _shift_rows(u, sp[:, 2 * w:3 * w] * sp[:, 0:w], sn[:, 2 * w:3 * w] * sn[:, 0:w])
    conv = up * scw_ref[0:1, :] + u * scw_ref[1:2, :] + un * scw_ref[2:3, :]
    ysc_ref[...] = (s[:, w:2 * w] * conv).astype(BF16)

    ab = ab_ref[...]
    sp_arg = ab + dtb_ref[...]
    softplus = jnp.maximum(sp_arg, 0.0) + jnp.log1p(jnp.exp(-jnp.abs(sp_arg)))
    gval = -jnp.exp(alog_ref[...]) * softplus
    lane = lax.broadcasted_iota(jnp.int32, ab.shape, 1)
    gb_ref[...] = jnp.where(lane < 2 * DN_HEADS, gval, _sigmoid(ab))


def _prep(qkv, sc, ab, dn_conv_w, sc_conv_w, alog_row, dtb_row):
    bsz, t, _ = qkv.shape
    tm = ROW_TILE
    nj = t // tm
    hb = tm // HALO
    nh = t // HALO
    row = lambda w: pl.BlockSpec((None, tm, w), lambda b, j: (b, j, 0))
    prev = lambda w: pl.BlockSpec((None, HALO, w), lambda b, j: (b, jnp.maximum(j * hb - 1, 0), 0))
    nxt = lambda w: pl.BlockSpec((None, HALO, w), lambda b, j: (b, jnp.minimum((j + 1) * hb, nh - 1), 0))
    full = lambda a: pl.BlockSpec(a.shape, lambda b, j: (0,) * a.ndim)
    sds = jax.ShapeDtypeStruct
    return pl.pallas_call(
        _prep_kernel,
        grid=(bsz, nj),
        in_specs=[row(QKV_W), prev(QKV_W), nxt(QKV_W), row(SC3_W), prev(SC3_W), nxt(SC3_W), row(AB_W),
                  full(dn_conv_w), full(sc_conv_w), full(alog_row), full(dtb_row)],
        out_specs=[row(QKV_W), row(SC_WIDTH), row(AB_W)],
        out_shape=[sds((bsz, t, QKV_W), BF16), sds((bsz, t, SC_WIDTH), BF16), sds((bsz, t, AB_W), F32)],
        compiler_params=_cparams(("parallel", "arbitrary")),
    )(qkv, qkv, qkv, sc, sc, sc, ab, dn_conv_w, sc_conv_w, alog_row, dtb_row)


def _stack_heads(x):
    return jnp.concatenate([x[:, hh * LANE:(hh + 1) * LANE] for hh in range(DN_HEADS)], axis=0)


def _stack_cols(x, c0):
    return jnp.concatenate([x[:, c0 + hh:c0 + hh + 1] for hh in range(DN_HEADS)], axis=0)


def _dn_direction(d, x, gbt, tri, m_causal, m_strict, eye, s_ref):
    c = DN_CHUNK
    nqk = DN_HEADS * DN_DK
    qst = _stack_heads(x[:, 0:nqk])
    kst = _stack_heads(x[:, nqk:2 * nqk])
    vst = _stack_heads(x[:, 2 * nqk:])
    gc_all = jnp.dot(tri, gbt, preferred_element_type=F32, precision=lax.Precision.HIGHEST)
    gc = _stack_cols(gc_all, d * DN_HEADS)
    beta = _stack_cols(gbt, (2 + d) * DN_HEADS)
    last = c - 1 if d == 0 else 0
    gl_heads = [gc_all[last:last + 1, d * DN_HEADS + hh:d * DN_HEADS + hh + 1] for hh in range(DN_HEADS)]
    gl = jnp.concatenate([jnp.broadcast_to(g1, (c, 1)) for g1 in gl_heads], axis=0)

    hc = DN_HEADS * c
    gmat = jnp.broadcast_to(gc, (hc, hc))
    dec = jnp.exp(jnp.minimum(gmat - gmat.T, 0.0))
    dec_c = jnp.where(m_causal > 0.5, dec, 0.0)
    dec_s = jnp.where(m_strict > 0.5, dec, 0.0)

    kf = kst.astype(F32)
    kb = kf * beta
    lmat = _dot_nt(kb.astype(BF16), kst) * dec_s
    xpow = -lmat
    tinv = eye + xpow
    for _ in range(int(math.log2(c)) - 1):
        xpow = jnp.dot(xpow, xpow, preferred_element_type=F32, precision=lax.Precision.HIGHEST)
        tinv = tinv + jnp.dot(tinv, xpow, preferred_element_type=F32, precision=lax.Precision.HIGHEST)
    eg = jnp.exp(gc)
    rhs = jnp.concatenate([vst.astype(F32) * beta, kb * eg], axis=1)
    sol = jnp.dot(tinv, rhs, preferred_element_type=F32, precision=lax.Precision.HIGHEST)
    u = sol[:, 0:DN_DV]
    w = sol[:, DN_DV:]
    intra = _dot_nt(qst, kst) * dec_c
    qd = qst.astype(F32) * eg
    kd = kf * jnp.exp(gl - gc)

    s_cat = s_ref[...]
    wq = jnp.concatenate([w, qd], axis=0).astype(BF16)
    full = _dot(wq, s_cat.astype(BF16))
    ws = jnp.concatenate([full[hh * c:(hh + 1) * c, hh * DN_DV:(hh + 1) * DN_DV] for hh in range(DN_HEADS)], axis=0)
    qs = jnp.concatenate([full[hc + hh * c:hc + (hh + 1) * c, hh * DN_DV:(hh + 1) * DN_DV]
                          for hh in range(DN_HEADS)], axis=0)
    vnew = u - ws
    o = qs + _dot(intra.astype(BF16), vnew.astype(BF16))
    rhead = lax.broadcasted_iota(jnp.int32, (hc, DN_DV), 0) // c
    vblk = jnp.concatenate([jnp.where(rhead == hh, vnew, 0.0) for hh in range(DN_HEADS)], axis=1).astype(BF16)
    gt = jnp.concatenate([jnp.broadcast_to(jnp.exp(g1), (1, DN_DV)) for g1 in gl_heads], axis=1)
    s_ref[...] = s_cat * gt + _dot(kd.T.astype(BF16), vblk)
    return jnp.concatenate([o[hh * c:(hh + 1) * c] for hh in range(DN_HEADS)], axis=1)


def _dn_kernel(xf_ref, gf_ref, xb_ref, gb_ref, trif_ref, trib_ref, mcf_ref, msf_ref, mcb_ref, msb_ref, eye_ref,
               of_ref, ob_ref, s_ref):
    @pl.when(pl.program_id(1) == 0)
    def _():
        s_ref[...] = jnp.zeros_like(s_ref)

    eye = eye_ref[...]
    of_ref[...] = _dn_direction(0, xf_ref[...], gf_ref[...], trif_ref[...], mcf_ref[...], msf_ref[...], eye,
                                s_ref.at[0]).astype(BF16)
    ob_ref[...] = _dn_direction(1, xb_ref[...], gb_ref[...], trib_ref[...], mcb_ref[...], msb_ref[...], eye,
                                s_ref.at[1]).astype(BF16)


def _deltanet(qkvn, gb, n_ctx_chunks):
    bsz, t, _ = qkvn.shape
    c = DN_CHUNK
    nc = t // c
    hc = DN_HEADS * c
    bmap = lambda i: jnp.where(i < n_ctx_chunks, n_ctx_chunks - 1 - i, nc + n_ctx_chunks - 1 - i)
    ii = jnp.arange(c)
    tri_f = (ii[None, :] <= ii[:, None]).astype(F32)
    tri_b = (ii[None, :] >= ii[:, None]).astype(F32)
    r = jnp.arange(hc)
    same = (r[:, None] // c) == (r[None, :] // c)
    pi, pj = r[:, None] % c, r[None, :] % c
    mc_f = (same & (pj <= pi)).astype(F32)
    ms_f = (same & (pj < pi)).astype(F32)
    mc_b = (same & (pj >= pi)).astype(F32)
    ms_b = (same & (pj > pi)).astype(F32)
    eye = jnp.eye(hc, dtype=F32)
    full = lambda a: pl.BlockSpec(a.shape, lambda b, i: (0,) * a.ndim)
    fw = lambda w: pl.BlockSpec((None, c, w), lambda b, i: (b, i, 0))
    bw = lambda w: pl.BlockSpec((None, c, w), lambda b, i: (b, bmap(i), 0))
    ow = DN_HEADS * DN_DV
    sds = jax.ShapeDtypeStruct
    return pl.pallas_call(
        _dn_kernel,
        grid=(bsz, nc),
        in_specs=[fw(QKV_W), fw(AB_W), bw(QKV_W), bw(AB_W), full(tri_f), full(tri_b),
                  full(mc_f), full(ms_f), full(mc_b), full(ms_b), full(eye)],
        out_specs=[fw(ow), bw(ow)],
        out_shape=[sds((bsz, t, ow), BF16), sds((bsz, t, ow), BF16)],
        scratch_shapes=[pltpu.VMEM((2, DN_DK, DN_HEADS * DN_DV), F32)],
        compiler_params=_cparams(("arbitrary", "arbitrary")),
    )(qkvn, gb, qkvn, gb, tri_f, tri_b, mc_f, ms_f, mc_b, ms_b, eye)


def _softmax_av(q, k, v):
    s = _dot_nt(q, k)
    m = jnp.max(s, axis=-1, keepdims=True)
    p = jnp.exp(s - m)
    l = jnp.sum(p, axis=-1, keepdims=True)
    return _dot(p.astype(BF16), v) / l


def _mla_kernel(q_ref, k_ref, v_ref, o_ref):
    j = pl.program_id(2)
    tc = q_ref.shape[0]

    @pl.when(j == 0)
    def _():
        o_ref[...] = _softmax_av(q_ref[...], k_ref[0:tc, :], v_ref[0:tc, :]).astype(BF16)

    @pl.when(j > 0)
    def _():
        o_ref[...] = _softmax_av(q_ref[...], k_ref[...], v_ref[...]).astype(BF16)


def _mla(q, k, v):
    bsz, nh, t, _ = q.shape
    tq = ROW_TILE
    return pl.pallas_call(
        _mla_kernel,
        grid=(bsz, nh, t // tq),
        in_specs=[pl.BlockSpec((None, None, tq, QK_W), lambda b, hh, j: (b, hh, j, 0)),
                  pl.BlockSpec((None, None, t, QK_W), lambda b, hh, j: (b, hh, 0, 0)),
                  pl.BlockSpec((None, None, t, MLA_V), lambda b, hh, j: (b, hh, 0, 0))],
        out_specs=pl.BlockSpec((None, tq, MLA_V), lambda b, hh, j: (b, j, hh)),
        out_shape=jax.ShapeDtypeStruct((bsz, t, nh * MLA_V), BF16),
        compiler_params=_cparams(("parallel", "parallel", "arbitrary")),
    )(q, k, v)


def _merge_kernel(h_ref, mod_ref, g1_ref, of_ref, ob_ref, z_ref, dng_ref, ysc_ref, ymla_ref,
                  wg_ref, bg_ref, wdn_ref, wsc_ref, wmla_ref, wout_ref, g2_ref, rw_ref, rwh_ref, rb_ref,
                  hn_ref, xm2_ref, ti_ref, tg_ref):
    d = h_ref.shape[-1]
    x = h_ref[...]
    xm = _rms(x, g1_ref[...]) * (1.0 + mod_ref[1:2, :]) + mod_ref[0:1, :]
    gates = _sigmoid(_dot(xm.astype(BF16), wg_ref[...]) + bg_ref[...])
    o = of_ref[...].astype(F32) + ob_ref[...].astype(F32)
    z = z_ref[...].astype(F32)
    parts = []
    for hh in range(DN_HEADS):
        oh = o[:, hh * DN_DV:(hh + 1) * DN_DV]
        zh = z[:, hh * DN_DV:(hh + 1) * DN_DV]
        parts.append(_rms(oh, dng_ref[...]) * (zh * _sigmoid(zh)))
    ydn = jnp.concatenate(parts, axis=1).astype(BF16)
    merged = (gates[:, 0:d] * _dot(ydn, wdn_ref[...]) + gates[:, d:2 * d] * _dot(ysc_ref[...], wsc_ref[...])
              + gates[:, 2 * d:3 * d] * _dot(ymla_ref[...], wmla_ref[...]))
    hn = x + mod_ref[2:3, :] * _dot(merged.astype(BF16), wout_ref[...])
    hn_ref[...] = hn
    xm2 = _rms(hn, g2_ref[...]) * (1.0 + mod_ref[4:5, :]) + mod_ref[3:4, :]
    xm2_ref[...] = xm2

    xh = xm2.astype(BF16)
    xl = (xm2 - xh.astype(F32)).astype(BF16)
    a = _dot(xh, rw_ref[...])
    logits = a[:, 0:LANE] + a[:, LANE:2 * LANE] + _dot(xl, rwh_ref[...]) + rb_ref[...]
    lane = lax.broadcasted_iota(jnp.int32, logits.shape, 1)
    work = jnp.where(lane < N_EXPERTS, logits, -jnp.inf)
    ti = jnp.zeros(logits.shape, jnp.int32)
    tv = jnp.zeros(logits.shape, F32)
    vals = []
    for kk in range(TOP_K):
        m = jnp.max(work, axis=-1, keepdims=True)
        idx = jnp.min(jnp.where(work == m, lane, LANE), axis=-1, keepdims=True)
        work = jnp.where(lane == idx, -jnp.inf, work)
        ti = jnp.where(lane == kk, idx, ti)
        vals.append(m)
    es = [jnp.exp(vv - vals[0]) for vv in vals]
    tot = es[0] + es[1] + es[2] + es[3]
    for kk in range(TOP_K):
        tv = jnp.where(lane == kk, es[kk] / tot, tv)
    ti_ref[...] = ti
    tg_ref[...] = tv


def _merge(h, mod, g1, o_f, o_b, z, dng, ysc, ymla, wg, bg, wdn, wsc, wmla, wout, g2, rw_cat, rw_hi, rb):
    bsz, t, d = h.shape
    tm = ROW_TILE
    row = lambda w: pl.BlockSpec((None, tm, w), lambda b, j: (b, j, 0))
    full = lambda a: pl.BlockSpec(a.shape, lambda b, j: (0,) * a.ndim)
    ow = DN_HEADS * DN_DV
    sds = jax.ShapeDtypeStruct
    return pl.pallas_call(
        _merge_kernel,
        grid=(bsz, t // tm),
        in_specs=[row(d), pl.BlockSpec((None, None, 6, d), lambda b, j: (b, jnp.minimum(j, 1), 0, 0)), full(g1),
                  row(ow), row(ow), row(Z_W), full(dng), row(SC_WIDTH), row(MLA_HEADS * MLA_V),
                  full(wg), full(bg), full(wdn), full(wsc), full(wmla), full(wout), full(g2),
                  full(rw_cat), full(rw_hi), full(rb)],
        out_specs=[row(d), row(d), row(LANE), row(LANE)],
        out_shape=[sds((bsz, t, d), F32), sds((bsz, t, d), F32), sds((bsz, t, LANE), jnp.int32),
                   sds((bsz, t, LANE), F32)],
        compiler_params=_cparams(("parallel", "arbitrary")),
    )(h, mod, g1, o_f, o_b, z, dng, ysc, ymla, wg, bg, wdn, wsc, wmla, wout, g2, rw_cat, rw_hi, rb)


def _gather_rows(idx_ref, n_idx_row, src_hbm, dst_ref, sem, n_rows):
    def issue(r, carry):
        pltpu.make_async_copy(src_hbm.at[pl.ds(idx_ref[n_idx_row, r], 1)], dst_ref.at[pl.ds(r, 1)], sem).start()
        return carry
    lax.fori_loop(0, n_rows, issue, 0)


def _wait_rows(src_hbm, dst_ref, sem, n_rows):
    pltpu.make_async_copy(src_hbm.at[pl.ds(0, n_rows)], dst_ref, sem).wait()


def _expert_kernel(be_ref, nreal_ref, tok_ref, x_hbm, w1g_ref, w1l_ref, b1g_ref, b1l_ref, w2_ref, b2_ref,
                   y_ref, xbuf, sem):
    i = pl.program_id(0)

    @pl.when(i < nreal_ref[0])
    def _():
        _gather_rows(tok_ref, 0, x_hbm, xbuf, sem, EXPERT_BLOCK)
        _wait_rows(x_hbm, xbuf, sem, EXPERT_BLOCK)
        xb = xbuf[...].astype(BF16)
        glu = jnp.minimum(_dot(xb, w1g_ref[...]) + b1g_ref[...], SWIGLU_LIMIT)
        lin = jnp.clip(_dot(xb, w1l_ref[...]) + b1l_ref[...], -SWIGLU_LIMIT, SWIGLU_LIMIT)
        act = glu * _sigmoid(SWIGLU_ALPHA * glu) * (lin + 1.0)
        y_ref[...] = _dot(act.astype(BF16), w2_ref[...]) + b2_ref[...]

    @pl.when(i >= nreal_ref[0])
    def _():
        y_ref[...] = jnp.zeros_like(y_ref)


def _experts(block_expert, n_real, row_token, x2d, w1g, w1l, b1g, b1l, w2, b2):
    n_blocks = block_expert.shape[0]
    d = x2d.shape[1]
    f = w1g.shape[2]
    eb = EXPERT_BLOCK
    wspec = lambda k, n: pl.BlockSpec((None, k, n), lambda i, be, nr: (be[i], 0, 0))
    gs = pltpu.PrefetchScalarGridSpec(
        num_scalar_prefetch=2,
        grid=(n_blocks,),
        in_specs=[pl.BlockSpec((None, 1, eb), lambda i, be, nr: (i, 0, 0), memory_space=pltpu.SMEM),
                  pl.BlockSpec(memory_space=pl.ANY),
                  wspec(d, f), wspec(d, f), wspec(1, f), wspec(1, f), wspec(f, d), wspec(1, d)],
        out_specs=pl.BlockSpec((eb, d), lambda i, be, nr: (i, 0)),
        scratch_shapes=[pltpu.VMEM((eb, d), F32), pltpu.SemaphoreType.DMA(())],
    )
    return pl.pallas_call(
        _expert_kernel,
        grid_spec=gs,
        out_shape=jax.ShapeDtypeStruct((n_blocks * eb, d), F32),
        compiler_params=_cparams(("arbitrary",)),
    )(block_expert, n_real, row_token.reshape(n_blocks, 1, eb), x2d, w1g, w1l, b1g, b1l, w2, b2)


def _combine_kernel(final, pos_ref, y_hbm, h_ref, mod_ref, tg_ref, fg_ref, o_ref, ybuf, sem):
    tm = h_ref.shape[0]
    for kk in range(TOP_K):
        _gather_rows(pos_ref, kk, y_hbm, ybuf.at[kk], sem, tm)
    for kk in range(TOP_K):
        _wait_rows(y_hbm, ybuf.at[kk], sem, tm)
    tg = tg_ref[...]
    y = tg[:, 0:1] * ybuf[0]
    for kk in range(1, TOP_K):
        y = y + tg[:, kk:kk + 1] * ybuf[kk]
    hn = h_ref[...] + mod_ref[5:6, :] * y
    if final:
        hn = _rms(hn, fg_ref[...])
    o_ref[...] = hn


def _combine(pos, y_rows, h, mod, tg, fg, final):
    bsz, t, d = h.shape
    tm = ROW_TILE
    nj = t // tm
    row = lambda w: pl.BlockSpec((None, tm, w), lambda b, j: (b, j, 0))
    return pl.pallas_call(
        functools.partial(_combine_kernel, final),
        grid=(bsz, nj),
        in_specs=[pl.BlockSpec((None, None, TOP_K, tm), lambda b, j: (b, j, 0, 0), memory_space=pltpu.SMEM),
                  pl.BlockSpec(memory_space=pl.ANY),
                  row(d), pl.BlockSpec((None, None, 6, d), lambda b, j: (b, jnp.minimum(j, 1), 0, 0)),
                  row(LANE), pl.BlockSpec(fg.shape, lambda b, j: (0, 0))],
        out_specs=row(d),
        out_shape=jax.ShapeDtypeStruct((bsz, t, d), F32),
        scratch_shapes=[pltpu.VMEM((TOP_K, tm, d), F32), pltpu.SemaphoreType.DMA(())],
        compiler_params=_cparams(("arbitrary", "arbitrary")),
    )(pos.reshape(bsz, nj, TOP_K, tm), y_rows, h, mod, tg, fg)


def _route(top_idx):
    n_tok = top_idx.shape[0]
    n_assign = n_tok * TOP_K
    eb = EXPERT_BLOCK
    flat_e = top_idx.reshape(n_assign)
    order = jnp.argsort(flat_e)
    sorted_e = flat_e[order]
    counts = jnp.bincount(flat_e, length=N_EXPERTS)
    padded = (counts + eb - 1) // eb * eb
    pad_end = jnp.cumsum(padded)
    pad_start = pad_end - padded
    start = jnp.cumsum(counts) - counts
    dest = (pad_start[sorted_e] + jnp.arange(n_assign) - start[sorted_e]).astype(jnp.int32)
    n_blocks = -(-n_assign // eb) + N_EXPERTS
    row_token = jnp.zeros((n_blocks * eb,), jnp.int32).at[dest].set((order // TOP_K).astype(jnp.int32))
    pos = jnp.zeros((n_assign,), jnp.int32).at[order].set(dest)
    block_expert = jnp.minimum(jnp.searchsorted(pad_end, jnp.arange(n_blocks) * eb, side='right'),
                               N_EXPERTS - 1).astype(jnp.int32)
    n_real = (pad_end[-1] // eb).astype(jnp.int32).reshape(1)
    return row_token, pos.reshape(n_tok, TOP_K), block_expert, n_real


def _rope_tables(t_lat, t_ctx):
    rows = t_lat // GRID_W
    row = jnp.repeat(jnp.arange(rows, dtype=F32), GRID_W)
    col = jnp.tile(jnp.arange(GRID_W, dtype=F32), rows)
    axis_dims = MLA_ROPE // 2
    inv = ROPE_BASE ** (-jnp.arange(0, axis_dims, 2, dtype=F32) / axis_dims)
    ar, ac = row[:, None] * inv, col[:, None] * inv
    cos64 = jnp.concatenate([jnp.cos(ar), jnp.cos(ar), jnp.cos(ac), jnp.cos(ac)], axis=-1)
    sin64 = jnp.concatenate([-jnp.sin(ar), jnp.sin(ar), -jnp.sin(ac), jnp.sin(ac)], axis=-1)
    pad = jnp.zeros((t_lat, LANE - MLA_ROPE), F32)
    cos_l = jnp.concatenate([cos64, pad], axis=-1)
    sin_l = jnp.concatenate([sin64, pad], axis=-1)
    cos_c = jnp.concatenate([jnp.ones((t_ctx, MLA_ROPE), F32), jnp.zeros((t_ctx, LANE - MLA_ROPE), F32)], axis=-1)
    sin_c = jnp.zeros((t_ctx, LANE), F32)
    return jnp.concatenate([cos_c, cos_l], axis=0), jnp.concatenate([sin_c, sin_l], axis=0)


def _rope_swap_perm():
    half = MLA_ROPE // 4
    perm = []
    for a in range(2):
        base = a * 2 * half
        perm += list(range(base + half, base + 2 * half)) + list(range(base, base + half))
    return jnp.array(perm, jnp.int32)


def _pad_cols(w, width):
    return jnp.concatenate([w, jnp.zeros(w.shape[:-1] + (width - w.shape[-1],), w.dtype)], axis=-1)


def _layer_weights(l, w_in, dn_a_log, dn_dt_bias, mla_w_qb, router_w, router_b, expert_w1, expert_b1):
    nqk = DN_HEADS * DN_DK
    wi = w_in[l]
    o = 0
    dn_qkv = wi[:, o:o + QKV_W]; o += QKV_W
    dn_z = wi[:, o:o + Z_W]; o += Z_W
    dn_ab = wi[:, o:o + 4 * DN_HEADS]; o += 4 * DN_HEADS
    sc3 = wi[:, o:o + SC3_W]; o += SC3_W
    qa = wi[:, o:o + MLA_Q_LORA]; o += MLA_Q_LORA
    ckv = wi[:, o:o + MLA_KV_LORA]; o += MLA_KV_LORA
    kr = wi[:, o:o + MLA_ROPE]
    perm = _rope_swap_perm()
    w_cat = jnp.concatenate([dn_qkv, dn_z, sc3, qa, ckv, _pad_cols(kr, LANE), _pad_cols(kr[:, perm], LANE),
                             _pad_cols(dn_ab, LANE)], axis=-1).astype(BF16)
    dq = MLA_NOPE + MLA_ROPE
    wq = mla_w_qb[l].reshape(MLA_Q_LORA, MLA_HEADS, dq)
    wq_rope = wq[:, :, MLA_NOPE:]
    wq_cat = jnp.concatenate([wq[:, :, :MLA_NOPE], _pad_cols(wq_rope, LANE), _pad_cols(wq_rope[:, :, perm], LANE)],
                             axis=-1).reshape(MLA_Q_LORA, MLA_HEADS * QH_W).astype(BF16)
    alog_row = _pad_cols(jnp.concatenate([dn_a_log[l].reshape(1, -1), jnp.zeros((1, 2 * DN_HEADS), F32)], -1), LANE)
    dtb_row = _pad_cols(jnp.concatenate([dn_dt_bias[l].reshape(1, -1), jnp.zeros((1, 2 * DN_HEADS), F32)], -1), LANE)
    rw = _pad_cols(router_w[l], LANE)
    rw_hi = rw.astype(BF16)
    rw_lo = (rw - rw_hi.astype(F32)).astype(BF16)
    rw_cat = jnp.concatenate([rw_hi, rw_lo], axis=-1)
    rb = jnp.concatenate([router_b[l], jnp.full((LANE - N_EXPERTS,), 0.0, F32)]).reshape(1, LANE)
    w1 = expert_w1[l]
    b1 = expert_b1[l]
    w1g, w1l = w1[:, :, 0::2].astype(BF16), w1[:, :, 1::2].astype(BF16)
    b1g, b1l = b1[:, None, 0::2], b1[:, None, 1::2]
    return w_cat, wq_cat, alog_row, dtb_row, rw_cat, rw_hi, rb, w1g, w1l, b1g, b1l


def kernel(x, c, ctx, c_ctx, ada_w, ada_b, norm1_g, norm2_g, w_in, dn_conv_w, dn_a_log, dn_dt_bias, dn_norm_g, sc_conv_w, mla_q_norm_g, mla_w_qb, mla_kv_norm_g, mla_w_kvb, w_branch_gate, b_branch_gate, w_branch_dn, w_branch_sc, w_branch_mla, w_out, router_w, router_b, expert_w1, expert_b1, expert_w2, expert_b2, final_norm_g):
    bsz, t_lat, d = x.shape
    t_ctx = ctx.shape[1]
    depth = ada_w.shape[0]
    assert t_ctx == ROW_TILE and t_lat % ROW_TILE == 0 and t_lat % GRID_W == 0
    t = t_ctx + t_lat
    n_tok = bsz * t

    cos_t, sin_t = _rope_tables(t_lat, t_ctx)
    n_mod_rows = -(-(bsz + 1) // 8) * 8
    cvec = jnp.concatenate([c, c_ctx[None], jnp.zeros((n_mod_rows - bsz - 1, d), F32)], axis=0)
    mods = _ada_mods(cvec, ada_w, ada_b)

    h = jnp.concatenate([ctx, x], axis=1)
    row2 = lambda v: v.reshape(1, -1)
    for l in range(depth):
        mod_lat = mods[l, :bsz].reshape(bsz, 1, 6, d)
        mod_ctx = jnp.broadcast_to(mods[l, bsz].reshape(1, 1, 6, d), (bsz, 1, 6, d))
        mod = jnp.concatenate([mod_ctx, mod_lat], axis=1)
        (w_cat, wq_cat, alog_row, dtb_row, rw_cat, rw_hi, rb, w1g, w1l, b1g, b1l) = _layer_weights(
            l, w_in, dn_a_log, dn_dt_bias, mla_w_qb, router_w, router_b, expert_w1, expert_b1)

        qkv, z, sc, ab, q, k, v = _inproj(h, mod, row2(norm1_g[l]), w_cat, row2(mla_q_norm_g[l]), wq_cat,
                                          row2(mla_kv_norm_g[l]), mla_w_kvb[l].astype(BF16), cos_t, sin_t)
        qkvn, ysc, gb = _prep(qkv, sc, ab, dn_conv_w[l], sc_conv_w[l], alog_row, dtb_row)
        o_f, o_b = _deltanet(qkvn, gb, t_ctx // DN_CHUNK)
        ymla = _mla(q, k, v)
        hn, xm2, ti, tg = _merge(h, mod, row2(norm1_g[l]), o_f, o_b, z, row2(dn_norm_g[l]), ysc, ymla,
                                 w_branch_gate[l].astype(BF16), row2(b_branch_gate[l]),
                                 w_branch_dn[l].astype(BF16), w_branch_sc[l].astype(BF16),
                                 w_branch_mla[l].astype(BF16), w_out[l].astype(BF16), row2(norm2_g[l]),
                                 rw_cat, rw_hi, rb)
        row_token, pos, block_expert, n_real = _route(ti.reshape(n_tok, LANE)[:, :TOP_K])
        y_rows = _experts(block_expert, n_real, row_token, xm2.reshape(n_tok, d), w1g, w1l, b1g, b1l,
                          expert_w2[l].astype(BF16), expert_b2[l][:, None, :])
        pos_t = pos.reshape(bsz, t // ROW_TILE, ROW_TILE, TOP_K).transpose(0, 1, 3, 2)
        h = _combine(pos_t, y_rows, hn, mod, tg, row2(final_norm_g), final=(l == depth - 1))
    return h[:, t_ctx:, :]
```

```python
import functools
import math

import jax
import jax.numpy as jnp
from jax import lax
from jax.experimental import pallas as pl
from jax.experimental.pallas import tpu as pltpu

F32 = jnp.float32
BF16 = jnp.bfloat16

GRID_W = 64
NORM_EPS = 1e-6
DN_HEADS = 4
DN_DK = 128
DN_DV = 128
DN_CHUNK = 64
SC_WIDTH = 512
MLA_HEADS = 4
MLA_Q_LORA = 256
MLA_KV_LORA = 128
MLA_NOPE = 128
MLA_ROPE = 64
MLA_V = 128
MLA_SCALE = (MLA_NOPE + MLA_ROPE) ** -0.5
ROPE_BASE = 10000.0
N_EXPERTS = 32
TOP_K = 4
EXPERT_FF = 1024
SWIGLU_ALPHA = 1.702
SWIGLU_LIMIT = 7.0
EXPERT_BLOCK = 256

LANE = 128
ROW_TILE = 256
HALO = 16
VMEM_LIMIT = 56 * 1024 * 1024

QKV_W = 3 * DN_HEADS * DN_DK
Z_W = DN_HEADS * DN_DV
SC3_W = 3 * SC_WIDTH
KVA_W = 3 * LANE
AB_W = LANE
IN_W = QKV_W + Z_W + SC3_W + MLA_Q_LORA + KVA_W + AB_W
QH_W = 3 * LANE
QK_W = 2 * LANE


def _cparams(sem):
    return pltpu.CompilerParams(dimension_semantics=sem, vmem_limit_bytes=VMEM_LIMIT)


def _dot(a, b):
    return jnp.dot(a, b, preferred_element_type=F32)


def _dot_nt(a, b):
    return lax.dot_general(a, b, (((1,), (1,)), ((), ())), preferred_element_type=F32)


def _rms(x, g):
    return x * lax.rsqrt(jnp.mean(x * x, axis=-1, keepdims=True) + NORM_EPS) * g


def _sigmoid(x):
    return 1.0 / (1.0 + jnp.exp(-x))


def _ada_kernel(c_ref, w_ref, b_ref, o_ref):
    cv = c_ref[...]
    s = cv * _sigmoid(cv)
    o_ref[...] = _dot(s.astype(BF16), w_ref[...].astype(BF16)) + b_ref[...]


def _ada_mods(cvec, ada_w, ada_b):
    n_layers, d, d6 = ada_w.shape
    r = cvec.shape[0]
    tn = 512
    return pl.pallas_call(
        _ada_kernel,
        grid=(n_layers, d6 // tn),
        in_specs=[pl.BlockSpec((r, d), lambda l, n: (0, 0)),
                  pl.BlockSpec((None, d, tn), lambda l, n: (l, 0, n)),
                  pl.BlockSpec((None, 1, tn), lambda l, n: (l, 0, n))],
        out_specs=pl.BlockSpec((None, r, tn), lambda l, n: (l, 0, n)),
        out_shape=jax.ShapeDtypeStruct((n_layers, r, d6), F32),
        compiler_params=_cparams(("arbitrary", "arbitrary")),
    )(cvec, ada_w, ada_b.reshape(n_layers, 1, d6))


def _inproj_kernel(h_ref, mod_ref, g1_ref, w_ref, gq_ref, wq_ref, gkv_ref, wkv_ref, cos_ref, sin_ref,
                   qkv_ref, z_ref, sc_ref, ab_ref, q_ref, k_ref, v_ref):
    x = h_ref[...]
    xm = _rms(x, g1_ref[...]) * (1.0 + mod_ref[1:2, :]) + mod_ref[0:1, :]
    p = _dot(xm.astype(BF16), w_ref[...])
    o = 0
    qkv_ref[...] = p[:, o:o + QKV_W].astype(BF16)
    o += QKV_W
    z_ref[...] = p[:, o:o + Z_W].astype(BF16)
    o += Z_W
    sc_ref[...] = p[:, o:o + SC3_W].astype(BF16)
    o += SC3_W
    qa = p[:, o:o + MLA_Q_LORA]
    o += MLA_Q_LORA
    ckv = p[:, o:o + LANE]
    kr = p[:, o + LANE:o + 2 * LANE]
    krs = p[:, o + 2 * LANE:o + 3 * LANE]
    o += KVA_W
    ab_ref[...] = p[:, o:o + AB_W]

    cos = cos_ref[...]
    sin = sin_ref[...]
    k_rope = (kr * cos + krs * sin).astype(BF16)
    qn = _rms(qa, gq_ref[...]).astype(BF16)
    qf = _dot(qn, wq_ref[...])
    kvn = _rms(ckv, gkv_ref[...]).astype(BF16)
    kv = _dot(kvn, wkv_ref[...])
    for hh in range(MLA_HEADS):
        b0 = hh * QH_W
        q_ref[hh, :, 0:LANE] = (qf[:, b0:b0 + LANE] * MLA_SCALE).astype(BF16)
        q_rope = qf[:, b0 + LANE:b0 + 2 * LANE] * cos + qf[:, b0 + 2 * LANE:b0 + 3 * LANE] * sin
        q_ref[hh, :, LANE:2 * LANE] = (q_rope * MLA_SCALE).astype(BF16)
        c0 = hh * (MLA_NOPE + MLA_V)
        k_ref[hh, :, 0:LANE] = kv[:, c0:c0 + MLA_NOPE].astype(BF16)
        k_ref[hh, :, LANE:2 * LANE] = k_rope
        v_ref[hh] = kv[:, c0 + MLA_NOPE:c0 + MLA_NOPE + MLA_V].astype(BF16)


def _inproj(h, mod, g1, w_cat, gq, wq_cat, gkv, wkv, cos_t, sin_t):
    bsz, t, d = h.shape
    tm = ROW_TILE
    nj = t // tm
    row = lambda w: pl.BlockSpec((None, tm, w), lambda b, j: (b, j, 0))
    full = lambda a: pl.BlockSpec(a.shape, lambda b, j: (0,) * a.ndim)
    head = lambda w: pl.BlockSpec((None, MLA_HEADS, tm, w), lambda b, j: (b, 0, j, 0))
    sds = jax.ShapeDtypeStruct
    return pl.pallas_call(
        _inproj_kernel,
        grid=(bsz, nj),
        in_specs=[row(d),
                  pl.BlockSpec((None, None, 6, d), lambda b, j: (b, jnp.minimum(j, 1), 0, 0)),
                  full(g1), full(w_cat), full(gq), full(wq_cat), full(gkv), full(wkv),
                  pl.BlockSpec((tm, LANE), lambda b, j: (j, 0)),
                  pl.BlockSpec((tm, LANE), lambda b, j: (j, 0))],
        out_specs=[row(QKV_W), row(Z_W), row(SC3_W), row(AB_W), head(QK_W), head(QK_W), head(MLA_V)],
        out_shape=[sds((bsz, t, QKV_W), BF16), sds((bsz, t, Z_W), BF16), sds((bsz, t, SC3_W), BF16),
                   sds((bsz, t, AB_W), F32),
                   sds((bsz, MLA_HEADS, t, QK_W), BF16), sds((bsz, MLA_HEADS, t, QK_W), BF16),
                   sds((bsz, MLA_HEADS, t, MLA_V), BF16)],
        compiler_params=_cparams(("parallel", "arbitrary")),
    )(h, mod, g1, w_cat, gq, wq_cat, gkv, wkv, cos_t, sin_t)


def _shift_rows(x, prev_row, next_row):
    tm = x.shape[0]
    rid = lax.broadcasted_iota(jnp.int32, x.shape, 0)
    xp = jnp.where(rid == 0, prev_row, pltpu.roll(x, 1, 0))
    xn = jnp.where(rid == tm - 1, next_row, pltpu.roll(x, tm - 1, 0))
    return xp, xn


def _prep_kernel(qkv_ref, qkv_p_ref, qkv_n_ref, sc_ref, sc_p_ref, sc_n_ref, ab_ref,
                 dnw_ref, scw_ref, alog_ref, dtb_ref, qkvn_ref, ysc_ref, gb_ref):
    j = pl.program_id(1)
    nj = pl.num_programs(1)
    pv = (j >= 2).astype(F32)
    nv = jnp.logical_and(j >= 1, j <= nj - 2).astype(F32)

    x = qkv_ref[...].astype(F32)
    xp, xn = _shift_rows(x, qkv_p_ref[HALO - 1:HALO, :].astype(F32) * pv, qkv_n_ref[0:1, :].astype(F32) * nv)
    y = xp * dnw_ref[0:1, :] + x * dnw_ref[1:2, :] + xn * dnw_ref[2:3, :]
    y = y * _sigmoid(y)
    nqk = DN_HEADS * DN_DK
    for g in range(2 * DN_HEADS):
        yy = y[:, g * DN_DK:(g + 1) * DN_DK]
        yy = yy * lax.rsqrt(jnp.sum(yy * yy, axis=-1, keepdims=True) + 1e-6)
        if g < DN_HEADS:
            yy = yy * (DN_DK ** -0.5)
        qkvn_ref[:, g * DN_DK:(g + 1) * DN_DK] = yy.astype(BF16)
    qkvn_ref[:, 2 * nqk:] = y[:, 2 * nqk:].astype(BF16)

    s = sc_ref[...].astype(F32)
    sp = sc_p_ref[HALO - 1:HALO, :].astype(F32) * pv
    sn = sc_n_ref[0:1, :].astype(F32) * nv
    w = SC_WIDTH
    u = s[:, 2 * w:3 * w] * s[:, 0:w]
    up, un = _shift_rows(u, sp[:, 2 * w:3 * w] * sp[:, 0:w], sn[:, 2 * w:3 * w] * sn[:, 0:w])
    conv = up * scw_ref[0:1, :] + u * scw_ref[1:2, :] + un * scw_ref[2:3, :]
    ysc_ref[...] = (s[:, w:2 * w] * conv).astype(BF16)

    ab = ab_ref[...]
    sp_arg = ab + dtb_ref[...]
    softplus = jnp.maximum(sp_arg, 0.0) + jnp.log1p(jnp.exp(-jnp.abs(sp_arg)))
    gval = -jnp.exp(alog_ref[...]) * softplus
    lane = lax.broadcasted_iota(jnp.int32, ab.shape, 1)
    gb_ref[...] = jnp.where(lane < 2 * DN_HEADS, gval, _sigmoid(ab))


def _prep(qkv, sc, ab, dn_conv_w, sc_conv_w, alog_row, dtb_row):
    bsz, t, _ = qkv.shape
    tm = ROW_TILE
    nj = t // tm
    hb = tm // HALO
    nh = t // HALO
    row = lambda w: pl.BlockSpec((None, tm, w), lambda b, j: (b, j, 0))
    prev = lambda w: pl.BlockSpec((None, HALO, w), lambda b, j: (b, jnp.maximum(j * hb - 1, 0), 0))
    nxt = lambda w: pl.BlockSpec((None, HALO, w), lambda b, j: (b, jnp.minimum((j + 1) * hb, nh - 1), 0))
    full = lambda a: pl.BlockSpec(a.shape, lambda b, j: (0,) * a.ndim)
    sds = jax.ShapeDtypeStruct
    return pl.pallas_call(
        _prep_kernel,
        grid=(bsz, nj),
        in_specs=[row(QKV_W), prev(QKV_W), nxt(QKV_W), row(SC3_W), prev(SC3_W), nxt(SC3_W), row(AB_W),
                  full(dn_conv_w), full(sc_conv_w), full(alog_row), full(dtb_row)],
        out_specs=[row(QKV_W), row(SC_WIDTH), row(AB_W)],
        out_shape=[sds((bsz, t, QKV_W), BF16), sds((bsz, t, SC_WIDTH), BF16), sds((bsz, t, AB_W), F32)],
        compiler_params=_cparams(("parallel", "arbitrary")),
    )(qkv, qkv, qkv, sc, sc, sc, ab, dn_conv_w, sc_conv_w, alog_row, dtb_row)


def _stack_heads(x):
    return jnp.concatenate([x[:, hh * LANE:(hh + 1) * LANE] for hh in range(DN_HEADS)], axis=0)


def _stack_cols(x, c0):
    return jnp.concatenate([x[:, c0 + hh:c0 + hh + 1] for hh in range(DN_HEADS)], axis=0)


def _dn_direction(d, x, gbt, tri, m_causal, m_strict, eye, s_ref):
    c = DN_CHUNK
    nqk = DN_HEADS * DN_DK
    qst = _stack_heads(x[:, 0:nqk])
    kst = _stack_heads(x[:, nqk:2 * nqk])
    vst = _stack_heads(x[:, 2 * nqk:])
    gc_all = jnp.dot(tri, gbt, preferred_element_type=F32, precision=lax.Precision.HIGHEST)
    gc = _stack_cols(gc_all, d * DN_HEADS)
    beta = _stack_cols(gbt, (2 + d) * DN_HEADS)
    last = c - 1 if d == 0 else 0
    gl_heads = [gc_all[last:last + 1, d * DN_HEADS + hh:d * DN_HEADS + hh + 1] for hh in range(DN_HEADS)]
    gl = jnp.concatenate([jnp.broadcast_to(g1, (c, 1)) for g1 in gl_heads], axis=0)

    hc = DN_HEADS * c
    gmat = jnp.broadcast_to(gc, (hc, hc))
    dec = jnp.exp(jnp.minimum(gmat - gmat.T, 0.0))
    dec_c = jnp.where(m_causal > 0.5, dec, 0.0)
    dec_s = jnp.where(m_strict > 0.5, dec, 0.0)

    kf = kst.astype(F32)
    kb = kf * beta
    lmat = _dot_nt(kb.astype(BF16), kst) * dec_s
    xpow = -lmat
    tinv = eye + xpow
    for _ in range(int(math.log2(c)) - 1):
        xb = xpow.astype(BF16)
        xpow = _dot(xb, xb)
        tinv = tinv + _dot(tinv.astype(BF16), xpow.astype(BF16))
    eg = jnp.exp(gc)
    rhs = jnp.concatenate([vst.astype(F32) * beta, kb * eg], axis=1)
    sol = _dot(tinv.astype(BF16), rhs.astype(BF16))
    u = sol[:, 0:DN_DV]
    w = sol[:, DN_DV:]
    intra = _dot_nt(qst, kst) * dec_c
    qd = qst.astype(F32) * eg
    kd = kf * jnp.exp(gl - gc)

    s_cat = s_ref[...]
    wq = jnp.concatenate([w, qd], axis=0).astype(BF16)
    full = _dot(wq, s_cat.astype(BF16))
    ws = jnp.concatenate([full[hh * c:(hh + 1) * c, hh * DN_DV:(hh + 1) * DN_DV] for hh in range(DN_HEADS)], axis=0)
    qs = jnp.concatenate([full[hc + hh * c:hc + (hh + 1) * c, hh * DN_DV:(hh + 1) * DN_DV]
                          for hh in range(DN_HEADS)], axis=0)
    vnew = u - ws
    o = qs + _dot(intra.astype(BF16), vnew.astype(BF16))
    rhead = lax.broadcasted_iota(jnp.int32, (hc, DN_DV), 0) // c
    vblk = jnp.concatenate([jnp.where(rhead == hh, vnew, 0.0) for hh in range(DN_HEADS)], axis=1).astype(BF16)
    gt = jnp.concatenate([jnp.broadcast_to(jnp.exp(g1), (1, DN_DV)) for g1 in gl_heads], axis=1)
    s_ref[...] = s_cat * gt + _dot(kd.T.astype(BF16), vblk)
    return jnp.concatenate([o[hh * c:(hh + 1) * c] for hh in range(DN_HEADS)], axis=1)


def _dn_kernel(xf_ref, gf_ref, xb_ref, gb_ref, trif_ref, trib_ref, mcf_ref, msf_ref, mcb_ref, msb_ref, eye_ref,
               of_ref, ob_ref, s_ref):
    @pl.when(pl.program_id(1) == 0)
    def _():
        s_ref[...] = jnp.zeros_like(s_ref)

    eye = eye_ref[...]
    of_ref[...] = _dn_direction(0, xf_ref[...], gf_ref[...], trif_ref[...], mcf_ref[...], msf_ref[...], eye,
                                s_ref.at[0]).astype(BF16)
    ob_ref[...] = _dn_direction(1, xb_ref[...], gb_ref[...], trib_ref[...], mcb_ref[...], msb_ref[...], eye,
                                s_ref.at[1]).astype(BF16)


def _deltanet(qkvn, gb, n_ctx_chunks):
    bsz, t, _ = qkvn.shape
    c = DN_CHUNK
    nc = t // c
    hc = DN_HEADS * c
    bmap = lambda i: jnp.where(i < n_ctx_chunks, n_ctx_chunks - 1 - i, nc + n_ctx_chunks - 1 - i)
    ii = jnp.arange(c)
    tri_f = (ii[None, :] <= ii[:, None]).astype(F32)
    tri_b = (ii[None, :] >= ii[:, None]).astype(F32)
    r = jnp.arange(hc)
    same = (r[:, None] // c) == (r[None, :] // c)
    pi, pj = r[:, None] % c, r[None, :] % c
    mc_f = (same & (pj <= pi)).astype(F32)
    ms_f = (same & (pj < pi)).astype(F32)
    mc_b = (same & (pj >= pi)).astype(F32)
    ms_b = (same & (pj > pi)).astype(F32)
    eye = jnp.eye(hc, dtype=F32)
    full = lambda a: pl.BlockSpec(a.shape, lambda b, i: (0,) * a.ndim)
    fw = lambda w: pl.BlockSpec((None, c, w), lambda b, i: (b, i, 0))
    bw = lambda w: pl.BlockSpec((None, c, w), lambda b, i: (b, bmap(i), 0))
    ow = DN_HEADS * DN_DV
    sds = jax.ShapeDtypeStruct
    return pl.pallas_call(
        _dn_kernel,
        grid=(bsz, nc),
        in_specs=[fw(QKV_W), fw(AB_W), bw(QKV_W), bw(AB_W), full(tri_f), full(tri_b),
                  full(mc_f), full(ms_f), full(mc_b), full(ms_b), full(eye)],
        out_specs=[fw(ow), bw(ow)],
        out_shape=[sds((bsz, t, ow), BF16), sds((bsz, t, ow), BF16)],
        scratch_shapes=[pltpu.VMEM((2, DN_DK, DN_HEADS * DN_DV), F32)],
        compiler_params=_cparams(("arbitrary", "arbitrary")),
    )(qkvn, gb, qkvn, gb, tri_f, tri_b, mc_f, ms_f, mc_b, ms_b, eye)


def _softmax_av(q, k, v):
    s = _dot_nt(q, k)
    m = jnp.max(s, axis=-1, keepdims=True)
    p = jnp.exp(s - m)
    l = jnp.sum(p, axis=-1, keepdims=True)
    return _dot(p.astype(BF16), v) / l


def _mla_kernel(q_ref, k_ref, v_ref, o_ref):
    j = pl.program_id(2)
    tc = q_ref.shape[0]

    @pl.when(j == 0)
    def _():
        o_ref[...] = _softmax_av(q_ref[...], k_ref[0:tc, :], v_ref[0:tc, :]).astype(BF16)

    @pl.when(j > 0)
    def _():
        o_ref[...] = _softmax_av(q_ref[...], k_ref[...], v_ref[...]).astype(BF16)


def _mla(q, k, v):
    bsz, nh, t, _ = q.shape
    tq = ROW_TILE
    return pl.pallas_call(
        _mla_kernel,
        grid=(bsz, nh, t // tq),
        in_specs=[pl.BlockSpec((None, None, tq, QK_W), lambda b, hh, j: (b, hh, j, 0)),
                  pl.BlockSpec((None, None, t, QK_W), lambda b, hh, j: (b, hh, 0, 0)),
                  pl.BlockSpec((None, None, t, MLA_V), lambda b, hh, j: (b, hh, 0, 0))],
        out_specs=pl.BlockSpec((None, tq, MLA_V), lambda b, hh, j: (b, j, hh)),
        out_shape=jax.ShapeDtypeStruct((bsz, t, nh * MLA_V), BF16),
        compiler_params=_cparams(("parallel", "parallel", "arbitrary")),
    )(q, k, v)


def _pack_halves(x):
    w = x.shape[1] // 2
    lo = lax.bitcast_convert_type(x[:, :w].astype(BF16).astype(F32), jnp.uint32)
    hi = lax.bitcast_convert_type(x[:, w:].astype(BF16).astype(F32), jnp.uint32)
    return (hi & jnp.uint32(0xFFFF0000)) | (lo >> 16)


def _unpack_halves(p):
    lo = lax.bitcast_convert_type(p << 16, F32)
    hi = lax.bitcast_convert_type(p & jnp.uint32(0xFFFF0000), F32)
    return lo, hi


def _merge_kernel(h_ref, mod_ref, g1_ref, of_ref, ob_ref, z_ref, dng_ref, ysc_ref, ymla_ref,
                  wg_ref, bg_ref, wdn_ref, wsc_ref, wmla_ref, wout_ref, g2_ref, rw_ref, rwh_ref, rb_ref, tril_ref,
                  hn_ref, xm2_ref, ti_ref, tg_ref, rank_ref, cnt_ref, cnt_scr):
    @pl.when(jnp.logical_and(pl.program_id(0) == 0, pl.program_id(1) == 0))
    def _():
        cnt_scr[...] = jnp.zeros_like(cnt_scr)

    d = h_ref.shape[-1]
    x = h_ref[...]
    xm = _rms(x, g1_ref[...]) * (1.0 + mod_ref[1:2, :]) + mod_ref[0:1, :]
    gates = _sigmoid(_dot(xm.astype(BF16), wg_ref[...]) + bg_ref[...])
    o = of_ref[...].astype(F32) + ob_ref[...].astype(F32)
    z = z_ref[...].astype(F32)
    parts = []
    for hh in range(DN_HEADS):
        oh = o[:, hh * DN_DV:(hh + 1) * DN_DV]
        zh = z[:, hh * DN_DV:(hh + 1) * DN_DV]
        parts.append(_rms(oh, dng_ref[...]) * (zh * _sigmoid(zh)))
    ydn = jnp.concatenate(parts, axis=1).astype(BF16)
    merged = (gates[:, 0:d] * _dot(ydn, wdn_ref[...]) + gates[:, d:2 * d] * _dot(ysc_ref[...], wsc_ref[...])
              + gates[:, 2 * d:3 * d] * _dot(ymla_ref[...], wmla_ref[...]))
    hn = x + mod_ref[2:3, :] * _dot(merged.astype(BF16), wout_ref[...])
    hn_ref[...] = hn
    xm2 = _rms(hn, g2_ref[...]) * (1.0 + mod_ref[4:5, :]) + mod_ref[3:4, :]
    xm2_ref[...] = _pack_halves(xm2)

    xh = xm2.astype(BF16)
    xl = (xm2 - xh.astype(F32)).astype(BF16)
    a = _dot(xh, rw_ref[...])
    logits = a[:, 0:LANE] + a[:, LANE:2 * LANE] + _dot(xl, rwh_ref[...]) + rb_ref[...]
    lane = lax.broadcasted_iota(jnp.int32, logits.shape, 1)
    work = jnp.where(lane < N_EXPERTS, logits, -jnp.inf)
    ti = jnp.zeros(logits.shape, jnp.int32)
    tv = jnp.zeros(logits.shape, F32)
    vals, idxs = [], []
    for kk in range(TOP_K):
        m = jnp.max(work, axis=-1, keepdims=True)
        idx = jnp.min(jnp.where(work == m, lane, LANE), axis=-1, keepdims=True)
        work = jnp.where(lane == idx, -jnp.inf, work)
        ti = jnp.where(lane == kk, idx, ti)
        vals.append(m)
        idxs.append(idx)
    es = [jnp.exp(vv - vals[0]) for vv in vals]
    tot = es[0] + es[1] + es[2] + es[3]
    for kk in range(TOP_K):
        tv = jnp.where(lane == kk, es[kk] / tot, tv)
    ti_ref[...] = ti
    tg_ref[...] = tv

    sel = jnp.zeros(logits.shape, F32)
    for idx in idxs:
        sel = sel + (lane == idx).astype(F32)
    before = _dot(tril_ref[...], sel.astype(BF16)) + cnt_scr[...]
    rank = jnp.zeros(logits.shape, F32)
    for kk in range(TOP_K):
        rk = jnp.sum(jnp.where(lane == idxs[kk], before, 0.0), axis=-1, keepdims=True)
        rank = jnp.where(lane == kk, rk, rank)
    rank_ref[...] = rank.astype(jnp.int32)
    total = cnt_scr[...] + jnp.sum(sel, axis=0, keepdims=True)
    cnt_scr[...] = total
    cnt_ref[...] = total


def _merge(h, mod, g1, o_f, o_b, z, dng, ysc, ymla, wg, bg, wdn, wsc, wmla, wout, g2, rw_cat, rw_hi, rb):
    bsz, t, d = h.shape
    tm = ROW_TILE
    row = lambda w: pl.BlockSpec((None, tm, w), lambda b, j: (b, j, 0))
    full = lambda a: pl.BlockSpec(a.shape, lambda b, j: (0,) * a.ndim)
    ow = DN_HEADS * DN_DV
    sds = jax.ShapeDtypeStruct
    ii = jnp.arange(tm)
    tril = (ii[None, :] < ii[:, None]).astype(BF16)
    return pl.pallas_call(
        _merge_kernel,
        grid=(bsz, t // tm),
        in_specs=[row(d), pl.BlockSpec((None, None, 6, d), lambda b, j: (b, jnp.minimum(j, 1), 0, 0)), full(g1),
                  row(ow), row(ow), row(Z_W), full(dng), row(SC_WIDTH), row(MLA_HEADS * MLA_V),
                  full(wg), full(bg), full(wdn), full(wsc), full(wmla), full(wout), full(g2),
                  full(rw_cat), full(rw_hi), full(rb), full(tril)],
        out_specs=[row(d), row(d // 2), row(LANE), row(LANE), row(LANE),
                   pl.BlockSpec((1, LANE), lambda b, j: (0, 0))],
        out_shape=[sds((bsz, t, d), F32), sds((bsz, t, d // 2), jnp.uint32), sds((bsz, t, LANE), jnp.int32),
                   sds((bsz, t, LANE), F32), sds((bsz, t, LANE), jnp.int32), sds((1, LANE), F32)],
        scratch_shapes=[pltpu.VMEM((1, LANE), F32)],
        compiler_params=_cparams(("arbitrary", "arbitrary")),
    )(h, mod, g1, o_f, o_b, z, dng, ysc, ymla, wg, bg, wdn, wsc, wmla, wout, g2, rw_cat, rw_hi, rb, tril)


GLU_GROUP = 2 * LANE


def _deinterleave_kernel(w_ref, p_ref, o_ref):
    perm = p_ref[...]
    for g in range(w_ref.shape[1] // GLU_GROUP):
        sl = slice(g * GLU_GROUP, (g + 1) * GLU_GROUP)
        o_ref[:, sl] = _dot(w_ref[:, sl].astype(BF16), perm).astype(BF16)


def _deinterleave_w1(expert_w1, l):
    _, n_e, d, f2 = expert_w1.shape
    i = jnp.arange(GLU_GROUP)
    perm = (jnp.where(i % 2 == 0, i // 2, LANE + i // 2)[:, None] == i[None, :]).astype(BF16)
    return pl.pallas_call(
        _deinterleave_kernel,
        grid=(n_e,),
        in_specs=[pl.BlockSpec((None, None, d, f2), lambda e: (l, e, 0, 0)),
                  pl.BlockSpec(perm.shape, lambda e: (0, 0))],
        out_specs=pl.BlockSpec((None, d, f2), lambda e: (e, 0, 0)),
        out_shape=jax.ShapeDtypeStruct((n_e, d, f2), BF16),
        compiler_params=_cparams(("arbitrary",)),
    )(expert_w1, perm)


DMA_UNROLL = 8


def _dispatch_kernel(dest_ref, x_ref, xs_in_hbm, xs_hbm, sem):
    del xs_in_hbm
    tm = x_ref.shape[0]
    for kk in range(TOP_K):
        def issue(r, carry):
            pltpu.make_async_copy(x_ref.at[pl.ds(r, 1)], xs_hbm.at[pl.ds(dest_ref[kk, r], 1)], sem).start()
            return carry
        lax.fori_loop(0, tm, issue, 0, unroll=DMA_UNROLL)
    for kk in range(TOP_K):
        pltpu.make_async_copy(x_ref, xs_hbm.at[pl.ds(0, tm)], sem).wait()


def _dispatch(dest_t, xp, n_rows):
    bsz, t, w = xp.shape
    tm = ROW_TILE
    nj = t // tm
    xs0 = jnp.zeros((n_rows, w), xp.dtype)
    return pl.pallas_call(
        _dispatch_kernel,
        grid=(bsz, nj),
        in_specs=[pl.BlockSpec((None, None, TOP_K, tm), lambda b, j: (b, j, 0, 0), memory_space=pltpu.SMEM),
                  pl.BlockSpec((tm, w), lambda b, j: (b * nj + j, 0)),
                  pl.BlockSpec(memory_space=pl.ANY)],
        out_specs=pl.BlockSpec(memory_space=pl.ANY),
        out_shape=jax.ShapeDtypeStruct((n_rows, w), xp.dtype),
        scratch_shapes=[pltpu.SemaphoreType.DMA(())],
        input_output_aliases={2: 0},
        compiler_params=_cparams(("arbitrary", "arbitrary")),
    )(dest_t, xp.reshape(bsz * t, w), xs0)


def _expert_kernel(be_ref, nreal_ref, xs_ref, w1_ref, b1_ref, w2_ref, b2_ref, y_ref):
    i = pl.program_id(0)
    half = w1_ref.shape[0] // 2

    @pl.when(i < nreal_ref[0])
    def _():
        lo, hi = _unpack_halves(xs_ref[...])
        hdn = (_dot(lo.astype(BF16), w1_ref[0:half, :]) + _dot(hi.astype(BF16), w1_ref[half:, :])) + b1_ref[...]
        acts = []
        for g in range(hdn.shape[1] // GLU_GROUP):
            glu = jnp.minimum(hdn[:, g * GLU_GROUP:g * GLU_GROUP + LANE], SWIGLU_LIMIT)
            lin = jnp.clip(hdn[:, g * GLU_GROUP + LANE:(g + 1) * GLU_GROUP], -SWIGLU_LIMIT, SWIGLU_LIMIT)
            acts.append((glu * _sigmoid(SWIGLU_ALPHA * glu) * (lin + 1.0)).astype(BF16))
        y_ref[...] = _pack_halves(_dot(jnp.concatenate(acts, axis=1), w2_ref[...]) + b2_ref[...])

    @pl.when(i >= nreal_ref[0])
    def _():
        y_ref[...] = jnp.zeros_like(y_ref)


def _experts(block_expert, n_real, xs, w1, b1, w2, b2):
    n_blocks = block_expert.shape[0]
    f, d = w2.shape[1:]
    eb = EXPERT_BLOCK
    wspec = lambda k, n: pl.BlockSpec((None, k, n), lambda i, be, nr: (be[i], 0, 0))
    gs = pltpu.PrefetchScalarGridSpec(
        num_scalar_prefetch=2,
        grid=(n_blocks,),
        in_specs=[pl.BlockSpec((eb, d // 2), lambda i, be, nr: (jnp.minimum(i, nr[0] - 1), 0)),
                  wspec(d, 2 * f), wspec(1, 2 * f), wspec(f, d), wspec(1, d)],
        out_specs=pl.BlockSpec((eb, d // 2), lambda i, be, nr: (i, 0)),
    )
    return pl.pallas_call(
        _expert_kernel,
        grid_spec=gs,
        out_shape=jax.ShapeDtypeStruct((n_blocks * eb, d // 2), jnp.uint32),
        compiler_params=_cparams(("arbitrary",)),
    )(block_expert, n_real, xs, w1, b1, w2, b2)


def _combine_kernel(final, pos_ref, y_hbm, h_ref, mod_ref, tg_ref, fg_ref, o_ref, ybuf, sem):
    tm = h_ref.shape[0]
    for kk in range(TOP_K):
        def issue(r, carry):
            pltpu.make_async_copy(y_hbm.at[pl.ds(pos_ref[kk, r], 1)], ybuf.at[kk, pl.ds(r, 1)], sem).start()
            return carry
        lax.fori_loop(0, tm, issue, 0, unroll=DMA_UNROLL)
    for kk in range(TOP_K):
        pltpu.make_async_copy(y_hbm.at[pl.ds(0, tm)], ybuf.at[kk], sem).wait()
    tg = tg_ref[...]
    ylo, yhi = None, None
    for kk in range(TOP_K):
        lo, hi = _unpack_halves(ybuf[kk])
        gk = tg[:, kk:kk + 1]
        ylo = gk * lo if ylo is None else ylo + gk * lo
        yhi = gk * hi if yhi is None else yhi + gk * hi
    hn = h_ref[...] + mod_ref[5:6, :] * jnp.concatenate([ylo, yhi], axis=1)
    if final:
        hn = _rms(hn, fg_ref[...])
    o_ref[...] = hn


def _combine(pos_t, y_rows, h, mod, tg, fg, final):
    bsz, t, d = h.shape
    tm = ROW_TILE
    nj = t // tm
    row = lambda w: pl.BlockSpec((None, tm, w), lambda b, j: (b, j, 0))
    return pl.pallas_call(
        functools.partial(_combine_kernel, final),
        grid=(bsz, nj),
        in_specs=[pl.BlockSpec((None, None, TOP_K, tm), lambda b, j: (b, j, 0, 0), memory_space=pltpu.SMEM),
                  pl.BlockSpec(memory_space=pl.ANY),
                  row(d), pl.BlockSpec((None, None, 6, d), lambda b, j: (b, jnp.minimum(j, 1), 0, 0)),
                  row(LANE), pl.BlockSpec(fg.shape, lambda b, j: (0, 0))],
        out_specs=row(d),
        out_shape=jax.ShapeDtypeStruct((bsz, t, d), F32),
        scratch_shapes=[pltpu.VMEM((TOP_K, tm, d // 2), jnp.uint32), pltpu.SemaphoreType.DMA(())],
        compiler_params=_cparams(("arbitrary", "arbitrary")),
    )(pos_t, y_rows, h, mod, tg, fg)


def _route(ti, rank, cnt):
    bsz, t, _ = ti.shape
    eb = EXPERT_BLOCK
    n_assign = bsz * t * TOP_K
    n_blocks = -(-n_assign // eb) + N_EXPERTS
    counts = cnt[0, :N_EXPERTS].astype(jnp.int32)
    padded = (counts + eb - 1) // eb * eb
    pad_end = jnp.cumsum(padded)
    pad_start = pad_end - padded
    e = ti[:, :, :TOP_K]
    start = jnp.sum(jnp.where(e[..., None] == jnp.arange(N_EXPERTS), pad_start, 0), axis=-1)
    dest = start + rank[:, :, :TOP_K]
    dest_t = dest.reshape(bsz, t // ROW_TILE, ROW_TILE, TOP_K).transpose(0, 1, 3, 2)
    block_expert = jnp.minimum(jnp.searchsorted(pad_end, jnp.arange(n_blocks) * eb, side='right'),
                               N_EXPERTS - 1).astype(jnp.int32)
    n_real = (pad_end[-1] // eb).astype(jnp.int32).reshape(1)
    return dest_t.astype(jnp.int32), block_expert, n_real, n_blocks


def _rope_tables(t_lat, t_ctx):
    rows = t_lat // GRID_W
    row = jnp.repeat(jnp.arange(rows, dtype=F32), GRID_W)
    col = jnp.tile(jnp.arange(GRID_W, dtype=F32), rows)
    axis_dims = MLA_ROPE // 2
    inv = ROPE_BASE ** (-jnp.arange(0, axis_dims, 2, dtype=F32) / axis_dims)
    ar, ac = row[:, None] * inv, col[:, None] * inv
    cos64 = jnp.concatenate([jnp.cos(ar), jnp.cos(ar), jnp.cos(ac), jnp.cos(ac)], axis=-1)
    sin64 = jnp.concatenate([-jnp.sin(ar), jnp.sin(ar), -jnp.sin(ac), jnp.sin(ac)], axis=-1)
    pad = jnp.zeros((t_lat, LANE - MLA_ROPE), F32)
    cos_l = jnp.concatenate([cos64, pad], axis=-1)
    sin_l = jnp.concatenate([sin64, pad], axis=-1)
    cos_c = jnp.concatenate([jnp.ones((t_ctx, MLA_ROPE), F32), jnp.zeros((t_ctx, LANE - MLA_ROPE), F32)], axis=-1)
    sin_c = jnp.zeros((t_ctx, LANE), F32)
    return jnp.concatenate([cos_c, cos_l], axis=0), jnp.concatenate([sin_c, sin_l], axis=0)


def _rope_swap_perm():
    half = MLA_ROPE // 4
    perm = []
    for a in range(2):
        base = a * 2 * half
        perm += list(range(base + half, base + 2 * half)) + list(range(base, base + half))
    return jnp.array(perm, jnp.int32)


def _pad_cols(w, width):
    return jnp.concatenate([w, jnp.zeros(w.shape[:-1] + (width - w.shape[-1],), w.dtype)], axis=-1)


def _layer_weights(l, w_in, dn_a_log, dn_dt_bias, mla_w_qb, router_w, router_b, expert_w1, expert_b1):
    nqk = DN_HEADS * DN_DK
    wi = w_in[l]
    o = 0
    dn_qkv = wi[:, o:o + QKV_W]; o += QKV_W
    dn_z = wi[:, o:o + Z_W]; o += Z_W
    dn_ab = wi[:, o:o + 4 * DN_HEADS]; o += 4 * DN_HEADS
    sc3 = wi[:, o:o + SC3_W]; o += SC3_W
    qa = wi[:, o:o + MLA_Q_LORA]; o += MLA_Q_LORA
    ckv = wi[:, o:o + MLA_KV_LORA]; o += MLA_KV_LORA
    kr = wi[:, o:o + MLA_ROPE]
    perm = _rope_swap_perm()
    w_cat = jnp.concatenate([dn_qkv, dn_z, sc3, qa, ckv, _pad_cols(kr, LANE), _pad_cols(kr[:, perm], LANE),
                             _pad_cols(dn_ab, LANE)], axis=-1).astype(BF16)
    dq = MLA_NOPE + MLA_ROPE
    wq = mla_w_qb[l].reshape(MLA_Q_LORA, MLA_HEADS, dq)
    wq_rope = wq[:, :, MLA_NOPE:]
    wq_cat = jnp.concatenate([wq[:, :, :MLA_NOPE], _pad_cols(wq_rope, LANE), _pad_cols(wq_rope[:, :, perm], LANE)],
                             axis=-1).reshape(MLA_Q_LORA, MLA_HEADS * QH_W).astype(BF16)
    alog_row = _pad_cols(jnp.concatenate([dn_a_log[l].reshape(1, -1), jnp.zeros((1, 2 * DN_HEADS), F32)], -1), LANE)
    dtb_row = _pad_cols(jnp.concatenate([dn_dt_bias[l].reshape(1, -1), jnp.zeros((1, 2 * DN_HEADS), F32)], -1), LANE)
    rw = _pad_cols(router_w[l], LANE)
    rw_hi = rw.astype(BF16)
    rw_lo = (rw - rw_hi.astype(F32)).astype(BF16)
    rw_cat = jnp.concatenate([rw_hi, rw_lo], axis=-1)
    rb = jnp.concatenate([router_b[l], jnp.full((LANE - N_EXPERTS,), 0.0, F32)]).reshape(1, LANE)
    w1 = _deinterleave_w1(expert_w1, l)
    n_e, f2 = expert_b1.shape[1:]
    b1 = expert_b1[l].reshape(n_e, f2 // GLU_GROUP, LANE, 2).transpose(0, 1, 3, 2).reshape(n_e, 1, f2)
    return w_cat, wq_cat, alog_row, dtb_row, rw_cat, rw_hi, rb, w1, b1


def kernel(x, c, ctx, c_ctx, ada_w, ada_b, norm1_g, norm2_g, w_in, dn_conv_w, dn_a_log, dn_dt_bias, dn_norm_g, sc_conv_w, mla_q_norm_g, mla_w_qb, mla_kv_norm_g, mla_w_kvb, w_branch_gate, b_branch_gate, w_branch_dn, w_branch_sc, w_branch_mla, w_out, router_w, router_b, expert_w1, expert_b1, expert_w2, expert_b2, final_norm_g):
    bsz, t_lat, d = x.shape
    t_ctx = ctx.shape[1]
    depth = ada_w.shape[0]
    assert t_ctx == ROW_TILE and t_lat % ROW_TILE == 0 and t_lat % GRID_W == 0
    t = t_ctx + t_lat

    cos_t, sin_t = _rope_tables(t_lat, t_ctx)
    n_mod_rows = -(-(bsz + 1) // 8) * 8
    cvec = jnp.concatenate([c, c_ctx[None], jnp.zeros((n_mod_rows - bsz - 1, d), F32)], axis=0)
    mods = _ada_mods(cvec, ada_w, ada_b)

    h = jnp.concatenate([ctx, x], axis=1)
    row2 = lambda v: v.reshape(1, -1)
    for l in range(depth):
        mod_lat = mods[l, :bsz].reshape(bsz, 1, 6, d)
        mod_ctx = jnp.broadcast_to(mods[l, bsz].reshape(1, 1, 6, d), (bsz, 1, 6, d))
        mod = jnp.concatenate([mod_ctx, mod_lat], axis=1)
        (w_cat, wq_cat, alog_row, dtb_row, rw_cat, rw_hi, rb, w1, b1) = _layer_weights(
            l, w_in, dn_a_log, dn_dt_bias, mla_w_qb, router_w, router_b, expert_w1, expert_b1)

        qkv, z, sc, ab, q, k, v = _inproj(h, mod, row2(norm1_g[l]), w_cat, row2(mla_q_norm_g[l]), wq_cat,
                                          row2(mla_kv_norm_g[l]), mla_w_kvb[l].astype(BF16), cos_t, sin_t)
        qkvn, ysc, gb = _prep(qkv, sc, ab, dn_conv_w[l], sc_conv_w[l], alog_row, dtb_row)
        o_f, o_b = _deltanet(qkvn, gb, t_ctx // DN_CHUNK)
        ymla = _mla(q, k, v)
        hn, xm2, ti, tg, rank, cnt = _merge(h, mod, row2(norm1_g[l]), o_f, o_b, z, row2(dn_norm_g[l]), ysc, ymla,
                                 w_branch_gate[l].astype(BF16), row2(b_branch_gate[l]),
                                 w_branch_dn[l].astype(BF16), w_branch_sc[l].astype(BF16),
                                 w_branch_mla[l].astype(BF16), w_out[l].astype(BF16), row2(norm2_g[l]),
                                 rw_cat, rw_hi, rb)
        dest_t, block_expert, n_real, n_blocks = _route(ti, rank, cnt)
        xs = _dispatch(dest_t, xm2, n_blocks * EXPERT_BLOCK)
        y_rows = _experts(block_expert, n_real, xs, w1, b1, expert_w2[l].astype(BF16), expert_b2[l][:, None, :])
        h = _combine(dest_t, y_rows, hn, mod, tg, row2(final_norm_g), final=(l == depth - 1))
    return h[:, t_ctx:, :]
```

```python
import functools
import math

import jax
import jax.numpy as jnp
from jax import lax
from jax.experimental import pallas as pl
from jax.experimental.pallas import tpu as pltpu

F32 = jnp.float32
BF16 = jnp.bfloat16

GRID_W = 64
NORM_EPS = 1e-6
DN_HEADS = 4
DN_DK = 128
DN_DV = 128
DN_CHUNK = 64
SC_WIDTH = 512
MLA_HEADS = 4
MLA_Q_LORA = 256
MLA_KV_LORA = 128
MLA_NOPE = 128
MLA_ROPE = 64
MLA_V = 128
MLA_SCALE = (MLA_NOPE + MLA_ROPE) ** -0.5
ROPE_BASE = 10000.0
N_EXPERTS = 32
TOP_K = 4
EXPERT_FF = 1024
SWIGLU_ALPHA = 1.702
SWIGLU_LIMIT = 7.0
EXPERT_BLOCK = 256

LANE = 128
ROW_TILE = 256
HALO = 16
RUN_ALIGN = 8
LOCAL_ROWS = TOP_K * ROW_TILE + N_EXPERTS * RUN_ALIGN
N_RUNS = N_EXPERTS + 1
VMEM_LIMIT = 56 * 1024 * 1024

QKV_W = 3 * DN_HEADS * DN_DK
Z_W = DN_HEADS * DN_DV
SC3_W = 3 * SC_WIDTH
KVA_W = 3 * LANE
AB_W = LANE
IN_W = QKV_W + Z_W + SC3_W + MLA_Q_LORA + KVA_W + AB_W
QH_W = 3 * LANE
QK_W = 2 * LANE


def _cparams(sem):
    return pltpu.CompilerParams(dimension_semantics=sem, vmem_limit_bytes=VMEM_LIMIT)


def _dot(a, b):
    return jnp.dot(a, b, preferred_element_type=F32)


def _dot_nt(a, b):
    return lax.dot_general(a, b, (((1,), (1,)), ((), ())), preferred_element_type=F32)


def _rms(x, g):
    return x * lax.rsqrt(jnp.mean(x * x, axis=-1, keepdims=True) + NORM_EPS) * g


def _sigmoid(x):
    return 1.0 / (1.0 + jnp.exp(-x))


def _ada_kernel(c_ref, w_ref, b_ref, o_ref):
    cv = c_ref[...]
    s = cv * _sigmoid(cv)
    o_ref[...] = _dot(s.astype(BF16), w_ref[...].astype(BF16)) + b_ref[...]


def _ada_mods(cvec, ada_w, ada_b):
    n_layers, d, d6 = ada_w.shape
    r = cvec.shape[0]
    tn = 512
    return pl.pallas_call(
        _ada_kernel,
        grid=(n_layers, d6 // tn),
        in_specs=[pl.BlockSpec((r, d), lambda l, n: (0, 0)),
                  pl.BlockSpec((None, d, tn), lambda l, n: (l, 0, n)),
                  pl.BlockSpec((None, 1, tn), lambda l, n: (l, 0, n))],
        out_specs=pl.BlockSpec((None, r, tn), lambda l, n: (l, 0, n)),
        out_shape=jax.ShapeDtypeStruct((n_layers, r, d6), F32),
        compiler_params=_cparams(("arbitrary", "arbitrary")),
    )(cvec, ada_w, ada_b.reshape(n_layers, 1, d6))


def _inproj_kernel(h_ref, mod_ref, g1_ref, w_ref, gq_ref, wq_ref, gkv_ref, wkv_ref, cos_ref, sin_ref,
                   qkv_ref, z_ref, sc_ref, ab_ref, q_ref, k_ref, v_ref):
    x = h_ref[...]
    xm = _rms(x, g1_ref[...]) * (1.0 + mod_ref[1:2, :]) + mod_ref[0:1, :]
    p = _dot(xm.astype(BF16), w_ref[...])
    o = 0
    qkv_ref[...] = p[:, o:o + QKV_W].astype(BF16)
    o += QKV_W
    z_ref[...] = p[:, o:o + Z_W].astype(BF16)
    o += Z_W
    sc_ref[...] = p[:, o:o + SC3_W].astype(BF16)
    o += SC3_W
    qa = p[:, o:o + MLA_Q_LORA]
    o += MLA_Q_LORA
    ckv = p[:, o:o + LANE]
    kr = p[:, o + LANE:o + 2 * LANE]
    krs = p[:, o + 2 * LANE:o + 3 * LANE]
    o += KVA_W
    ab_ref[...] = p[:, o:o + AB_W]

    cos = cos_ref[...]
    sin = sin_ref[...]
    k_rope = (kr * cos + krs * sin).astype(BF16)
    qn = _rms(qa, gq_ref[...]).astype(BF16)
    qf = _dot(qn, wq_ref[...])
    kvn = _rms(ckv, gkv_ref[...]).astype(BF16)
    kv = _dot(kvn, wkv_ref[...])
    for hh in range(MLA_HEADS):
        b0 = hh * QH_W
        q_ref[hh, :, 0:LANE] = (qf[:, b0:b0 + LANE] * MLA_SCALE).astype(BF16)
        q_rope = qf[:, b0 + LANE:b0 + 2 * LANE] * cos + qf[:, b0 + 2 * LANE:b0 + 3 * LANE] * sin
        q_ref[hh, :, LANE:2 * LANE] = (q_rope * MLA_SCALE).astype(BF16)
        c0 = hh * (MLA_NOPE + MLA_V)
        k_ref[hh, :, 0:LANE] = kv[:, c0:c0 + MLA_NOPE].astype(BF16)
        k_ref[hh, :, LANE:2 * LANE] = k_rope
        v_ref[hh] = kv[:, c0 + MLA_NOPE:c0 + MLA_NOPE + MLA_V].astype(BF16)


def _inproj(h, mod, g1, w_cat, gq, wq_cat, gkv, wkv, cos_t, sin_t):
    bsz, t, d = h.shape
    tm = ROW_TILE
    nj = t // tm
    row = lambda w: pl.BlockSpec((None, tm, w), lambda b, j: (b, j, 0))
    full = lambda a: pl.BlockSpec(a.shape, lambda b, j: (0,) * a.ndim)
    head = lambda w: pl.BlockSpec((None, MLA_HEADS, tm, w), lambda b, j: (b, 0, j, 0))
    sds = jax.ShapeDtypeStruct
    return pl.pallas_call(
        _inproj_kernel,
        grid=(bsz, nj),
        in_specs=[row(d),
                  pl.BlockSpec((None, None, 6, d), lambda b, j: (b, jnp.minimum(j, 1), 0, 0)),
                  full(g1), full(w_cat), full(gq), full(wq_cat), full(gkv), full(wkv),
                  pl.BlockSpec((tm, LANE), lambda b, j: (j, 0)),
                  pl.BlockSpec((tm, LANE), lambda b, j: (j, 0))],
        out_specs=[row(QKV_W), row(Z_W), row(SC3_W), row(AB_W), head(QK_W), head(QK_W), head(MLA_V)],
        out_shape=[sds((bsz, t, QKV_W), BF16), sds((bsz, t, Z_W), BF16), sds((bsz, t, SC3_W), BF16),
                   sds((bsz, t, AB_W), F32),
                   sds((bsz, MLA_HEADS, t, QK_W), BF16), sds((bsz, MLA_HEADS, t, QK_W), BF16),
                   sds((bsz, MLA_HEADS, t, MLA_V), BF16)],
        compiler_params=_cparams(("parallel", "arbitrary")),
    )(h, mod, g1, w_cat, gq, wq_cat, gkv, wkv, cos_t, sin_t)


def _shift_rows(x, prev_row, next_row):
    tm = x.shape[0]
    rid = lax.broadcasted_iota(jnp.int32, x.shape, 0)
    xp = jnp.where(rid == 0, prev_row, pltpu.roll(x, 1, 0))
    xn = jnp.where(rid == tm - 1, next_row, pltpu.roll(x, tm - 1, 0))
    return xp, xn


def _prep_kernel(qkv_ref, qkv_p_ref, qkv_n_ref, sc_ref, sc_p_ref, sc_n_ref, ab_ref,
                 dnw_ref, scw_ref, alog_ref, dtb_ref, qkvn_ref, ysc_ref, gb_ref):
    j = pl.program_id(1)
    nj = pl.num_programs(1)
    pv = (j >= 2).astype(F32)
    nv = jnp.logical_and(j >= 1, j <= nj - 2).astype(F32)

    x = qkv_ref[...].astype(F32)
    xp, xn = _shift_rows(x, qkv_p_ref[HALO - 1:HALO, :].astype(F32) * pv, qkv_n_ref[0:1, :].astype(F32) * nv)
    y = xp * dnw_ref[0:1, :] + x * dnw_ref[1:2, :] + xn * dnw_ref[2:3, :]
    y = y * _sigmoid(y)
    nqk = DN_HEADS * DN_DK
    for g in range(2 * DN_HEADS):
        yy = y[:, g * DN_DK:(g + 1) * DN_DK]
        yy = yy * lax.rsqrt(jnp.sum(yy * yy, axis=-1, keepdims=True) + 1e-6)
        if g < DN_HEADS:
            yy = yy * (DN_DK ** -0.5)
        qkvn_ref[:, g * DN_DK:(g + 1) * DN_DK] = yy.astype(BF16)
    qkvn_ref[:, 2 * nqk:] = y[:, 2 * nqk:].astype(BF16)

    s = sc_ref[...].astype(F32)
    sp = sc_p_ref[HALO - 1:HALO, :].astype(F32) * pv
    sn = sc_n_ref[0:1, :].astype(F32) * nv
    w = SC_WIDTH
    u = s[:, 2 * w:3 * w] * s[:, 0:w]
    up, un = _shift_rows(u, sp[:, 2 * w:3 * w] * sp[:, 0:w], sn[:, 2 * w:3 * w] * sn[:, 0:w])
    conv = up * scw_ref[0:1, :] + u * scw_ref[1:2, :] + un * scw_ref[2:3, :]
    ysc_ref[...] = (s[:, w:2 * w] * conv).astype(BF16)

    ab = ab_ref[...]
    sp_arg = ab + dtb_ref[...]
    softplus = jnp.maximum(sp_arg, 0.0) + jnp.log1p(jnp.exp(-jnp.abs(sp_arg)))
    gval = -jnp.exp(alog_ref[...]) * softplus
    lane = lax.broadcasted_iota(jnp.int32, ab.shape, 1)
    gb_ref[...] = jnp.where(lane < 2 * DN_HEADS, gval, _sigmoid(ab))


def _prep(qkv, sc, ab, dn_conv_w, sc_conv_w, alog_row, dtb_row):
    bsz, t, _ = qkv.shape
    tm = ROW_TILE
    nj = t // tm
    hb = tm // HALO
    nh = t // HALO
    row = lambda w: pl.BlockSpec((None, tm, w), lambda b, j: (b, j, 0))
    prev = lambda w: pl.BlockSpec((None, HALO, w), lambda b, j: (b, jnp.maximum(j * hb - 1, 0), 0))
    nxt = lambda w: pl.BlockSpec((None, HALO, w), lambda b, j: (b, jnp.minimum((j + 1) * hb, nh - 1), 0))
    full = lambda a: pl.BlockSpec(a.shape, lambda b, j: (0,) * a.ndim)
    sds = jax.ShapeDtypeStruct
    return pl.pallas_call(
        _prep_kernel,
        grid=(bsz, nj),
        in_specs=[row(QKV_W), prev(QKV_W), nxt(QKV_W), row(SC3_W), prev(SC3_W), nxt(SC3_W), row(AB_W),
                  full(dn_conv_w), full(sc_conv_w), full(alog_row), full(dtb_row)],
        out_specs=[row(QKV_W), row(SC_WIDTH), row(AB_W)],
        out_shape=[sds((bsz, t, QKV_W), BF16), sds((bsz, t, SC_WIDTH), BF16), sds((bsz, t, AB_W), F32)],
        compiler_params=_cparams(("parallel", "arbitrary")),
    )(qkv, qkv, qkv, sc, sc, sc, ab, dn_conv_w, sc_conv_w, alog_row, dtb_row)


def _stack_heads(x):
    return jnp.concatenate([x[:, hh * LANE:(hh + 1) * LANE] for hh in range(DN_HEADS)], axis=0)


def _stack_cols(x, c0):
    return jnp.concatenate([x[:, c0 + hh:c0 + hh + 1] for hh in range(DN_HEADS)], axis=0)


def _dn_direction(d, x, gbt, tri, m_causal, m_strict, eye, s_ref):
    c = DN_CHUNK
    nqk = DN_HEADS * DN_DK
    qst = _stack_heads(x[:, 0:nqk])
    kst = _stack_heads(x[:, nqk:2 * nqk])
    vst = _stack_heads(x[:, 2 * nqk:])
    gc_all = jnp.dot(tri, gbt, preferred_element_type=F32, precision=lax.Precision.HIGHEST)
    gc = _stack_cols(gc_all, d * DN_HEADS)
    beta = _stack_cols(gbt, (2 + d) * DN_HEADS)
    last = c - 1 if d == 0 else 0
    gl_heads = [gc_all[last:last + 1, d * DN_HEADS + hh:d * DN_HEADS + hh + 1] for hh in range(DN_HEADS)]
    gl = jnp.concatenate([jnp.broadcast_to(g1, (c, 1)) for g1 in gl_heads], axis=0)

    hc = DN_HEADS * c
    gmat = jnp.broadcast_to(gc, (hc, hc))
    dec = jnp.exp(jnp.minimum(gmat - gmat.T, 0.0))
    dec_c = jnp.where(m_causal > 0.5, dec, 0.0)
    dec_s = jnp.where(m_strict > 0.5, dec, 0.0)

    kf = kst.astype(F32)
    kb = kf * beta
    lmat = _dot_nt(kb.astype(BF16), kst) * dec_s
    xpow = -lmat
    tinv = eye + xpow
    for _ in range(int(math.log2(c)) - 1):
        xb = xpow.astype(BF16)
        xpow = _dot(xb, xb)
        tinv = tinv + _dot(tinv.astype(BF16), xpow.astype(BF16))
    eg = jnp.exp(gc)
    rhs = jnp.concatenate([vst.astype(F32) * beta, kb * eg], axis=1)
    sol = _dot(tinv.astype(BF16), rhs.astype(BF16))
    u = sol[:, 0:DN_DV]
    w = sol[:, DN_DV:]
    intra = _dot_nt(qst, kst) * dec_c
    qd = qst.astype(F32) * eg
    kd = kf * jnp.exp(gl - gc)

    s_cat = s_ref[...]
    wq = jnp.concatenate([w, qd], axis=0).astype(BF16)
    full = _dot(wq, s_cat.astype(BF16))
    ws = jnp.concatenate([full[hh * c:(hh + 1) * c, hh * DN_DV:(hh + 1) * DN_DV] for hh in range(DN_HEADS)], axis=0)
    qs = jnp.concatenate([full[hc + hh * c:hc + (hh + 1) * c, hh * DN_DV:(hh + 1) * DN_DV]
                          for hh in range(DN_HEADS)], axis=0)
    vnew = u - ws
    o = qs + _dot(intra.astype(BF16), vnew.astype(BF16))
    rhead = lax.broadcasted_iota(jnp.int32, (hc, DN_DV), 0) // c
    vblk = jnp.concatenate([jnp.where(rhead == hh, vnew, 0.0) for hh in range(DN_HEADS)], axis=1).astype(BF16)
    gt = jnp.concatenate([jnp.broadcast_to(jnp.exp(g1), (1, DN_DV)) for g1 in gl_heads], axis=1)
    s_ref[...] = s_cat * gt + _dot(kd.T.astype(BF16), vblk)
    return jnp.concatenate([o[hh * c:(hh + 1) * c] for hh in range(DN_HEADS)], axis=1)


def _dn_kernel(xf_ref, gf_ref, xb_ref, gb_ref, trif_ref, trib_ref, mcf_ref, msf_ref, mcb_ref, msb_ref, eye_ref,
               of_ref, ob_ref, s_ref):
    @pl.when(pl.program_id(1) == 0)
    def _():
        s_ref[...] = jnp.zeros_like(s_ref)

    eye = eye_ref[...]
    of_ref[...] = _dn_direction(0, xf_ref[...], gf_ref[...], trif_ref[...], mcf_ref[...], msf_ref[...], eye,
                                s_ref.at[0]).astype(BF16)
    ob_ref[...] = _dn_direction(1, xb_ref[...], gb_ref[...], trib_ref[...], mcb_ref[...], msb_ref[...], eye,
                                s_ref.at[1]).astype(BF16)


def _deltanet(qkvn, gb, n_ctx_chunks):
    bsz, t, _ = qkvn.shape
    c = DN_CHUNK
    nc = t // c
    hc = DN_HEADS * c
    bmap = lambda i: jnp.where(i < n_ctx_chunks, n_ctx_chunks - 1 - i, nc + n_ctx_chunks - 1 - i)
    ii = jnp.arange(c)
    tri_f = (ii[None, :] <= ii[:, None]).astype(F32)
    tri_b = (ii[None, :] >= ii[:, None]).astype(F32)
    r = jnp.arange(hc)
    same = (r[:, None] // c) == (r[None, :] // c)
    pi, pj = r[:, None] % c, r[None, :] % c
    mc_f = (same & (pj <= pi)).astype(F32)
    ms_f = (same & (pj < pi)).astype(F32)
    mc_b = (same & (pj >= pi)).astype(F32)
    ms_b = (same & (pj > pi)).astype(F32)
    eye = jnp.eye(hc, dtype=F32)
    full = lambda a: pl.BlockSpec(a.shape, lambda b, i: (0,) * a.ndim)
    fw = lambda w: pl.BlockSpec((None, c, w), lambda b, i: (b, i, 0))
    bw = lambda w: pl.BlockSpec((None, c, w), lambda b, i: (b, bmap(i), 0))
    ow = DN_HEADS * DN_DV
    sds = jax.ShapeDtypeStruct
    return pl.pallas_call(
        _dn_kernel,
        grid=(bsz, nc),
        in_specs=[fw(QKV_W), fw(AB_W), bw(QKV_W), bw(AB_W), full(tri_f), full(tri_b),
                  full(mc_f), full(ms_f), full(mc_b), full(ms_b), full(eye)],
        out_specs=[fw(ow), bw(ow)],
        out_shape=[sds((bsz, t, ow), BF16), sds((bsz, t, ow), BF16)],
        scratch_shapes=[pltpu.VMEM((2, DN_DK, DN_HEADS * DN_DV), F32)],
        compiler_params=_cparams(("arbitrary", "arbitrary")),
    )(qkvn, gb, qkvn, gb, tri_f, tri_b, mc_f, ms_f, mc_b, ms_b, eye)


def _softmax_av(q, k, v):
    s = _dot_nt(q, k)
    m = jnp.max(s, axis=-1, keepdims=True)
    p = jnp.exp(s - m)
    l = jnp.sum(p, axis=-1, keepdims=True)
    return _dot(p.astype(BF16), v) / l


def _mla_kernel(q_ref, k_ref, v_ref, o_ref):
    j = pl.program_id(2)
    tc = q_ref.shape[0]

    @pl.when(j == 0)
    def _():
        o_ref[...] = _softmax_av(q_ref[...], k_ref[0:tc, :], v_ref[0:tc, :]).astype(BF16)

    @pl.when(j > 0)
    def _():
        o_ref[...] = _softmax_av(q_ref[...], k_ref[...], v_ref[...]).astype(BF16)


def _mla(q, k, v):
    bsz, nh, t, _ = q.shape
    tq = ROW_TILE
    return pl.pallas_call(
        _mla_kernel,
        grid=(bsz, nh, t // tq),
        in_specs=[pl.BlockSpec((None, None, tq, QK_W), lambda b, hh, j: (b, hh, j, 0)),
                  pl.BlockSpec((None, None, t, QK_W), lambda b, hh, j: (b, hh, 0, 0)),
                  pl.BlockSpec((None, None, t, MLA_V), lambda b, hh, j: (b, hh, 0, 0))],
        out_specs=pl.BlockSpec((None, tq, MLA_V), lambda b, hh, j: (b, j, hh)),
        out_shape=jax.ShapeDtypeStruct((bsz, t, nh * MLA_V), BF16),
        compiler_params=_cparams(("parallel", "parallel", "arbitrary")),
    )(q, k, v)


def _pack_halves(x):
    w = x.shape[1] // 2
    lo = lax.bitcast_convert_type(x[:, :w].astype(BF16).astype(F32), jnp.uint32)
    hi = lax.bitcast_convert_type(x[:, w:].astype(BF16).astype(F32), jnp.uint32)
    return (hi & jnp.uint32(0xFFFF0000)) | (lo >> 16)


def _unpack_halves(p):
    lo = lax.bitcast_convert_type(p << 16, F32)
    hi = lax.bitcast_convert_type(p & jnp.uint32(0xFFFF0000), F32)
    return lo, hi


def _merge_kernel(h_ref, mod_ref, g1_ref, of_ref, ob_ref, z_ref, dng_ref, ysc_ref, ymla_ref,
                  wg_ref, bg_ref, wdn_ref, wsc_ref, wmla_ref, wout_ref, g2_ref, rw_ref, rwh_ref, rb_ref, tril_ref, triu_ref,
                  hn_ref, xm2_ref, tg_ref, lrank_ref, info_ref, cnt_ref, cnt_scr):
    @pl.when(jnp.logical_and(pl.program_id(0) == 0, pl.program_id(1) == 0))
    def _():
        cnt_scr[...] = jnp.zeros_like(cnt_scr)

    d = h_ref.shape[-1]
    x = h_ref[...]
    xm = _rms(x, g1_ref[...]) * (1.0 + mod_ref[1:2, :]) + mod_ref[0:1, :]
    gates = _sigmoid(_dot(xm.astype(BF16), wg_ref[...]) + bg_ref[...])
    o = of_ref[...].astype(F32) + ob_ref[...].astype(F32)
    z = z_ref[...].astype(F32)
    parts = []
    for hh in range(DN_HEADS):
        oh = o[:, hh * DN_DV:(hh + 1) * DN_DV]
        zh = z[:, hh * DN_DV:(hh + 1) * DN_DV]
        parts.append(_rms(oh, dng_ref[...]) * (zh * _sigmoid(zh)))
    ydn = jnp.concatenate(parts, axis=1).astype(BF16)
    merged = (gates[:, 0:d] * _dot(ydn, wdn_ref[...]) + gates[:, d:2 * d] * _dot(ysc_ref[...], wsc_ref[...])
              + gates[:, 2 * d:3 * d] * _dot(ymla_ref[...], wmla_ref[...]))
    hn = x + mod_ref[2:3, :] * _dot(merged.astype(BF16), wout_ref[...])
    hn_ref[...] = hn
    xm2 = _rms(hn, g2_ref[...]) * (1.0 + mod_ref[4:5, :]) + mod_ref[3:4, :]
    xm2_ref[...] = xm2.astype(BF16)

    xh = xm2.astype(BF16)
    xl = (xm2 - xh.astype(F32)).astype(BF16)
    a = _dot(xh, rw_ref[...])
    logits = a[:, 0:LANE] + a[:, LANE:2 * LANE] + _dot(xl, rwh_ref[...]) + rb_ref[...]
    lane = lax.broadcasted_iota(jnp.int32, logits.shape, 1)
    work = jnp.where(lane < N_EXPERTS, logits, -jnp.inf)
    tv = jnp.zeros(logits.shape, F32)
    vals, idxs = [], []
    for kk in range(TOP_K):
        m = jnp.max(work, axis=-1, keepdims=True)
        idx = jnp.min(jnp.where(work == m, lane, LANE), axis=-1, keepdims=True)
        work = jnp.where(lane == idx, -jnp.inf, work)
        vals.append(m)
        idxs.append(idx)
    es = [jnp.exp(vv - vals[0]) for vv in vals]
    tot = es[0] + es[1] + es[2] + es[3]
    for kk in range(TOP_K):
        tv = jnp.where(lane == kk, es[kk] / tot, tv)
    tg_ref[...] = tv

    sel = jnp.zeros(logits.shape, F32)
    for idx in idxs:
        sel = sel + (lane == idx).astype(F32)
    tcnt = jnp.floor((jnp.sum(sel, axis=0, keepdims=True) + (RUN_ALIGN - 1.0)) * (1.0 / RUN_ALIGN)) * RUN_ALIGN
    tstart = _dot(jnp.broadcast_to(tcnt, (8, LANE)).astype(BF16), triu_ref[...])[0:1]
    lpos = _dot(tril_ref[...], sel.astype(BF16)) + tstart
    lrank = jnp.zeros(logits.shape, F32)
    for kk in range(TOP_K):
        lp = jnp.sum(jnp.where(lane == idxs[kk], lpos, 0.0), axis=-1, keepdims=True)
        lrank = jnp.where(lane == kk, lp, lrank)
    lrank_ref[...] = lrank.astype(jnp.int32)
    sub = lax.broadcasted_iota(jnp.int32, (8, LANE), 0)
    info = jnp.where(sub == 0, tcnt, jnp.where(sub == 1, tstart, jnp.where(sub == 2, cnt_scr[...], 0.0)))
    info_ref[...] = info.astype(jnp.int32)
    total = cnt_scr[...] + tcnt
    cnt_scr[...] = total
    cnt_ref[...] = total


def _merge(h, mod, g1, o_f, o_b, z, dng, ysc, ymla, wg, bg, wdn, wsc, wmla, wout, g2, rw_cat, rw_hi, rb):
    bsz, t, d = h.shape
    tm = ROW_TILE
    row = lambda w: pl.BlockSpec((None, tm, w), lambda b, j: (b, j, 0))
    full = lambda a: pl.BlockSpec(a.shape, lambda b, j: (0,) * a.ndim)
    ow = DN_HEADS * DN_DV
    sds = jax.ShapeDtypeStruct
    ii = jnp.arange(tm)
    tril = (ii[None, :] < ii[:, None]).astype(BF16)
    ee = jnp.arange(LANE)
    triu = (ee[:, None] < ee[None, :]).astype(BF16)
    tile_i32 = pl.BlockSpec((None, None, 8, LANE), lambda b, j: (b, j, 0, 0))
    return pl.pallas_call(
        _merge_kernel,
        grid=(bsz, t // tm),
        in_specs=[row(d), pl.BlockSpec((None, None, 6, d), lambda b, j: (b, jnp.minimum(j, 1), 0, 0)), full(g1),
                  row(ow), row(ow), row(Z_W), full(dng), row(SC_WIDTH), row(MLA_HEADS * MLA_V),
                  full(wg), full(bg), full(wdn), full(wsc), full(wmla), full(wout), full(g2),
                  full(rw_cat), full(rw_hi), full(rb), full(tril), full(triu)],
        out_specs=[row(d), row(d), row(LANE), row(LANE), tile_i32,
                   pl.BlockSpec((1, LANE), lambda b, j: (0, 0))],
        out_shape=[sds((bsz, t, d), F32), sds((bsz, t, d), BF16), sds((bsz, t, LANE), F32),
                   sds((bsz, t, LANE), jnp.int32), sds((bsz, t // tm, 8, LANE), jnp.int32), sds((1, LANE), F32)],
        scratch_shapes=[pltpu.VMEM((1, LANE), F32)],
        compiler_params=_cparams(("arbitrary", "arbitrary")),
    )(h, mod, g1, o_f, o_b, z, dng, ysc, ymla, wg, bg, wdn, wsc, wmla, wout, g2, rw_cat, rw_hi, rb, tril, triu)


GLU_GROUP = 2 * LANE


def _deinterleave_kernel(w_ref, p_ref, o_ref):
    perm = p_ref[...]
    for g in range(w_ref.shape[1] // GLU_GROUP):
        sl = slice(g * GLU_GROUP, (g + 1) * GLU_GROUP)
        o_ref[:, sl] = _dot(w_ref[:, sl].astype(BF16), perm).astype(BF16)


def _deinterleave_w1(expert_w1, l):
    _, n_e, d, f2 = expert_w1.shape
    i = jnp.arange(GLU_GROUP)
    perm = (jnp.where(i % 2 == 0, i // 2, LANE + i // 2)[:, None] == i[None, :]).astype(BF16)
    return pl.pallas_call(
        _deinterleave_kernel,
        grid=(n_e,),
        in_specs=[pl.BlockSpec((None, None, d, f2), lambda e: (l, e, 0, 0)),
                  pl.BlockSpec(perm.shape, lambda e: (0, 0))],
        out_specs=pl.BlockSpec((None, d, f2), lambda e: (e, 0, 0)),
        out_shape=jax.ShapeDtypeStruct((n_e, d, f2), BF16),
        compiler_params=_cparams(("arbitrary",)),
    )(expert_w1, perm)


RUN_PIECES = tuple(RUN_ALIGN << i for i in range((LOCAL_ROWS // RUN_ALIGN).bit_length()))


def _run_copies(run_ref, make_copy):
    def per_run(e, carry):
        cnt, src, dst = run_ref[0, e], run_ref[1, e], run_ref[2, e]
        off = jnp.int32(0)
        for piece in RUN_PIECES:
            has = (cnt & piece) != 0

            @pl.when(has)
            def _(off=off, piece=piece):
                make_copy(pl.multiple_of(src + off, RUN_ALIGN), pl.multiple_of(dst + off, RUN_ALIGN), piece).start()
            off = off + jnp.where(has, piece, 0)
        return carry
    lax.fori_loop(0, N_RUNS, per_run, 0)


def _dispatch_kernel(run_ref, x_ref, lrank_ref, xs_in_hbm, xs_hbm, loc, sem):
    del xs_in_hbm
    tm = x_ref.shape[0]
    n_loc = LOCAL_ROWS
    lr = lrank_ref[...].astype(F32).T
    rows = lax.broadcasted_iota(jnp.int32, (n_loc, tm), 0).astype(F32)
    onehot = jnp.zeros((n_loc, tm), F32)
    for kk in range(TOP_K):
        onehot = onehot + jnp.where(rows == lr[kk:kk + 1, :], 1.0, 0.0)
    loc[...] = _pack_halves(_dot(onehot.astype(BF16), x_ref[...]))
    _run_copies(run_ref, lambda s, d, n: pltpu.make_async_copy(loc.at[pl.ds(s, n)], xs_hbm.at[pl.ds(d, n)], sem))
    pltpu.make_async_copy(loc, xs_hbm.at[pl.ds(0, n_loc)], sem).wait()


def _dispatch(runs, x2, lrank, n_rows):
    bsz, t, d = x2.shape
    tm = ROW_TILE
    nj = t // tm
    w = d // 2
    xs0 = jnp.zeros((n_rows, w), jnp.uint32)
    return pl.pallas_call(
        _dispatch_kernel,
        grid=(bsz, nj),
        in_specs=[pl.BlockSpec((None, None, 3, N_RUNS), lambda b, j: (b, j, 0, 0), memory_space=pltpu.SMEM),
                  pl.BlockSpec((None, tm, d), lambda b, j: (b, j, 0)),
                  pl.BlockSpec((None, tm, LANE), lambda b, j: (b, j, 0)),
                  pl.BlockSpec(memory_space=pl.ANY)],
        out_specs=pl.BlockSpec(memory_space=pl.ANY),
        out_shape=jax.ShapeDtypeStruct((n_rows, w), jnp.uint32),
        scratch_shapes=[pltpu.VMEM((LOCAL_ROWS, w), jnp.uint32), pltpu.SemaphoreType.DMA(())],
        input_output_aliases={3: 0},
        compiler_params=_cparams(("arbitrary", "arbitrary")),
    )(runs, x2, lrank, xs0)


def _expert_kernel(be_ref, nreal_ref, xs_ref, w1_ref, b1_ref, w2_ref, b2_ref, y_ref):
    i = pl.program_id(0)
    half = w1_ref.shape[0] // 2

    @pl.when(i < nreal_ref[0])
    def _():
        lo, hi = _unpack_halves(xs_ref[...])
        hdn = (_dot(lo.astype(BF16), w1_ref[0:half, :]) + _dot(hi.astype(BF16), w1_ref[half:, :])) + b1_ref[...]
        acts = []
        for g in range(hdn.shape[1] // GLU_GROUP):
            glu = jnp.minimum(hdn[:, g * GLU_GROUP:g * GLU_GROUP + LANE], SWIGLU_LIMIT)
            lin = jnp.clip(hdn[:, g * GLU_GROUP + LANE:(g + 1) * GLU_GROUP], -SWIGLU_LIMIT, SWIGLU_LIMIT)
            acts.append((glu * _sigmoid(SWIGLU_ALPHA * glu) * (lin + 1.0)).astype(BF16))
        y_ref[...] = _pack_halves(_dot(jnp.concatenate(acts, axis=1), w2_ref[...]) + b2_ref[...])

    @pl.when(i >= nreal_ref[0])
    def _():
        y_ref[...] = jnp.zeros_like(y_ref)


def _experts(block_expert, n_real, xs, w1, b1, w2, b2):
    n_blocks = block_expert.shape[0]
    f, d = w2.shape[1:]
    eb = EXPERT_BLOCK
    wspec = lambda k, n: pl.BlockSpec((None, k, n), lambda i, be, nr: (be[i], 0, 0))
    gs = pltpu.PrefetchScalarGridSpec(
        num_scalar_prefetch=2,
        grid=(n_blocks,),
        in_specs=[pl.BlockSpec((eb, d // 2), lambda i, be, nr: (jnp.minimum(i, nr[0] - 1), 0)),
                  wspec(d, 2 * f), wspec(1, 2 * f), wspec(f, d), wspec(1, d)],
        out_specs=pl.BlockSpec((eb, d // 2), lambda i, be, nr: (i, 0)),
    )
    return pl.pallas_call(
        _expert_kernel,
        grid_spec=gs,
        out_shape=jax.ShapeDtypeStruct((n_blocks * eb, d // 2), jnp.uint32),
        compiler_params=_cparams(("arbitrary",)),
    )(block_expert, n_real, xs, w1, b1, w2, b2)


def _combine_kernel(final, run_ref, lrank_ref, y_hbm, h_ref, mod_ref, tg_ref, fg_ref, o_ref, loc, sem):
    tm = h_ref.shape[0]
    n_loc = LOCAL_ROWS
    _run_copies(run_ref, lambda s, d, n: pltpu.make_async_copy(y_hbm.at[pl.ds(d, n)], loc.at[pl.ds(s, n)], sem))
    lr = lrank_ref[...]
    tg = tg_ref[...]
    cols = lax.broadcasted_iota(jnp.int32, (tm, n_loc), 1)
    wsel = jnp.zeros((tm, n_loc), F32)
    for kk in range(TOP_K):
        wsel = wsel + jnp.where(cols == lr[:, kk:kk + 1], tg[:, kk:kk + 1], 0.0)
    wsel = wsel.astype(BF16)
    pltpu.make_async_copy(y_hbm.at[pl.ds(0, n_loc)], loc, sem).wait()
    lo, hi = _unpack_halves(loc[...])
    y = jnp.concatenate([_dot(wsel, lo.astype(BF16)), _dot(wsel, hi.astype(BF16))], axis=1)
    hn = h_ref[...] + mod_ref[5:6, :] * y
    if final:
        hn = _rms(hn, fg_ref[...])
    o_ref[...] = hn


def _combine(runs, lrank, y_rows, h, mod, tg, fg, final):
    bsz, t, d = h.shape
    tm = ROW_TILE
    nj = t // tm
    row = lambda w: pl.BlockSpec((None, tm, w), lambda b, j: (b, j, 0))
    return pl.pallas_call(
        functools.partial(_combine_kernel, final),
        grid=(bsz, nj),
        in_specs=[pl.BlockSpec((None, None, 3, N_RUNS), lambda b, j: (b, j, 0, 0), memory_space=pltpu.SMEM),
                  row(LANE), pl.BlockSpec(memory_space=pl.ANY),
                  row(d), pl.BlockSpec((None, None, 6, d), lambda b, j: (b, jnp.minimum(j, 1), 0, 0)),
                  row(LANE), pl.BlockSpec(fg.shape, lambda b, j: (0, 0))],
        out_specs=row(d),
        out_shape=jax.ShapeDtypeStruct((bsz, t, d), F32),
        scratch_shapes=[pltpu.VMEM((LOCAL_ROWS, d // 2), jnp.uint32), pltpu.SemaphoreType.DMA(())],
        compiler_params=_cparams(("arbitrary", "arbitrary")),
    )(runs, lrank, y_rows, h, mod, tg, fg)


def _route(info, cnt, n_assign):
    eb = EXPERT_BLOCK
    n_tiles = info.shape[0] * info.shape[1]
    max_rows = n_assign + n_tiles * N_EXPERTS * (RUN_ALIGN - 1)
    n_blocks = -(-max_rows // eb) + N_EXPERTS + -(-LOCAL_ROWS // eb)
    filler_row = n_blocks * eb - LOCAL_ROWS
    counts = cnt[0, :N_EXPERTS].astype(jnp.int32)
    padded = (counts + eb - 1) // eb * eb
    pad_end = jnp.cumsum(padded)
    pad_start = pad_end - padded
    tcnt, tstart, tcarry = (info[:, :, r, :N_EXPERTS] for r in range(3))
    used = jnp.sum(tcnt, axis=-1, keepdims=True)
    runs = jnp.stack([jnp.concatenate([tcnt, LOCAL_ROWS - used], axis=-1),
                      jnp.concatenate([tstart, used], axis=-1),
                      jnp.concatenate([pad_start + tcarry, jnp.full_like(used, filler_row)], axis=-1)],
                     axis=2).astype(jnp.int32)
    block_row = jnp.arange(n_blocks, dtype=jnp.int32) * eb
    block_expert = jnp.minimum(jnp.sum((pad_end[None, :] <= block_row[:, None]).astype(jnp.int32), axis=1),
                               N_EXPERTS - 1)
    n_real = (pad_end[-1] // eb).astype(jnp.int32).reshape(1)
    return runs, block_expert, n_real, n_blocks


def _rope_tables(t_lat, t_ctx):
    rows = t_lat // GRID_W
    row = jnp.repeat(jnp.arange(rows, dtype=F32), GRID_W)
    col = jnp.tile(jnp.arange(GRID_W, dtype=F32), rows)
    axis_dims = MLA_ROPE // 2
    inv = ROPE_BASE ** (-jnp.arange(0, axis_dims, 2, dtype=F32) / axis_dims)
    ar, ac = row[:, None] * inv, col[:, None] * inv
    cos64 = jnp.concatenate([jnp.cos(ar), jnp.cos(ar), jnp.cos(ac), jnp.cos(ac)], axis=-1)
    sin64 = jnp.concatenate([-jnp.sin(ar), jnp.sin(ar), -jnp.sin(ac), jnp.sin(ac)], axis=-1)
    pad = jnp.zeros((t_lat, LANE - MLA_ROPE), F32)
    cos_l = jnp.concatenate([cos64, pad], axis=-1)
    sin_l = jnp.concatenate([sin64, pad], axis=-1)
    cos_c = jnp.concatenate([jnp.ones((t_ctx, MLA_ROPE), F32), jnp.zeros((t_ctx, LANE - MLA_ROPE), F32)], axis=-1)
    sin_c = jnp.zeros((t_ctx, LANE), F32)
    return jnp.concatenate([cos_c, cos_l], axis=0), jnp.concatenate([sin_c, sin_l], axis=0)


def _rope_swap_perm():
    half = MLA_ROPE // 4
    perm = []
    for a in range(2):
        base = a * 2 * half
        perm += list(range(base + half, base + 2 * half)) + list(range(base, base + half))
    return jnp.array(perm, jnp.int32)


def _pad_cols(w, width):
    return jnp.concatenate([w, jnp.zeros(w.shape[:-1] + (width - w.shape[-1],), w.dtype)], axis=-1)


def _layer_weights(l, w_in, dn_a_log, dn_dt_bias, mla_w_qb, router_w, router_b, expert_w1, expert_b1):
    nqk = DN_HEADS * DN_DK
    wi = w_in[l]
    o = 0
    dn_qkv = wi[:, o:o + QKV_W]; o += QKV_W
    dn_z = wi[:, o:o + Z_W]; o += Z_W
    dn_ab = wi[:, o:o + 4 * DN_HEADS]; o += 4 * DN_HEADS
    sc3 = wi[:, o:o + SC3_W]; o += SC3_W
    qa = wi[:, o:o + MLA_Q_LORA]; o += MLA_Q_LORA
    ckv = wi[:, o:o + MLA_KV_LORA]; o += MLA_KV_LORA
    kr = wi[:, o:o + MLA_ROPE]
    perm = _rope_swap_perm()
    w_cat = jnp.concatenate([dn_qkv, dn_z, sc3, qa, ckv, _pad_cols(kr, LANE), _pad_cols(kr[:, perm], LANE),
                             _pad_cols(dn_ab, LANE)], axis=-1).astype(BF16)
    dq = MLA_NOPE + MLA_ROPE
    wq = mla_w_qb[l].reshape(MLA_Q_LORA, MLA_HEADS, dq)
    wq_rope = wq[:, :, MLA_NOPE:]
    wq_cat = jnp.concatenate([wq[:, :, :MLA_NOPE], _pad_cols(wq_rope, LANE), _pad_cols(wq_rope[:, :, perm], LANE)],
                             axis=-1).reshape(MLA_Q_LORA, MLA_HEADS * QH_W).astype(BF16)
    alog_row = _pad_cols(jnp.concatenate([dn_a_log[l].reshape(1, -1), jnp.zeros((1, 2 * DN_HEADS), F32)], -1), LANE)
    dtb_row = _pad_cols(jnp.concatenate([dn_dt_bias[l].reshape(1, -1), jnp.zeros((1, 2 * DN_HEADS), F32)], -1), LANE)
    rw = _pad_cols(router_w[l], LANE)
    rw_hi = rw.astype(BF16)
    rw_lo = (rw - rw_hi.astype(F32)).astype(BF16)
    rw_cat = jnp.concatenate([rw_hi, rw_lo], axis=-1)
    rb = jnp.concatenate([router_b[l], jnp.full((LANE - N_EXPERTS,), 0.0, F32)]).reshape(1, LANE)
    w1 = _deinterleave_w1(expert_w1, l)
    n_e, f2 = expert_b1.shape[1:]
    b1 = expert_b1[l].reshape(n_e, f2 // GLU_GROUP, LANE, 2).transpose(0, 1, 3, 2).reshape(n_e, 1, f2)
    return w_cat, wq_cat, alog_row, dtb_row, rw_cat, rw_hi, rb, w1, b1


def kernel(x, c, ctx, c_ctx, ada_w, ada_b, norm1_g, norm2_g, w_in, dn_conv_w, dn_a_log, dn_dt_bias, dn_norm_g, sc_conv_w, mla_q_norm_g, mla_w_qb, mla_kv_norm_g, mla_w_kvb, w_branch_gate, b_branch_gate, w_branch_dn, w_branch_sc, w_branch_mla, w_out, router_w, router_b, expert_w1, expert_b1, expert_w2, expert_b2, final_norm_g):
    bsz, t_lat, d = x.shape
    t_ctx = ctx.shape[1]
    depth = ada_w.shape[0]
    assert t_ctx == ROW_TILE and t_lat % ROW_TILE == 0 and t_lat % GRID_W == 0
    t = t_ctx + t_lat

    cos_t, sin_t = _rope_tables(t_lat, t_ctx)
    n_mod_rows = -(-(bsz + 1) // 8) * 8
    cvec = jnp.concatenate([c, c_ctx[None], jnp.zeros((n_mod_rows - bsz - 1, d), F32)], axis=0)
    mods = _ada_mods(cvec, ada_w, ada_b)

    h = jnp.concatenate([ctx, x], axis=1)
    row2 = lambda v: v.reshape(1, -1)
    for l in range(depth):
        mod_lat = mods[l, :bsz].reshape(bsz, 1, 6, d)
        mod_ctx = jnp.broadcast_to(mods[l, bsz].reshape(1, 1, 6, d), (bsz, 1, 6, d))
        mod = jnp.concatenate([mod_ctx, mod_lat], axis=1)
        (w_cat, wq_cat, alog_row, dtb_row, rw_cat, rw_hi, rb, w1, b1) = _layer_weights(
            l, w_in, dn_a_log, dn_dt_bias, mla_w_qb, router_w, router_b, expert_w1, expert_b1)

        qkv, z, sc, ab, q, k, v = _inproj(h, mod, row2(norm1_g[l]), w_cat, row2(mla_q_norm_g[l]), wq_cat,
                                          row2(mla_kv_norm_g[l]), mla_w_kvb[l].astype(BF16), cos_t, sin_t)
        qkvn, ysc, gb = _prep(qkv, sc, ab, dn_conv_w[l], sc_conv_w[l], alog_row, dtb_row)
        o_f, o_b = _deltanet(qkvn, gb, t_ctx // DN_CHUNK)
        ymla = _mla(q, k, v)
        hn, xm2, tg, lrank, info, cnt = _merge(h, mod, row2(norm1_g[l]), o_f, o_b, z, row2(dn_norm_g[l]), ysc, ymla,
                                 w_branch_gate[l].astype(BF16), row2(b_branch_gate[l]),
                                 w_branch_dn[l].astype(BF16), w_branch_sc[l].astype(BF16),
                                 w_branch_mla[l].astype(BF16), w_out[l].astype(BF16), row2(norm2_g[l]),
                                 rw_cat, rw_hi, rb)
        runs, block_expert, n_real, n_blocks = _route(info, cnt, bsz * t * TOP_K)
        xs = _dispatch(runs, xm2, lrank, n_blocks * EXPERT_BLOCK)
        y_rows = _experts(block_expert, n_real, xs, w1, b1, expert_w2[l].astype(BF16), expert_b2[l][:, None, :])
        h = _combine(runs, lrank, y_rows, hn, mod, tg, row2(final_norm_g), final=(l == depth - 1))
    return h[:, t_ctx:, :]
```

```python
import functools
import math

import jax
import jax.numpy as jnp
from jax import lax
from jax.experimental import pallas as pl
from jax.experimental.pallas import tpu as pltpu

F32 = jnp.float32
BF16 = jnp.bfloat16

GRID_W = 64
NORM_EPS = 1e-6
DN_HEADS = 4
DN_DK = 128
DN_DV = 128
DN_CHUNK = 64
SC_WIDTH = 512
MLA_HEADS = 4
MLA_Q_LORA = 256
MLA_KV_LORA = 128
MLA_NOPE = 128
MLA_ROPE = 64
MLA_V = 128
MLA_SCALE = (MLA_NOPE + MLA_ROPE) ** -0.5
Q_SCALE = MLA_SCALE * math.log2(math.e)
ROPE_BASE = 10000.0
N_EXPERTS = 32
TOP_K = 4
EXPERT_FF = 1024
SWIGLU_ALPHA = 1.702
SWIGLU_LIMIT = 7.0
EXPERT_BLOCK = 256

LANE = 128
ROW_TILE = 256
HALO = 16
DN_BATCH = 4
RUN_ALIGN = 8
LOCAL_ROWS = TOP_K * ROW_TILE + N_EXPERTS * RUN_ALIGN
N_RUNS = N_EXPERTS + 1
VMEM_LIMIT = 56 * 1024 * 1024

QKV_W = 3 * DN_HEADS * DN_DK
Z_W = DN_HEADS * DN_DV
SC3_W = 3 * SC_WIDTH
KVA_W = 3 * LANE
AB_W = LANE
IN_W = QKV_W + Z_W + SC3_W + MLA_Q_LORA + KVA_W + AB_W
QH_W = 3 * LANE
QK_W = 2 * LANE


def _cparams(sem):
    return pltpu.CompilerParams(dimension_semantics=sem, vmem_limit_bytes=VMEM_LIMIT)


def _dot(a, b):
    return jnp.dot(a, b, preferred_element_type=F32)


def _dot_nt(a, b):
    return lax.dot_general(a, b, (((1,), (1,)), ((), ())), preferred_element_type=F32)


def _rms(x, g):
    return x * lax.rsqrt(jnp.mean(x * x, axis=-1, keepdims=True) + NORM_EPS) * g


def _sigmoid(x):
    return 1.0 / (1.0 + jnp.exp(-x))


def _ada_kernel(c_ref, w_ref, b_ref, o_ref):
    cv = c_ref[...]
    s = cv * _sigmoid(cv)
    o_ref[...] = _dot(s.astype(BF16), w_ref[...].astype(BF16)) + b_ref[...]


def _ada_mods(cvec, ada_w, ada_b):
    n_layers, d, d6 = ada_w.shape
    r = cvec.shape[0]
    tn = 512
    return pl.pallas_call(
        _ada_kernel,
        grid=(n_layers, d6 // tn),
        in_specs=[pl.BlockSpec((r, d), lambda l, n: (0, 0)),
                  pl.BlockSpec((None, d, tn), lambda l, n: (l, 0, n)),
                  pl.BlockSpec((None, 1, tn), lambda l, n: (l, 0, n))],
        out_specs=pl.BlockSpec((None, r, tn), lambda l, n: (l, 0, n)),
        out_shape=jax.ShapeDtypeStruct((n_layers, r, d6), F32),
        compiler_params=_cparams(("arbitrary", "arbitrary")),
    )(cvec, ada_w, ada_b.reshape(n_layers, 1, d6))


def _inproj_kernel(h_ref, mod_ref, g1_ref, w_ref, gq_ref, wq_ref, gkv_ref, wkv_ref, cos_ref, sin_ref,
                   qkv_ref, z_ref, sc_ref, ab_ref, q_ref, k_ref, v_ref):
    x = h_ref[...]
    xm = _rms(x, g1_ref[...]) * (1.0 + mod_ref[1:2, :]) + mod_ref[0:1, :]
    p = _dot(xm.astype(BF16), w_ref[...])
    o = 0
    qkv_ref[...] = p[:, o:o + QKV_W].astype(BF16)
    o += QKV_W
    z_ref[...] = p[:, o:o + Z_W].astype(BF16)
    o += Z_W
    sc_ref[...] = p[:, o:o + SC3_W].astype(BF16)
    o += SC3_W
    qa = p[:, o:o + MLA_Q_LORA]
    o += MLA_Q_LORA
    ckv = p[:, o:o + LANE]
    kr = p[:, o + LANE:o + 2 * LANE]
    krs = p[:, o + 2 * LANE:o + 3 * LANE]
    o += KVA_W
    ab_ref[...] = p[:, o:o + AB_W]

    cos = cos_ref[...]
    sin = sin_ref[...]
    k_rope = (kr * cos + krs * sin).astype(BF16)
    qn = _rms(qa, gq_ref[...]).astype(BF16)
    qf = _dot(qn, wq_ref[...])
    kvn = _rms(ckv, gkv_ref[...]).astype(BF16)
    kv = _dot(kvn, wkv_ref[...])
    for hh in range(MLA_HEADS):
        b0 = hh * QH_W
        q_ref[hh, :, 0:LANE] = (qf[:, b0:b0 + LANE] * Q_SCALE).astype(BF16)
        q_rope = qf[:, b0 + LANE:b0 + 2 * LANE] * cos + qf[:, b0 + 2 * LANE:b0 + 3 * LANE] * sin
        q_ref[hh, :, LANE:2 * LANE] = (q_rope * Q_SCALE).astype(BF16)
        c0 = hh * (MLA_NOPE + MLA_V)
        k_ref[hh, :, 0:LANE] = kv[:, c0:c0 + MLA_NOPE].astype(BF16)
        k_ref[hh, :, LANE:2 * LANE] = k_rope
        v_ref[hh] = kv[:, c0 + MLA_NOPE:c0 + MLA_NOPE + MLA_V].astype(BF16)


def _inproj(h, mod, g1, w_cat, gq, wq_cat, gkv, wkv, cos_t, sin_t):
    bsz, t, d = h.shape
    tm = ROW_TILE
    nj = t // tm
    row = lambda w: pl.BlockSpec((None, tm, w), lambda b, j: (b, j, 0))
    full = lambda a: pl.BlockSpec(a.shape, lambda b, j: (0,) * a.ndim)
    head = lambda w: pl.BlockSpec((None, MLA_HEADS, tm, w), lambda b, j: (b, 0, j, 0))
    sds = jax.ShapeDtypeStruct
    return pl.pallas_call(
        _inproj_kernel,
        grid=(bsz, nj),
        in_specs=[row(d),
                  pl.BlockSpec((None, None, 6, d), lambda b, j: (b, jnp.minimum(j, 1), 0, 0)),
                  full(g1), full(w_cat), full(gq), full(wq_cat), full(gkv), full(wkv),
                  pl.BlockSpec((tm, LANE), lambda b, j: (j, 0)),
                  pl.BlockSpec((tm, LANE), lambda b, j: (j, 0))],
        out_specs=[row(QKV_W), row(Z_W), row(SC3_W), row(AB_W), head(QK_W), head(QK_W), head(MLA_V)],
        out_shape=[sds((bsz, t, QKV_W), BF16), sds((bsz, t, Z_W), BF16), sds((bsz, t, SC3_W), BF16),
                   sds((bsz, t, AB_W), F32),
                   sds((bsz, MLA_HEADS, t, QK_W), BF16), sds((bsz, MLA_HEADS, t, QK_W), BF16),
                   sds((bsz, MLA_HEADS, t, MLA_V), BF16)],
        compiler_params=_cparams(("parallel", "arbitrary")),
    )(h, mod, g1, w_cat, gq, wq_cat, gkv, wkv, cos_t, sin_t)


def _shift_rows(x, prev_row, next_row):
    tm = x.shape[0]
    rid = lax.broadcasted_iota(jnp.int32, x.shape, 0)
    xp = jnp.where(rid == 0, prev_row, pltpu.roll(x, 1, 0))
    xn = jnp.where(rid == tm - 1, next_row, pltpu.roll(x, tm - 1, 0))
    return xp, xn


def _prep_kernel(qkv_ref, qkv_p_ref, qkv_n_ref, sc_ref, sc_p_ref, sc_n_ref, ab_ref,
                 dnw_ref, scw_ref, alog_ref, dtb_ref, qkvn_ref, ysc_ref, gb_ref):
    j = pl.program_id(1)
    nj = pl.num_programs(1)
    pv = (j >= 2).astype(F32)
    nv = jnp.logical_and(j >= 1, j <= nj - 2).astype(F32)

    x = qkv_ref[...].astype(F32)
    xp, xn = _shift_rows(x, qkv_p_ref[HALO - 1:HALO, :].astype(F32) * pv, qkv_n_ref[0:1, :].astype(F32) * nv)
    y = xp * dnw_ref[0:1, :] + x * dnw_ref[1:2, :] + xn * dnw_ref[2:3, :]
    y = y * _sigmoid(y)
    nqk = DN_HEADS * DN_DK
    for g in range(2 * DN_HEADS):
        yy = y[:, g * DN_DK:(g + 1) * DN_DK]
        yy = yy * lax.rsqrt(jnp.sum(yy * yy, axis=-1, keepdims=True) + 1e-6)
        if g < DN_HEADS:
            yy = yy * (DN_DK ** -0.5)
        qkvn_ref[:, g * DN_DK:(g + 1) * DN_DK] = yy.astype(BF16)
    qkvn_ref[:, 2 * nqk:] = y[:, 2 * nqk:].astype(BF16)

    s = sc_ref[...].astype(F32)
    sp = sc_p_ref[HALO - 1:HALO, :].astype(F32) * pv
    sn = sc_n_ref[0:1, :].astype(F32) * nv
    w = SC_WIDTH
    u = s[:, 2 * w:3 * w] * s[:, 0:w]
    up, un = _shift_rows(u, sp[:, 2 * w:3 * w] * sp[:, 0:w], sn[:, 2 * w:3 * w] * sn[:, 0:w])
    conv = up * scw_ref[0:1, :] + u * scw_ref[1:2, :] + un * scw_ref[2:3, :]
    ysc_ref[...] = (s[:, w:2 * w] * conv).astype(BF16)

    ab = ab_ref[...]
    sp_arg = ab + dtb_ref[...]
    softplus = jnp.maximum(sp_arg, 0.0) + jnp.log1p(jnp.exp(-jnp.abs(sp_arg)))
    gval = -jnp.exp(alog_ref[...]) * softplus
    lane = lax.broadcasted_iota(jnp.int32, ab.shape, 1)
    gb_ref[...] = jnp.where(lane < 2 * DN_HEADS, gval, _sigmoid(ab))


def _prep(qkv, sc, ab, dn_conv_w, sc_conv_w, alog_row, dtb_row):
    bsz, t, _ = qkv.shape
    tm = ROW_TILE
    nj = t // tm
    hb = tm // HALO
    nh = t // HALO
    row = lambda w: pl.BlockSpec((None, tm, w), lambda b, j: (b, j, 0))
    prev = lambda w: pl.BlockSpec((None, HALO, w), lambda b, j: (b, jnp.maximum(j * hb - 1, 0), 0))
    nxt = lambda w: pl.BlockSpec((None, HALO, w), lambda b, j: (b, jnp.minimum((j + 1) * hb, nh - 1), 0))
    full = lambda a: pl.BlockSpec(a.shape, lambda b, j: (0,) * a.ndim)
    sds = jax.ShapeDtypeStruct
    return pl.pallas_call(
        _prep_kernel,
        grid=(bsz, nj),
        in_specs=[row(QKV_W), prev(QKV_W), nxt(QKV_W), row(SC3_W), prev(SC3_W), nxt(SC3_W), row(AB_W),
                  full(dn_conv_w), full(sc_conv_w), full(alog_row), full(dtb_row)],
        out_specs=[row(QKV_W), row(SC_WIDTH), row(AB_W)],
        out_shape=[sds((bsz, t, QKV_W), BF16), sds((bsz, t, SC_WIDTH), BF16), sds((bsz, t, AB_W), F32)],
        compiler_params=_cparams(("parallel", "arbitrary")),
    )(qkv, qkv, qkv, sc, sc, sc, ab, dn_conv_w, sc_conv_w, alog_row, dtb_row)


DN_GROUP = 2


def _stack_heads(x, h0):
    return jnp.concatenate([x[:, (h0 + i) * LANE:(h0 + i + 1) * LANE] for i in range(DN_GROUP)], axis=0)


def _stack_cols(x, c0):
    return jnp.concatenate([x[:, c0 + i:c0 + i + 1] for i in range(DN_GROUP)], axis=0)


def _dn_direction(d, h0, x, gbt, tri, m_causal, m_strict, eye, s_ref):
    c = DN_CHUNK
    grp = range(DN_GROUP)
    nqk = DN_HEADS * DN_DK
    qst = _stack_heads(x[:, 0:nqk], h0)
    kst = _stack_heads(x[:, nqk:2 * nqk], h0)
    vst = _stack_heads(x[:, 2 * nqk:], h0)
    gc_all = jnp.dot(tri, gbt, preferred_element_type=F32, precision=lax.Precision.HIGHEST)
    col0 = d * DN_HEADS + h0
    gc = _stack_cols(gc_all, col0)
    beta = _stack_cols(gbt, 2 * DN_HEADS + col0)
    last = c - 1 if d == 0 else 0
    gl_heads = [gc_all[last:last + 1, col0 + i:col0 + i + 1] for i in grp]
    gl = jnp.concatenate([jnp.broadcast_to(g1, (c, 1)) for g1 in gl_heads], axis=0)
    kf = kst.astype(F32)
    kb = kf * beta
    kk = _dot_nt(kb.astype(BF16), kst)
    qk = _dot_nt(qst, kst)
    yield

    hc = DN_GROUP * c
    gmat = jnp.broadcast_to(gc, (hc, hc))
    dec = jnp.exp(jnp.minimum(gmat - gmat.T, 0.0))
    xpow = -(kk * jnp.where(m_strict > 0.5, dec, 0.0))
    tinv = eye + xpow
    intra = (qk * jnp.where(m_causal > 0.5, dec, 0.0)).astype(BF16)
    eg = jnp.exp(gc)
    rhs = jnp.concatenate([vst.astype(F32) * beta, kb * eg], axis=1).astype(BF16)
    qd = qst.astype(F32) * eg
    kdt = (kf * jnp.exp(gl - gc)).T.astype(BF16)
    for _ in range(int(math.log2(c)) - 1):
        xb = xpow.astype(BF16)
        xpow = _dot(xb, xb)
        yield
        tinv = tinv + _dot(tinv.astype(BF16), xpow.astype(BF16))
        yield
    sol = _dot(tinv.astype(BF16), rhs)
    yield
    u = sol[:, 0:DN_DV]
    w = sol[:, DN_DV:]

    s_cat = s_ref[...]
    wq = jnp.concatenate([w, qd], axis=0).astype(BF16)
    full = _dot(wq, s_cat.astype(BF16))
    yield
    ws = jnp.concatenate([full[i * c:(i + 1) * c, i * DN_DV:(i + 1) * DN_DV] for i in grp], axis=0)
    qs = jnp.concatenate([full[hc + i * c:hc + (i + 1) * c, i * DN_DV:(i + 1) * DN_DV] for i in grp], axis=0)
    vnew = u - ws
    o = qs + _dot(intra, vnew.astype(BF16))
    rhead = lax.broadcasted_iota(jnp.int32, (hc, DN_DV), 0) // c
    vblk = jnp.concatenate([jnp.where(rhead == i, vnew, 0.0) for i in grp], axis=1).astype(BF16)
    gt = jnp.concatenate([jnp.broadcast_to(jnp.exp(g1), (1, DN_DV)) for g1 in gl_heads], axis=1)
    s_ref[...] = s_cat * gt + _dot(kdt, vblk)
    yield
    return jnp.concatenate([o[i * c:(i + 1) * c] for i in grp], axis=1)


def _lockstep(gens):
    outs = [None] * len(gens)
    live = list(range(len(gens)))
    while live:
        for i in list(live):
            try:
                next(gens[i])
            except StopIteration as done:
                outs[i] = done.value
                live.remove(i)
    return outs


def _dn_kernel(xf_ref, gf_ref, xb_ref, gb_ref, trif_ref, trib_ref, mcf_ref, msf_ref, mcb_ref, msb_ref, eye_ref,
               of_ref, ob_ref, s_ref):
    @pl.when(pl.program_id(1) == 0)
    def _():
        s_ref[...] = jnp.zeros_like(s_ref)

    eye = eye_ref[...]
    gw = DN_GROUP * DN_DV
    gens, dsts = [], []
    for bb in range(xf_ref.shape[0]):
        for g in range(DN_HEADS // DN_GROUP):
            gens.append(_dn_direction(0, g * DN_GROUP, xf_ref[bb], gf_ref[bb], trif_ref[...], mcf_ref[...],
                                      msf_ref[...], eye, s_ref.at[bb, 0, :, g * gw:(g + 1) * gw]))
            dsts.append((of_ref, bb, g))
            gens.append(_dn_direction(1, g * DN_GROUP, xb_ref[bb], gb_ref[bb], trib_ref[...], mcb_ref[...],
                                      msb_ref[...], eye, s_ref.at[bb, 1, :, g * gw:(g + 1) * gw]))
            dsts.append((ob_ref, bb, g))
    for (o_ref, bb, g), o in zip(dsts, _lockstep(gens)):
        o_ref[bb, :, g * gw:(g + 1) * gw] = o.astype(BF16)


def _deltanet(qkvn, gb, n_ctx_chunks):
    bsz, t, _ = qkvn.shape
    c = DN_CHUNK
    nc = t // c
    hc = DN_GROUP * c
    bmap = lambda i: jnp.where(i < n_ctx_chunks, n_ctx_chunks - 1 - i, nc + n_ctx_chunks - 1 - i)
    ii = jnp.arange(c)
    tri_f = (ii[None, :] <= ii[:, None]).astype(F32)
    tri_b = (ii[None, :] >= ii[:, None]).astype(F32)
    r = jnp.arange(hc)
    same = (r[:, None] // c) == (r[None, :] // c)
    pi, pj = r[:, None] % c, r[None, :] % c
    mc_f = (same & (pj <= pi)).astype(F32)
    ms_f = (same & (pj < pi)).astype(F32)
    mc_b = (same & (pj >= pi)).astype(F32)
    ms_b = (same & (pj > pi)).astype(F32)
    eye = jnp.eye(hc, dtype=F32)
    full = lambda a: pl.BlockSpec(a.shape, lambda b, i: (0,) * a.ndim)
    nb = DN_BATCH if bsz % DN_BATCH == 0 else 1
    fw = lambda w: pl.BlockSpec((nb, c, w), lambda b, i: (b, i, 0))
    bw = lambda w: pl.BlockSpec((nb, c, w), lambda b, i: (b, bmap(i), 0))
    ow = DN_HEADS * DN_DV
    sds = jax.ShapeDtypeStruct
    return pl.pallas_call(
        _dn_kernel,
        grid=(bsz // nb, nc),
        in_specs=[fw(QKV_W), fw(AB_W), bw(QKV_W), bw(AB_W), full(tri_f), full(tri_b),
                  full(mc_f), full(ms_f), full(mc_b), full(ms_b), full(eye)],
        out_specs=[fw(ow), bw(ow)],
        out_shape=[sds((bsz, t, ow), BF16), sds((bsz, t, ow), BF16)],
        scratch_shapes=[pltpu.VMEM((nb, 2, DN_DK, DN_HEADS * DN_DV), F32)],
        compiler_params=_cparams(("arbitrary", "arbitrary")),
    )(qkvn, gb, qkvn, gb, tri_f, tri_b, mc_f, ms_f, mc_b, ms_b, eye)


MLA_GROUP = 2


def _softmax_av(q, k, v):
    s = _dot_nt(q, k)
    yield
    m = jnp.max(s, axis=-1, keepdims=True)
    p = jnp.exp2(s - m)
    l = jnp.sum(p, axis=-1, keepdims=True)
    p = p.astype(BF16)
    yield
    return _dot(p, v) / l


def _mla_kernel(q_ref, k_ref, v_ref, o_ref):
    j = pl.program_id(2)
    tc = q_ref.shape[1]

    def attend(n_keys):
        outs = _lockstep([_softmax_av(q_ref[g], k_ref[g, 0:n_keys, :], v_ref[g, 0:n_keys, :])
                          for g in range(MLA_GROUP)])
        for g, o in enumerate(outs):
            o_ref[:, g * MLA_V:(g + 1) * MLA_V] = o.astype(BF16)

    @pl.when(j == 0)
    def _():
        attend(tc)

    @pl.when(j > 0)
    def _():
        attend(k_ref.shape[1])


def _mla(q, k, v):
    bsz, nh, t, _ = q.shape
    tq = ROW_TILE
    g = MLA_GROUP
    return pl.pallas_call(
        _mla_kernel,
        grid=(bsz, nh // g, t // tq),
        in_specs=[pl.BlockSpec((None, g, tq, QK_W), lambda b, hp, j: (b, hp, j, 0)),
                  pl.BlockSpec((None, g, t, QK_W), lambda b, hp, j: (b, hp, 0, 0)),
                  pl.BlockSpec((None, g, t, MLA_V), lambda b, hp, j: (b, hp, 0, 0))],
        out_specs=pl.BlockSpec((None, tq, g * MLA_V), lambda b, hp, j: (b, j, hp)),
        out_shape=jax.ShapeDtypeStruct((bsz, t, nh * MLA_V), BF16),
        compiler_params=_cparams(("parallel", "parallel", "arbitrary")),
    )(q, k, v)


def _pack_halves(x):
    w = x.shape[1] // 2
    lo = lax.bitcast_convert_type(x[:, :w].astype(BF16).astype(F32), jnp.uint32)
    hi = lax.bitcast_convert_type(x[:, w:].astype(BF16).astype(F32), jnp.uint32)
    return (hi & jnp.uint32(0xFFFF0000)) | (lo >> 16)


def _unpack_halves(p):
    lo = lax.bitcast_convert_type(p << 16, F32)
    hi = lax.bitcast_convert_type(p & jnp.uint32(0xFFFF0000), F32)
    return lo, hi


def _merge_kernel(h_ref, mod_ref, g1_ref, of_ref, ob_ref, z_ref, dng_ref, ysc_ref, ymla_ref,
                  wg_ref, bg_ref, wdn_ref, wsc_ref, wmla_ref, wout_ref, g2_ref, rw_ref, rwh_ref, rb_ref, tril_ref, triu_ref,
                  hn_ref, xm2_ref, tg_ref, lrank_ref, info_ref, cnt_ref, cnt_scr):
    @pl.when(jnp.logical_and(pl.program_id(0) == 0, pl.program_id(1) == 0))
    def _():
        cnt_scr[...] = jnp.zeros_like(cnt_scr)

    d = h_ref.shape[-1]
    x = h_ref[...]
    xm = _rms(x, g1_ref[...]) * (1.0 + mod_ref[1:2, :]) + mod_ref[0:1, :]
    gates = _sigmoid(_dot(xm.astype(BF16), wg_ref[...]) + bg_ref[...])
    o = of_ref[...].astype(F32) + ob_ref[...].astype(F32)
    z = z_ref[...].astype(F32)
    parts = []
    for hh in range(DN_HEADS):
        oh = o[:, hh * DN_DV:(hh + 1) * DN_DV]
        zh = z[:, hh * DN_DV:(hh + 1) * DN_DV]
        parts.append(_rms(oh, dng_ref[...]) * (zh * _sigmoid(zh)))
    ydn = jnp.concatenate(parts, axis=1).astype(BF16)
    merged = (gates[:, 0:d] * _dot(ydn, wdn_ref[...]) + gates[:, d:2 * d] * _dot(ysc_ref[...], wsc_ref[...])
              + gates[:, 2 * d:3 * d] * _dot(ymla_ref[...], wmla_ref[...]))
    hn = x + mod_ref[2:3, :] * _dot(merged.astype(BF16), wout_ref[...])
    hn_ref[...] = hn
    xm2 = _rms(hn, g2_ref[...]) * (1.0 + mod_ref[4:5, :]) + mod_ref[3:4, :]
    xm2_ref[...] = xm2.astype(BF16)

    xh = xm2.astype(BF16)
    xl = (xm2 - xh.astype(F32)).astype(BF16)
    a = _dot(xh, rw_ref[...])
    logits = a[:, 0:LANE] + a[:, LANE:2 * LANE] + _dot(xl, rwh_ref[...]) + rb_ref[...]
    lane = lax.broadcasted_iota(jnp.int32, logits.shape, 1)
    work = jnp.where(lane < N_EXPERTS, logits, -jnp.inf)
    tv = jnp.zeros(logits.shape, F32)
    vals, idxs = [], []
    for kk in range(TOP_K):
        m = jnp.max(work, axis=-1, keepdims=True)
        idx = jnp.min(jnp.where(work == m, lane, LANE), axis=-1, keepdims=True)
        work = jnp.where(lane == idx, -jnp.inf, work)
        vals.append(m)
        idxs.append(idx)
    es = [jnp.exp(vv - vals[0]) for vv in vals]
    tot = es[0] + es[1] + es[2] + es[3]
    for kk in range(TOP_K):
        tv = jnp.where(lane == kk, es[kk] / tot, tv)
    tg_ref[...] = tv

    sel = jnp.zeros(logits.shape, F32)
    for idx in idxs:
        sel = sel + (lane == idx).astype(F32)
    tcnt = jnp.floor((jnp.sum(sel, axis=0, keepdims=True) + (RUN_ALIGN - 1.0)) * (1.0 / RUN_ALIGN)) * RUN_ALIGN
    tstart = _dot(jnp.broadcast_to(tcnt, (8, LANE)).astype(BF16), triu_ref[...])[0:1]
    lpos = _dot(tril_ref[...], sel.astype(BF16)) + tstart
    lrank = jnp.zeros(logits.shape, F32)
    for kk in range(TOP_K):
        lp = jnp.sum(jnp.where(lane == idxs[kk], lpos, 0.0), axis=-1, keepdims=True)
        lrank = jnp.where(lane == kk, lp, lrank)
    lrank_ref[...] = lrank.astype(jnp.int32)
    sub = lax.broadcasted_iota(jnp.int32, (8, LANE), 0)
    info = jnp.where(sub == 0, tcnt, jnp.where(sub == 1, tstart, jnp.where(sub == 2, cnt_scr[...], 0.0)))
    info_ref[...] = info.astype(jnp.int32)
    total = cnt_scr[...] + tcnt
    cnt_scr[...] = total
    cnt_ref[...] = total


def _merge(h, mod, g1, o_f, o_b, z, dng, ysc, ymla, wg, bg, wdn, wsc, wmla, wout, g2, rw_cat, rw_hi, rb):
    bsz, t, d = h.shape
    tm = ROW_TILE
    row = lambda w: pl.BlockSpec((None, tm, w), lambda b, j: (b, j, 0))
    full = lambda a: pl.BlockSpec(a.shape, lambda b, j: (0,) * a.ndim)
    ow = DN_HEADS * DN_DV
    sds = jax.ShapeDtypeStruct
    ii = jnp.arange(tm)
    tril = (ii[None, :] < ii[:, None]).astype(BF16)
    ee = jnp.arange(LANE)
    triu = (ee[:, None] < ee[None, :]).astype(BF16)
    tile_i32 = pl.BlockSpec((None, None, 8, LANE), lambda b, j: (b, j, 0, 0))
    return pl.pallas_call(
        _merge_kernel,
        grid=(bsz, t // tm),
        in_specs=[row(d), pl.BlockSpec((None, None, 6, d), lambda b, j: (b, jnp.minimum(j, 1), 0, 0)), full(g1),
                  row(ow), row(ow), row(Z_W), full(dng), row(SC_WIDTH), row(MLA_HEADS * MLA_V),
                  full(wg), full(bg), full(wdn), full(wsc), full(wmla), full(wout), full(g2),
                  full(rw_cat), full(rw_hi), full(rb), full(tril), full(triu)],
        out_specs=[row(d), row(d), row(LANE), row(LANE), tile_i32,
                   pl.BlockSpec((1, LANE), lambda b, j: (0, 0))],
        out_shape=[sds((bsz, t, d), F32), sds((bsz, t, d), BF16), sds((bsz, t, LANE), F32),
                   sds((bsz, t, LANE), jnp.int32), sds((bsz, t // tm, 8, LANE), jnp.int32), sds((1, LANE), F32)],
        scratch_shapes=[pltpu.VMEM((1, LANE), F32)],
        compiler_params=_cparams(("arbitrary", "arbitrary")),
    )(h, mod, g1, o_f, o_b, z, dng, ysc, ymla, wg, bg, wdn, wsc, wmla, wout, g2, rw_cat, rw_hi, rb, tril, triu)


GLU_GROUP = 2 * LANE


def _deinterleave_kernel(w_ref, p_ref, o_ref):
    perm = p_ref[...]
    for g in range(w_ref.shape[1] // GLU_GROUP):
        sl = slice(g * GLU_GROUP, (g + 1) * GLU_GROUP)
        o_ref[:, sl] = _dot(w_ref[:, sl].astype(BF16), perm).astype(BF16)


def _deinterleave_w1(expert_w1, l):
    _, n_e, d, f2 = expert_w1.shape
    i = jnp.arange(GLU_GROUP)
    perm = (jnp.where(i % 2 == 0, i // 2, LANE + i // 2)[:, None] == i[None, :]).astype(BF16)
    return pl.pallas_call(
        _deinterleave_kernel,
        grid=(n_e,),
        in_specs=[pl.BlockSpec((None, None, d, f2), lambda e: (l, e, 0, 0)),
                  pl.BlockSpec(perm.shape, lambda e: (0, 0))],
        out_specs=pl.BlockSpec((None, d, f2), lambda e: (e, 0, 0)),
        out_shape=jax.ShapeDtypeStruct((n_e, d, f2), BF16),
        compiler_params=_cparams(("arbitrary",)),
    )(expert_w1, perm)


RUN_PIECES = tuple(RUN_ALIGN << i for i in range((LOCAL_ROWS // RUN_ALIGN).bit_length()))


def _run_copies(run_ref, make_copy):
    def per_run(e, carry):
        cnt, src, dst = run_ref[0, e], run_ref[1, e], run_ref[2, e]
        off = jnp.int32(0)
        for piece in RUN_PIECES:
            has = (cnt & piece) != 0

            @pl.when(has)
            def _(off=off, piece=piece):
                make_copy(pl.multiple_of(src + off, RUN_ALIGN), pl.multiple_of(dst + off, RUN_ALIGN), piece).start()
            off = off + jnp.where(has, piece, 0)
        return carry
    lax.fori_loop(0, N_RUNS, per_run, 0)


def _dispatch_kernel(run_ref, x_ref, lrank_ref, xs_in_hbm, xs_hbm, loc, sem):
    del xs_in_hbm
    tm = x_ref.shape[0]
    n_loc = LOCAL_ROWS
    lr = lrank_ref[...].astype(F32).T
    rows = lax.broadcasted_iota(jnp.int32, (n_loc, tm), 0).astype(F32)
    onehot = jnp.zeros((n_loc, tm), F32)
    for kk in range(TOP_K):
        onehot = onehot + jnp.where(rows == lr[kk:kk + 1, :], 1.0, 0.0)
    loc[...] = _pack_halves(_dot(onehot.astype(BF16), x_ref[...]))
    _run_copies(run_ref, lambda s, d, n: pltpu.make_async_copy(loc.at[pl.ds(s, n)], xs_hbm.at[pl.ds(d, n)], sem))
    pltpu.make_async_copy(loc, xs_hbm.at[pl.ds(0, n_loc)], sem).wait()


def _dispatch(runs, x2, lrank, n_rows):
    bsz, t, d = x2.shape
    tm = ROW_TILE
    nj = t // tm
    w = d // 2
    xs0 = jnp.zeros((n_rows, w), jnp.uint32)
    return pl.pallas_call(
        _dispatch_kernel,
        grid=(bsz, nj),
        in_specs=[pl.BlockSpec((None, None, 3, N_RUNS), lambda b, j: (b, j, 0, 0), memory_space=pltpu.SMEM),
                  pl.BlockSpec((None, tm, d), lambda b, j: (b, j, 0)),
                  pl.BlockSpec((None, tm, LANE), lambda b, j: (b, j, 0)),
                  pl.BlockSpec(memory_space=pl.ANY)],
        out_specs=pl.BlockSpec(memory_space=pl.ANY),
        out_shape=jax.ShapeDtypeStruct((n_rows, w), jnp.uint32),
        scratch_shapes=[pltpu.VMEM((LOCAL_ROWS, w), jnp.uint32), pltpu.SemaphoreType.DMA(())],
        input_output_aliases={3: 0},
        compiler_params=_cparams(("arbitrary", "arbitrary")),
    )(runs, x2, lrank, xs0)


def _expert_kernel(be_ref, nreal_ref, xs_ref, w1_ref, b1_ref, w2_ref, b2_ref, y_ref):
    i = pl.program_id(0)
    half = w1_ref.shape[0] // 2

    @pl.when(i < nreal_ref[0])
    def _():
        lo, hi = _unpack_halves(xs_ref[...])
        hdn = (_dot(lo.astype(BF16), w1_ref[0:half, :]) + _dot(hi.astype(BF16), w1_ref[half:, :])) + b1_ref[...]
        acts = []
        for g in range(hdn.shape[1] // GLU_GROUP):
            glu = jnp.minimum(hdn[:, g * GLU_GROUP:g * GLU_GROUP + LANE], SWIGLU_LIMIT)
            lin = jnp.clip(hdn[:, g * GLU_GROUP + LANE:(g + 1) * GLU_GROUP], -SWIGLU_LIMIT, SWIGLU_LIMIT)
            acts.append((glu * _sigmoid(SWIGLU_ALPHA * glu) * (lin + 1.0)).astype(BF16))
        y_ref[...] = _pack_halves(_dot(jnp.concatenate(acts, axis=1), w2_ref[...]) + b2_ref[...])

    @pl.when(i >= nreal_ref[0])
    def _():
        y_ref[...] = jnp.zeros_like(y_ref)


def _experts(block_expert, n_real, xs, w1, b1, w2, b2):
    n_blocks = block_expert.shape[0]
    f, d = w2.shape[1:]
    eb = EXPERT_BLOCK
    wspec = lambda k, n: pl.BlockSpec((None, k, n), lambda i, be, nr: (be[i], 0, 0))
    gs = pltpu.PrefetchScalarGridSpec(
        num_scalar_prefetch=2,
        grid=(n_blocks,),
        in_specs=[pl.BlockSpec((eb, d // 2), lambda i, be, nr: (jnp.minimum(i, nr[0] - 1), 0)),
                  wspec(d, 2 * f), wspec(1, 2 * f), wspec(f, d), wspec(1, d)],
        out_specs=pl.BlockSpec((eb, d // 2), lambda i, be, nr: (i, 0)),
    )
    return pl.pallas_call(
        _expert_kernel,
        grid_spec=gs,
        out_shape=jax.ShapeDtypeStruct((n_blocks * eb, d // 2), jnp.uint32),
        compiler_params=_cparams(("arbitrary",)),
    )(block_expert, n_real, xs, w1, b1, w2, b2)


def _combine_kernel(final, run_ref, lrank_ref, y_hbm, h_ref, mod_ref, tg_ref, fg_ref, o_ref, loc, sem):
    tm = h_ref.shape[0]
    n_loc = LOCAL_ROWS
    _run_copies(run_ref, lambda s, d, n: pltpu.make_async_copy(y_hbm.at[pl.ds(d, n)], loc.at[pl.ds(s, n)], sem))
    lr = lrank_ref[...]
    tg = tg_ref[...]
    cols = lax.broadcasted_iota(jnp.int32, (tm, n_loc), 1)
    wsel = jnp.zeros((tm, n_loc), F32)
    for kk in range(TOP_K):
        wsel = wsel + jnp.where(cols == lr[:, kk:kk + 1], tg[:, kk:kk + 1], 0.0)
    wsel = wsel.astype(BF16)
    pltpu.make_async_copy(y_hbm.at[pl.ds(0, n_loc)], loc, sem).wait()
    lo, hi = _unpack_halves(loc[...])
    y = jnp.concatenate([_dot(wsel, lo.astype(BF16)), _dot(wsel, hi.astype(BF16))], axis=1)
    hn = h_ref[...] + mod_ref[5:6, :] * y
    if final:
        hn = _rms(hn, fg_ref[...])
    o_ref[...] = hn


def _combine(runs, lrank, y_rows, h, mod, tg, fg, final):
    bsz, t, d = h.shape
    tm = ROW_TILE
    nj = t // tm
    row = lambda w: pl.BlockSpec((None, tm, w), lambda b, j: (b, j, 0))
    return pl.pallas_call(
        functools.partial(_combine_kernel, final),
        grid=(bsz, nj),
        in_specs=[pl.BlockSpec((None, None, 3, N_RUNS), lambda b, j: (b, j, 0, 0), memory_space=pltpu.SMEM),
                  row(LANE), pl.BlockSpec(memory_space=pl.ANY),
                  row(d), pl.BlockSpec((None, None, 6, d), lambda b, j: (b, jnp.minimum(j, 1), 0, 0)),
                  row(LANE), pl.BlockSpec(fg.shape, lambda b, j: (0, 0))],
        out_specs=row(d),
        out_shape=jax.ShapeDtypeStruct((bsz, t, d), F32),
        scratch_shapes=[pltpu.VMEM((LOCAL_ROWS, d // 2), jnp.uint32), pltpu.SemaphoreType.DMA(())],
        compiler_params=_cparams(("arbitrary", "arbitrary")),
    )(runs, lrank, y_rows, h, mod, tg, fg)


def _route(info, cnt, n_assign):
    eb = EXPERT_BLOCK
    n_tiles = info.shape[0] * info.shape[1]
    max_rows = n_assign + n_tiles * N_EXPERTS * (RUN_ALIGN - 1)
    n_blocks = -(-max_rows // eb) + N_EXPERTS + -(-LOCAL_ROWS // eb)
    filler_row = n_blocks * eb - LOCAL_ROWS
    counts = cnt[0, :N_EXPERTS].astype(jnp.int32)
    padded = (counts + eb - 1) // eb * eb
    pad_end = jnp.cumsum(padded)
    pad_start = pad_end - padded
    tcnt, tstart, tcarry = (info[:, :, r, :N_EXPERTS] for r in range(3))
    used = jnp.sum(tcnt, axis=-1, keepdims=True)
    runs = jnp.stack([jnp.concatenate([tcnt, LOCAL_ROWS - used], axis=-1),
                      jnp.concatenate([tstart, used], axis=-1),
                      jnp.concatenate([pad_start + tcarry, jnp.full_like(used, filler_row)], axis=-1)],
                     axis=2).astype(jnp.int32)
    block_row = jnp.arange(n_blocks, dtype=jnp.int32) * eb
    block_expert = jnp.minimum(jnp.sum((pad_end[None, :] <= block_row[:, None]).astype(jnp.int32), axis=1),
                               N_EXPERTS - 1)
    n_real = (pad_end[-1] // eb).astype(jnp.int32).reshape(1)
    return runs, block_expert, n_real, n_blocks


def _rope_tables(t_lat, t_ctx):
    rows = t_lat // GRID_W
    row = jnp.repeat(jnp.arange(rows, dtype=F32), GRID_W)
    col = jnp.tile(jnp.arange(GRID_W, dtype=F32), rows)
    axis_dims = MLA_ROPE // 2
    inv = ROPE_BASE ** (-jnp.arange(0, axis_dims, 2, dtype=F32) / axis_dims)
    ar, ac = row[:, None] * inv, col[:, None] * inv
    cos64 = jnp.concatenate([jnp.cos(ar), jnp.cos(ar), jnp.cos(ac), jnp.cos(ac)], axis=-1)
    sin64 = jnp.concatenate([-jnp.sin(ar), jnp.sin(ar), -jnp.sin(ac), jnp.sin(ac)], axis=-1)
    pad = jnp.zeros((t_lat, LANE - MLA_ROPE), F32)
    cos_l = jnp.concatenate([cos64, pad], axis=-1)
    sin_l = jnp.concatenate([sin64, pad], axis=-1)
    cos_c = jnp.concatenate([jnp.ones((t_ctx, MLA_ROPE), F32), jnp.zeros((t_ctx, LANE - MLA_ROPE), F32)], axis=-1)
    sin_c = jnp.zeros((t_ctx, LANE), F32)
    return jnp.concatenate([cos_c, cos_l], axis=0), jnp.concatenate([sin_c, sin_l], axis=0)


def _rope_swap_perm():
    half = MLA_ROPE // 4
    perm = []
    for a in range(2):
        base = a * 2 * half
        perm += list(range(base + half, base + 2 * half)) + list(range(base, base + half))
    return jnp.array(perm, jnp.int32)


def _pad_cols(w, width):
    return jnp.concatenate([w, jnp.zeros(w.shape[:-1] + (width - w.shape[-1],), w.dtype)], axis=-1)


def _layer_weights(l, w_in, dn_a_log, dn_dt_bias, mla_w_qb, router_w, router_b, expert_w1, expert_b1):
    nqk = DN_HEADS * DN_DK
    wi = w_in[l]
    o = 0
    dn_qkv = wi[:, o:o + QKV_W]; o += QKV_W
    dn_z = wi[:, o:o + Z_W]; o += Z_W
    dn_ab = wi[:, o:o + 4 * DN_HEADS]; o += 4 * DN_HEADS
    sc3 = wi[:, o:o + SC3_W]; o += SC3_W
    qa = wi[:, o:o + MLA_Q_LORA]; o += MLA_Q_LORA
    ckv = wi[:, o:o + MLA_KV_LORA]; o += MLA_KV_LORA
    kr = wi[:, o:o + MLA_ROPE]
    perm = _rope_swap_perm()
    w_cat = jnp.concatenate([dn_qkv, dn_z, sc3, qa, ckv, _pad_cols(kr, LANE), _pad_cols(kr[:, perm], LANE),
                             _pad_cols(dn_ab, LANE)], axis=-1).astype(BF16)
    dq = MLA_NOPE + MLA_ROPE
    wq = mla_w_qb[l].reshape(MLA_Q_LORA, MLA_HEADS, dq)
    wq_rope = wq[:, :, MLA_NOPE:]
    wq_cat = jnp.concatenate([wq[:, :, :MLA_NOPE], _pad_cols(wq_rope, LANE), _pad_cols(wq_rope[:, :, perm], LANE)],
                             axis=-1).reshape(MLA_Q_LORA, MLA_HEADS * QH_W).astype(BF16)
    alog_row = _pad_cols(jnp.concatenate([dn_a_log[l].reshape(1, -1), jnp.zeros((1, 2 * DN_HEADS), F32)], -1), LANE)
    dtb_row = _pad_cols(jnp.concatenate([dn_dt_bias[l].reshape(1, -1), jnp.zeros((1, 2 * DN_HEADS), F32)], -1), LANE)
    rw = _pad_cols(router_w[l], LANE)
    rw_hi = rw.astype(BF16)
    rw_lo = (rw - rw_hi.astype(F32)).astype(BF16)
    rw_cat = jnp.concatenate([rw_hi, rw_lo], axis=-1)
    rb = jnp.concatenate([router_b[l], jnp.full((LANE - N_EXPERTS,), 0.0, F32)]).reshape(1, LANE)
    w1 = _deinterleave_w1(expert_w1, l)
    n_e, f2 = expert_b1.shape[1:]
    b1 = expert_b1[l].reshape(n_e, f2 // GLU_GROUP, LANE, 2).transpose(0, 1, 3, 2).reshape(n_e, 1, f2)
    return w_cat, wq_cat, alog_row, dtb_row, rw_cat, rw_hi, rb, w1, b1


def kernel(x, c, ctx, c_ctx, ada_w, ada_b, norm1_g, norm2_g, w_in, dn_conv_w, dn_a_log, dn_dt_bias, dn_norm_g, sc_conv_w, mla_q_norm_g, mla_w_qb, mla_kv_norm_g, mla_w_kvb, w_branch_gate, b_branch_gate, w_branch_dn, w_branch_sc, w_branch_mla, w_out, router_w, router_b, expert_w1, expert_b1, expert_w2, expert_b2, final_norm_g):
    bsz, t_lat, d = x.shape
    t_ctx = ctx.shape[1]
    depth = ada_w.shape[0]
    assert t_ctx == ROW_TILE and t_lat % ROW_TILE == 0 and t_lat % GRID_W == 0
    t = t_ctx + t_lat

    cos_t, sin_t = _rope_tables(t_lat, t_ctx)
    n_mod_rows = -(-(bsz + 1) // 8) * 8
    cvec = jnp.concatenate([c, c_ctx[None], jnp.zeros((n_mod_rows - bsz - 1, d), F32)], axis=0)
    mods = _ada_mods(cvec, ada_w, ada_b)

    h = jnp.concatenate([ctx, x], axis=1)
    row2 = lambda v: v.reshape(1, -1)
    for l in range(depth):
        mod_lat = mods[l, :bsz].reshape(bsz, 1, 6, d)
        mod_ctx = jnp.broadcast_to(mods[l, bsz].reshape(1, 1, 6, d), (bsz, 1, 6, d))
        mod = jnp.concatenate([mod_ctx, mod_lat], axis=1)
        (w_cat, wq_cat, alog_row, dtb_row, rw_cat, rw_hi, rb, w1, b1) = _layer_weights(
            l, w_in, dn_a_log, dn_dt_bias, mla_w_qb, router_w, router_b, expert_w1, expert_b1)

        qkv, z, sc, ab, q, k, v = _inproj(h, mod, row2(norm1_g[l]), w_cat, row2(mla_q_norm_g[l]), wq_cat,
                                          row2(mla_kv_norm_g[l]), mla_w_kvb[l].astype(BF16), cos_t, sin_t)
        qkvn, ysc, gb = _prep(qkv, sc, ab, dn_conv_w[l], sc_conv_w[l], alog_row, dtb_row)
        o_f, o_b = _deltanet(qkvn, gb, t_ctx // DN_CHUNK)
        ymla = _mla(q, k, v)
        hn, xm2, tg, lrank, info, cnt = _merge(h, mod, row2(norm1_g[l]), o_f, o_b, z, row2(dn_norm_g[l]), ysc, ymla,
                                 w_branch_gate[l].astype(BF16), row2(b_branch_gate[l]),
                                 w_branch_dn[l].astype(BF16), w_branch_sc[l].astype(BF16),
                                 w_branch_mla[l].astype(BF16), w_out[l].astype(BF16), row2(norm2_g[l]),
                                 rw_cat, rw_hi, rb)
        runs, block_expert, n_real, n_blocks = _route(info, cnt, bsz * t * TOP_K)
        xs = _dispatch(runs, xm2, lrank, n_blocks * EXPERT_BLOCK)
        y_rows = _experts(block_expert, n_real, xs, w1, b1, expert_w2[l].astype(BF16), expert_b2[l][:, None, :])
        h = _combine(runs, lrank, y_rows, hn, mod, tg, row2(final_norm_g), final=(l == depth - 1))
    return h[:, t_ctx:, :]
```

```python
import functools
import math

import jax
import jax.numpy as jnp
from jax import lax
from jax.experimental import pallas as pl
from jax.experimental.pallas import tpu as pltpu

F32 = jnp.float32
BF16 = jnp.bfloat16

GRID_W = 64
NORM_EPS = 1e-6
DN_HEADS = 4
DN_DK = 128
DN_DV = 128
DN_CHUNK = 64
SC_WIDTH = 512
MLA_HEADS = 4
MLA_Q_LORA = 256
MLA_KV_LORA = 128
MLA_NOPE = 128
MLA_ROPE = 64
MLA_V = 128
MLA_SCALE = (MLA_NOPE + MLA_ROPE) ** -0.5
Q_SCALE = MLA_SCALE * math.log2(math.e)
ROPE_BASE = 10000.0
N_EXPERTS = 32
TOP_K = 4
EXPERT_FF = 1024
SWIGLU_ALPHA = 1.702
SWIGLU_LIMIT = 7.0
EXPERT_BLOCK = 512

LANE = 128
ROW_TILE = 256
HALO = 16
DN_BATCH = 4
RUN_ALIGN = 8
LOCAL_ROWS = TOP_K * ROW_TILE + N_EXPERTS * RUN_ALIGN
N_RUNS = N_EXPERTS + 1
VMEM_LIMIT = 56 * 1024 * 1024

QKV_W = 3 * DN_HEADS * DN_DK
Z_W = DN_HEADS * DN_DV
SC3_W = 3 * SC_WIDTH
KVA_W = 3 * LANE
AB_W = LANE
IN_W = QKV_W + Z_W + SC3_W + MLA_Q_LORA + KVA_W + AB_W
QH_W = 3 * LANE
QK_W = 2 * LANE


def _cparams(sem):
    return pltpu.CompilerParams(dimension_semantics=sem, vmem_limit_bytes=VMEM_LIMIT)


def _dot(a, b):
    return jnp.dot(a, b, preferred_element_type=F32)


def _dot_nt(a, b):
    return lax.dot_general(a, b, (((1,), (1,)), ((), ())), preferred_element_type=F32)


def _rms(x, g):
    return x * lax.rsqrt(jnp.mean(x * x, axis=-1, keepdims=True) + NORM_EPS) * g


def _sigmoid(x):
    return 1.0 / (1.0 + jnp.exp(-x))


def _ada_kernel(c_ref, w_ref, b_ref, o_ref):
    cv = c_ref[...]
    s = cv * _sigmoid(cv)
    o_ref[...] = _dot(s.astype(BF16), w_ref[...].astype(BF16)) + b_ref[...]


def _ada_mods(cvec, ada_w, ada_b):
    n_layers, d, d6 = ada_w.shape
    r = cvec.shape[0]
    tn = 512
    return pl.pallas_call(
        _ada_kernel,
        grid=(n_layers, d6 // tn),
        in_specs=[pl.BlockSpec((r, d), lambda l, n: (0, 0)),
                  pl.BlockSpec((None, d, tn), lambda l, n: (l, 0, n)),
                  pl.BlockSpec((None, 1, tn), lambda l, n: (l, 0, n))],
        out_specs=pl.BlockSpec((None, r, tn), lambda l, n: (l, 0, n)),
        out_shape=jax.ShapeDtypeStruct((n_layers, r, d6), F32),
        compiler_params=_cparams(("arbitrary", "arbitrary")),
    )(cvec, ada_w, ada_b.reshape(n_layers, 1, d6))


def _inproj_kernel(h_ref, mod_ref, g1_ref, w_ref, gq_ref, wq_ref, gkv_ref, wkv_ref, cos_ref, sin_ref,
                   qkv_ref, z_ref, sc_ref, ab_ref, q_ref, k_ref, v_ref):
    x = h_ref[...]
    xm = _rms(x, g1_ref[...]) * (1.0 + mod_ref[1:2, :]) + mod_ref[0:1, :]
    p = _dot(xm.astype(BF16), w_ref[...])
    o = 0
    qkv_ref[...] = p[:, o:o + QKV_W].astype(BF16)
    o += QKV_W
    z_ref[...] = p[:, o:o + Z_W].astype(BF16)
    o += Z_W
    sc_ref[...] = p[:, o:o + SC3_W].astype(BF16)
    o += SC3_W
    qa = p[:, o:o + MLA_Q_LORA]
    o += MLA_Q_LORA
    ckv = p[:, o:o + LANE]
    kr = p[:, o + LANE:o + 2 * LANE]
    krs = p[:, o + 2 * LANE:o + 3 * LANE]
    o += KVA_W
    ab_ref[...] = p[:, o:o + AB_W]

    cos = cos_ref[...]
    sin = sin_ref[...]
    k_rope = (kr * cos + krs * sin).astype(BF16)
    qn = _rms(qa, gq_ref[...]).astype(BF16)
    qf = _dot(qn, wq_ref[...])
    kvn = _rms(ckv, gkv_ref[...]).astype(BF16)
    kv = _dot(kvn, wkv_ref[...])
    for hh in range(MLA_HEADS):
        b0 = hh * QH_W
        q_ref[hh, :, 0:LANE] = (qf[:, b0:b0 + LANE] * Q_SCALE).astype(BF16)
        q_rope = qf[:, b0 + LANE:b0 + 2 * LANE] * cos + qf[:, b0 + 2 * LANE:b0 + 3 * LANE] * sin
        q_ref[hh, :, LANE:2 * LANE] = (q_rope * Q_SCALE).astype(BF16)
        c0 = hh * (MLA_NOPE + MLA_V)
        k_ref[hh, :, 0:LANE] = kv[:, c0:c0 + MLA_NOPE].astype(BF16)
        k_ref[hh, :, LANE:2 * LANE] = k_rope
        v_ref[hh] = kv[:, c0 + MLA_NOPE:c0 + MLA_NOPE + MLA_V].astype(BF16)


def _inproj(h, mod, g1, w_cat, gq, wq_cat, gkv, wkv, cos_t, sin_t):
    bsz, t, d = h.shape
    tm = ROW_TILE
    nj = t // tm
    row = lambda w: pl.BlockSpec((None, tm, w), lambda b, j: (b, j, 0))
    full = lambda a: pl.BlockSpec(a.shape, lambda b, j: (0,) * a.ndim)
    head = lambda w: pl.BlockSpec((None, MLA_HEADS, tm, w), lambda b, j: (b, 0, j, 0))
    sds = jax.ShapeDtypeStruct
    return pl.pallas_call(
        _inproj_kernel,
        grid=(bsz, nj),
        in_specs=[row(d),
                  pl.BlockSpec((None, None, 6, d), lambda b, j: (b, jnp.minimum(j, 1), 0, 0)),
                  full(g1), full(w_cat), full(gq), full(wq_cat), full(gkv), full(wkv),
                  pl.BlockSpec((tm, LANE), lambda b, j: (j, 0)),
                  pl.BlockSpec((tm, LANE), lambda b, j: (j, 0))],
        out_specs=[row(QKV_W), row(Z_W), row(SC3_W), row(AB_W), head(QK_W), head(QK_W), head(MLA_V)],
        out_shape=[sds((bsz, t, QKV_W), BF16), sds((bsz, t, Z_W), BF16), sds((bsz, t, SC3_W), BF16),
                   sds((bsz, t, AB_W), F32),
                   sds((bsz, MLA_HEADS, t, QK_W), BF16), sds((bsz, MLA_HEADS, t, QK_W), BF16),
                   sds((bsz, MLA_HEADS, t, MLA_V), BF16)],
        compiler_params=_cparams(("parallel", "arbitrary")),
    )(h, mod, g1, w_cat, gq, wq_cat, gkv, wkv, cos_t, sin_t)


def _shift_rows(x, prev_row, next_row):
    tm = x.shape[0]
    rid = lax.broadcasted_iota(jnp.int32, x.shape, 0)
    xp = jnp.where(rid == 0, prev_row, pltpu.roll(x, 1, 0))
    xn = jnp.where(rid == tm - 1, next_row, pltpu.roll(x, tm - 1, 0))
    return xp, xn


def _prep_kernel(qkv_ref, qkv_p_ref, qkv_n_ref, sc_ref, sc_p_ref, sc_n_ref, ab_ref,
                 dnw_ref, scw_ref, alog_ref, dtb_ref, qkvn_ref, ysc_ref, gb_ref):
    j = pl.program_id(1)
    nj = pl.num_programs(1)
    pv = (j >= 2).astype(F32)
    nv = jnp.logical_and(j >= 1, j <= nj - 2).astype(F32)

    x = qkv_ref[...].astype(F32)
    xp, xn = _shift_rows(x, qkv_p_ref[HALO - 1:HALO, :].astype(F32) * pv, qkv_n_ref[0:1, :].astype(F32) * nv)
    y = xp * dnw_ref[0:1, :] + x * dnw_ref[1:2, :] + xn * dnw_ref[2:3, :]
    y = y * _sigmoid(y)
    nqk = DN_HEADS * DN_DK
    for g in range(2 * DN_HEADS):
        yy = y[:, g * DN_DK:(g + 1) * DN_DK]
        yy = yy * lax.rsqrt(jnp.sum(yy * yy, axis=-1, keepdims=True) + 1e-6)
        if g < DN_HEADS:
            yy = yy * (DN_DK ** -0.5)
        qkvn_ref[:, g * DN_DK:(g + 1) * DN_DK] = yy.astype(BF16)
    qkvn_ref[:, 2 * nqk:] = y[:, 2 * nqk:].astype(BF16)

    s = sc_ref[...].astype(F32)
    sp = sc_p_ref[HALO - 1:HALO, :].astype(F32) * pv
    sn = sc_n_ref[0:1, :].astype(F32) * nv
    w = SC_WIDTH
    u = s[:, 2 * w:3 * w] * s[:, 0:w]
    up, un = _shift_rows(u, sp[:, 2 * w:3 * w] * sp[:, 0:w], sn[:, 2 * w:3 * w] * sn[:, 0:w])
    conv = up * scw_ref[0:1, :] + u * scw_ref[1:2, :] + un * scw_ref[2:3, :]
    ysc_ref[...] = (s[:, w:2 * w] * conv).astype(BF16)

    ab = ab_ref[...]
    sp_arg = ab + dtb_ref[...]
    softplus = jnp.maximum(sp_arg, 0.0) + jnp.log1p(jnp.exp(-jnp.abs(sp_arg)))
    gval = -jnp.exp(alog_ref[...]) * softplus
    lane = lax.broadcasted_iota(jnp.int32, ab.shape, 1)
    gb_ref[...] = jnp.where(lane < 2 * DN_HEADS, gval, _sigmoid(ab))


def _prep(qkv, sc, ab, dn_conv_w, sc_conv_w, alog_row, dtb_row):
    bsz, t, _ = qkv.shape
    tm = ROW_TILE
    nj = t // tm
    hb = tm // HALO
    nh = t // HALO
    row = lambda w: pl.BlockSpec((None, tm, w), lambda b, j: (b, j, 0))
    prev = lambda w: pl.BlockSpec((None, HALO, w), lambda b, j: (b, jnp.maximum(j * hb - 1, 0), 0))
    nxt = lambda w: pl.BlockSpec((None, HALO, w), lambda b, j: (b, jnp.minimum((j + 1) * hb, nh - 1), 0))
    full = lambda a: pl.BlockSpec(a.shape, lambda b, j: (0,) * a.ndim)
    sds = jax.ShapeDtypeStruct
    return pl.pallas_call(
        _prep_kernel,
        grid=(bsz, nj),
        in_specs=[row(QKV_W), prev(QKV_W), nxt(QKV_W), row(SC3_W), prev(SC3_W), nxt(SC3_W), row(AB_W),
                  full(dn_conv_w), full(sc_conv_w), full(alog_row), full(dtb_row)],
        out_specs=[row(QKV_W), row(SC_WIDTH), row(AB_W)],
        out_shape=[sds((bsz, t, QKV_W), BF16), sds((bsz, t, SC_WIDTH), BF16), sds((bsz, t, AB_W), F32)],
        compiler_params=_cparams(("parallel", "arbitrary")),
    )(qkv, qkv, qkv, sc, sc, sc, ab, dn_conv_w, sc_conv_w, alog_row, dtb_row)


DN_GROUP = 2


def _stack_heads(x, h0):
    return jnp.concatenate([x[:, (h0 + i) * LANE:(h0 + i + 1) * LANE] for i in range(DN_GROUP)], axis=0)


def _stack_cols(x, c0):
    return jnp.concatenate([x[:, c0 + i:c0 + i + 1] for i in range(DN_GROUP)], axis=0)


def _dn_direction(d, h0, x, gbt, tri, m_causal, m_strict, eye, s_ref):
    c = DN_CHUNK
    grp = range(DN_GROUP)
    nqk = DN_HEADS * DN_DK
    qst = _stack_heads(x[:, 0:nqk], h0)
    kst = _stack_heads(x[:, nqk:2 * nqk], h0)
    vst = _stack_heads(x[:, 2 * nqk:], h0)
    gc_all = jnp.dot(tri, gbt, preferred_element_type=F32, precision=lax.Precision.HIGHEST)
    col0 = d * DN_HEADS + h0
    gc = _stack_cols(gc_all, col0)
    beta = _stack_cols(gbt, 2 * DN_HEADS + col0)
    last = c - 1 if d == 0 else 0
    gl_heads = [gc_all[last:last + 1, col0 + i:col0 + i + 1] for i in grp]
    gl = jnp.concatenate([jnp.broadcast_to(g1, (c, 1)) for g1 in gl_heads], axis=0)
    kf = kst.astype(F32)
    kb = kf * beta
    kk = _dot_nt(kb.astype(BF16), kst)
    qk = _dot_nt(qst, kst)
    yield

    hc = DN_GROUP * c
    gmat = jnp.broadcast_to(gc, (hc, hc))
    dec = jnp.exp(jnp.minimum(gmat - gmat.T, 0.0))
    xpow = -(kk * jnp.where(m_strict > 0.5, dec, 0.0))
    tinv = eye + xpow
    intra = (qk * jnp.where(m_causal > 0.5, dec, 0.0)).astype(BF16)
    eg = jnp.exp(gc)
    rhs = jnp.concatenate([vst.astype(F32) * beta, kb * eg], axis=1).astype(BF16)
    qd = qst.astype(F32) * eg
    kdt = (kf * jnp.exp(gl - gc)).T.astype(BF16)
    for _ in range(int(math.log2(c)) - 1):
        xb = xpow.astype(BF16)
        xpow = _dot(xb, xb)
        yield
        tinv = tinv + _dot(tinv.astype(BF16), xpow.astype(BF16))
        yield
    sol = _dot(tinv.astype(BF16), rhs)
    yield
    u = sol[:, 0:DN_DV]
    w = sol[:, DN_DV:]

    s_cat = s_ref[...]
    wq = jnp.concatenate([w, qd], axis=0).astype(BF16)
    full = _dot(wq, s_cat.astype(BF16))
    yield
    ws = jnp.concatenate([full[i * c:(i + 1) * c, i * DN_DV:(i + 1) * DN_DV] for i in grp], axis=0)
    qs = jnp.concatenate([full[hc + i * c:hc + (i + 1) * c, i * DN_DV:(i + 1) * DN_DV] for i in grp], axis=0)
    vnew = u - ws
    o = qs + _dot(intra, vnew.astype(BF16))
    rhead = lax.broadcasted_iota(jnp.int32, (hc, DN_DV), 0) // c
    vblk = jnp.concatenate([jnp.where(rhead == i, vnew, 0.0) for i in grp], axis=1).astype(BF16)
    gt = jnp.concatenate([jnp.broadcast_to(jnp.exp(g1), (1, DN_DV)) for g1 in gl_heads], axis=1)
    s_ref[...] = s_cat * gt + _dot(kdt, vblk)
    yield
    return jnp.concatenate([o[i * c:(i + 1) * c] for i in grp], axis=1)


def _lockstep(gens):
    outs = [None] * len(gens)
    live = list(range(len(gens)))
    while live:
        for i in list(live):
            try:
                next(gens[i])
            except StopIteration as done:
                outs[i] = done.value
                live.remove(i)
    return outs


def _dn_kernel(xf_ref, gf_ref, xb_ref, gb_ref, trif_ref, trib_ref, mcf_ref, msf_ref, mcb_ref, msb_ref, eye_ref,
               of_ref, ob_ref, s_ref):
    @pl.when(pl.program_id(1) == 0)
    def _():
        s_ref[...] = jnp.zeros_like(s_ref)

    eye = eye_ref[...]
    gw = DN_GROUP * DN_DV
    gens, dsts = [], []
    for bb in range(xf_ref.shape[0]):
        for g in range(DN_HEADS // DN_GROUP):
            gens.append(_dn_direction(0, g * DN_GROUP, xf_ref[bb], gf_ref[bb], trif_ref[...], mcf_ref[...],
                                      msf_ref[...], eye, s_ref.at[bb, 0, :, g * gw:(g + 1) * gw]))
            dsts.append((of_ref, bb, g))
            gens.append(_dn_direction(1, g * DN_GROUP, xb_ref[bb], gb_ref[bb], trib_ref[...], mcb_ref[...],
                                      msb_ref[...], eye, s_ref.at[bb, 1, :, g * gw:(g + 1) * gw]))
            dsts.append((ob_ref, bb, g))
    for (o_ref, bb, g), o in zip(dsts, _lockstep(gens)):
        o_ref[bb, :, g * gw:(g + 1) * gw] = o.astype(BF16)


def _deltanet(qkvn, gb, n_ctx_chunks):
    bsz, t, _ = qkvn.shape
    c = DN_CHUNK
    nc = t // c
    hc = DN_GROUP * c
    bmap = lambda i: jnp.where(i < n_ctx_chunks, n_ctx_chunks - 1 - i, nc + n_ctx_chunks - 1 - i)
    ii = jnp.arange(c)
    tri_f = (ii[None, :] <= ii[:, None]).astype(F32)
    tri_b = (ii[None, :] >= ii[:, None]).astype(F32)
    r = jnp.arange(hc)
    same = (r[:, None] // c) == (r[None, :] // c)
    pi, pj = r[:, None] % c, r[None, :] % c
    mc_f = (same & (pj <= pi)).astype(F32)
    ms_f = (same & (pj < pi)).astype(F32)
    mc_b = (same & (pj >= pi)).astype(F32)
    ms_b = (same & (pj > pi)).astype(F32)
    eye = jnp.eye(hc, dtype=F32)
    full = lambda a: pl.BlockSpec(a.shape, lambda b, i: (0,) * a.ndim)
    nb = DN_BATCH if bsz % DN_BATCH == 0 else 1
    fw = lambda w: pl.BlockSpec((nb, c, w), lambda b, i: (b, i, 0))
    bw = lambda w: pl.BlockSpec((nb, c, w), lambda b, i: (b, bmap(i), 0))
    ow = DN_HEADS * DN_DV
    sds = jax.ShapeDtypeStruct
    return pl.pallas_call(
        _dn_kernel,
        grid=(bsz // nb, nc),
        in_specs=[fw(QKV_W), fw(AB_W), bw(QKV_W), bw(AB_W), full(tri_f), full(tri_b),
                  full(mc_f), full(ms_f), full(mc_b), full(ms_b), full(eye)],
        out_specs=[fw(ow), bw(ow)],
        out_shape=[sds((bsz, t, ow), BF16), sds((bsz, t, ow), BF16)],
        scratch_shapes=[pltpu.VMEM((nb, 2, DN_DK, DN_HEADS * DN_DV), F32)],
        compiler_params=_cparams(("arbitrary", "arbitrary")),
    )(qkvn, gb, qkvn, gb, tri_f, tri_b, mc_f, ms_f, mc_b, ms_b, eye)


MLA_GROUP = 2


def _softmax_av(q, k, v):
    s = _dot_nt(q, k)
    yield
    m = jnp.max(s, axis=-1, keepdims=True)
    p = jnp.exp2(s - m)
    l = jnp.sum(p, axis=-1, keepdims=True)
    p = p.astype(BF16)
    yield
    return _dot(p, v) / l


def _mla_kernel(q_ref, k_ref, v_ref, o_ref):
    j = pl.program_id(2)
    tc = q_ref.shape[1]

    def attend(n_keys):
        outs = _lockstep([_softmax_av(q_ref[g], k_ref[g, 0:n_keys, :], v_ref[g, 0:n_keys, :])
                          for g in range(MLA_GROUP)])
        for g, o in enumerate(outs):
            o_ref[:, g * MLA_V:(g + 1) * MLA_V] = o.astype(BF16)

    @pl.when(j == 0)
    def _():
        attend(tc)

    @pl.when(j > 0)
    def _():
        attend(k_ref.shape[1])


def _mla(q, k, v):
    bsz, nh, t, _ = q.shape
    tq = ROW_TILE
    g = MLA_GROUP
    return pl.pallas_call(
        _mla_kernel,
        grid=(bsz, nh // g, t // tq),
        in_specs=[pl.BlockSpec((None, g, tq, QK_W), lambda b, hp, j: (b, hp, j, 0)),
                  pl.BlockSpec((None, g, t, QK_W), lambda b, hp, j: (b, hp, 0, 0)),
                  pl.BlockSpec((None, g, t, MLA_V), lambda b, hp, j: (b, hp, 0, 0))],
        out_specs=pl.BlockSpec((None, tq, g * MLA_V), lambda b, hp, j: (b, j, hp)),
        out_shape=jax.ShapeDtypeStruct((bsz, t, nh * MLA_V), BF16),
        compiler_params=_cparams(("parallel", "parallel", "arbitrary")),
    )(q, k, v)


def _pack_halves(x):
    w = x.shape[1] // 2
    lo = lax.bitcast_convert_type(x[:, :w].astype(BF16).astype(F32), jnp.uint32)
    hi = lax.bitcast_convert_type(x[:, w:].astype(BF16).astype(F32), jnp.uint32)
    return (hi & jnp.uint32(0xFFFF0000)) | (lo >> 16)


def _unpack_halves(p):
    lo = lax.bitcast_convert_type(p << 16, F32)
    hi = lax.bitcast_convert_type(p & jnp.uint32(0xFFFF0000), F32)
    return lo, hi


def _merge_kernel(h_ref, mod_ref, g1_ref, of_ref, ob_ref, z_ref, dng_ref, ysc_ref, ymla_ref,
                  wg_ref, bg_ref, wdn_ref, wsc_ref, wmla_ref, wout_ref, g2_ref, rw_ref, rwh_ref, rb_ref, tril_ref, triu_ref,
                  hn_ref, xm2_ref, tg_ref, lrank_ref, info_ref, cnt_ref, cnt_scr):
    @pl.when(jnp.logical_and(pl.program_id(0) == 0, pl.program_id(1) == 0))
    def _():
        cnt_scr[...] = jnp.zeros_like(cnt_scr)

    d = h_ref.shape[-1]
    x = h_ref[...]
    xm = (_rms(x, g1_ref[...]) * (1.0 + mod_ref[1:2, :]) + mod_ref[0:1, :]).astype(BF16)
    o = of_ref[...].astype(F32) + ob_ref[...].astype(F32)
    z = z_ref[...].astype(F32)
    parts = []
    for hh in range(DN_HEADS):
        oh = o[:, hh * DN_DV:(hh + 1) * DN_DV]
        zh = z[:, hh * DN_DV:(hh + 1) * DN_DV]
        parts.append(_rms(oh, dng_ref[...]) * (zh * _sigmoid(zh)))
    ydn = jnp.concatenate(parts, axis=1).astype(BF16)
    gates = _sigmoid(_dot(xm, wg_ref[...]) + bg_ref[...])
    merged = (gates[:, 0:d] * _dot(ydn, wdn_ref[...]) + gates[:, d:2 * d] * _dot(ysc_ref[...], wsc_ref[...])
              + gates[:, 2 * d:3 * d] * _dot(ymla_ref[...], wmla_ref[...]))
    hn = x + mod_ref[2:3, :] * _dot(merged.astype(BF16), wout_ref[...])
    hn_ref[...] = hn
    xm2 = _rms(hn, g2_ref[...]) * (1.0 + mod_ref[4:5, :]) + mod_ref[3:4, :]
    xm2_ref[...] = xm2.astype(BF16)

    xh = xm2.astype(BF16)
    xl = (xm2 - xh.astype(F32)).astype(BF16)
    a = _dot(xh, rw_ref[...])
    logits = a[:, 0:LANE] + a[:, LANE:2 * LANE] + _dot(xl, rwh_ref[...]) + rb_ref[...]
    lane = lax.broadcasted_iota(jnp.int32, logits.shape, 1)
    work = jnp.where(lane < N_EXPERTS, logits, -jnp.inf)
    tv = jnp.zeros(logits.shape, F32)
    vals, idxs = [], []
    for kk in range(TOP_K):
        m = jnp.max(work, axis=-1, keepdims=True)
        idx = jnp.min(jnp.where(work == m, lane, LANE), axis=-1, keepdims=True)
        work = jnp.where(lane == idx, -jnp.inf, work)
        vals.append(m)
        idxs.append(idx)
    es = [jnp.exp(vv - vals[0]) for vv in vals]
    tot = es[0] + es[1] + es[2] + es[3]
    for kk in range(TOP_K):
        tv = jnp.where(lane == kk, es[kk] / tot, tv)
    tg_ref[...] = tv

    sel = jnp.zeros(logits.shape, F32)
    for idx in idxs:
        sel = sel + (lane == idx).astype(F32)
    tcnt = jnp.floor((jnp.sum(sel, axis=0, keepdims=True) + (RUN_ALIGN - 1.0)) * (1.0 / RUN_ALIGN)) * RUN_ALIGN
    tstart = _dot(jnp.broadcast_to(tcnt, (8, LANE)).astype(BF16), triu_ref[...])[0:1]
    lpos = _dot(tril_ref[...], sel.astype(BF16)) + tstart
    lrank = jnp.zeros(logits.shape, F32)
    for kk in range(TOP_K):
        lp = jnp.sum(jnp.where(lane == idxs[kk], lpos, 0.0), axis=-1, keepdims=True)
        lrank = jnp.where(lane == kk, lp, lrank)
    lrank_ref[...] = lrank.astype(jnp.int32)
    sub = lax.broadcasted_iota(jnp.int32, (8, LANE), 0)
    info = jnp.where(sub == 0, tcnt, jnp.where(sub == 1, tstart, jnp.where(sub == 2, cnt_scr[...], 0.0)))
    info_ref[...] = info.astype(jnp.int32)
    total = cnt_scr[...] + tcnt
    cnt_scr[...] = total
    cnt_ref[...] = total


def _merge(h, mod, g1, o_f, o_b, z, dng, ysc, ymla, wg, bg, wdn, wsc, wmla, wout, g2, rw_cat, rw_hi, rb, j0):
    bsz, t, d = h.shape
    tm = ROW_TILE
    nj = t // tm - j0
    to = nj * tm
    row = lambda w: pl.BlockSpec((None, tm, w), lambda b, j: (b, j + j0, 0))
    orow = lambda w: pl.BlockSpec((None, tm, w), lambda b, j: (b, j, 0))
    full = lambda a: pl.BlockSpec(a.shape, lambda b, j: (0,) * a.ndim)
    ow = DN_HEADS * DN_DV
    sds = jax.ShapeDtypeStruct
    ii = jnp.arange(tm)
    tril = (ii[None, :] < ii[:, None]).astype(BF16)
    ee = jnp.arange(LANE)
    triu = (ee[:, None] < ee[None, :]).astype(BF16)
    tile_i32 = pl.BlockSpec((None, None, 8, LANE), lambda b, j: (b, j, 0, 0))
    return pl.pallas_call(
        _merge_kernel,
        grid=(bsz, nj),
        in_specs=[row(d), pl.BlockSpec((None, None, 6, d), lambda b, j: (b, jnp.minimum(j + j0, 1), 0, 0)), full(g1),
                  row(ow), row(ow), row(Z_W), full(dng), row(SC_WIDTH), row(MLA_HEADS * MLA_V),
                  full(wg), full(bg), full(wdn), full(wsc), full(wmla), full(wout), full(g2),
                  full(rw_cat), full(rw_hi), full(rb), full(tril), full(triu)],
        out_specs=[orow(d), orow(d), orow(LANE), orow(LANE), tile_i32,
                   pl.BlockSpec((1, LANE), lambda b, j: (0, 0))],
        out_shape=[sds((bsz, to, d), F32), sds((bsz, to, d), BF16), sds((bsz, to, LANE), F32),
                   sds((bsz, to, LANE), jnp.int32), sds((bsz, nj, 8, LANE), jnp.int32), sds((1, LANE), F32)],
        scratch_shapes=[pltpu.VMEM((1, LANE), F32)],
        compiler_params=_cparams(("arbitrary", "arbitrary")),
    )(h, mod, g1, o_f, o_b, z, dng, ysc, ymla, wg, bg, wdn, wsc, wmla, wout, g2, rw_cat, rw_hi, rb, tril, triu)


GLU_GROUP = 2 * LANE


def _deinterleave_kernel(w_ref, p_ref, o_ref):
    perm = p_ref[...]
    for g in range(w_ref.shape[1] // GLU_GROUP):
        sl = slice(g * GLU_GROUP, (g + 1) * GLU_GROUP)
        o_ref[:, sl] = _dot(w_ref[:, sl].astype(BF16), perm).astype(BF16)


def _deinterleave_w1(expert_w1, l):
    _, n_e, d, f2 = expert_w1.shape
    i = jnp.arange(GLU_GROUP)
    perm = (jnp.where(i % 2 == 0, i // 2, LANE + i // 2)[:, None] == i[None, :]).astype(BF16)
    return pl.pallas_call(
        _deinterleave_kernel,
        grid=(n_e,),
        in_specs=[pl.BlockSpec((None, None, d, f2), lambda e: (l, e, 0, 0)),
                  pl.BlockSpec(perm.shape, lambda e: (0, 0))],
        out_specs=pl.BlockSpec((None, d, f2), lambda e: (e, 0, 0)),
        out_shape=jax.ShapeDtypeStruct((n_e, d, f2), BF16),
        compiler_params=_cparams(("arbitrary",)),
    )(expert_w1, perm)


RUN_PIECES = tuple(RUN_ALIGN << i for i in range((LOCAL_ROWS // RUN_ALIGN).bit_length()))


def _run_copies(run_ref, make_copy):
    def per_run(e, carry):
        cnt, src, dst = run_ref[0, e], run_ref[1, e], run_ref[2, e]
        off = jnp.int32(0)
        for piece in RUN_PIECES:
            has = (cnt & piece) != 0

            @pl.when(has)
            def _(off=off, piece=piece):
                make_copy(pl.multiple_of(src + off, RUN_ALIGN), pl.multiple_of(dst + off, RUN_ALIGN), piece).start()
            off = off + jnp.where(has, piece, 0)
        return carry
    lax.fori_loop(0, N_RUNS, per_run, 0)


TAIL_PIECES = tuple(RUN_ALIGN << i for i in range((EXPERT_BLOCK // RUN_ALIGN).bit_length() - 1))


def _tail_fill(tail_ref, zeros_ref, xs_hbm, sem):
    def pieces(e, act):
        dst, cnt = tail_ref[0, e], tail_ref[1, e]
        off = jnp.int32(0)
        for piece in TAIL_PIECES:
            has = (cnt & piece) != 0

            @pl.when(has)
            def _(off=off, piece=piece):
                act(pltpu.make_async_copy(zeros_ref.at[pl.ds(0, piece)],
                                          xs_hbm.at[pl.ds(pl.multiple_of(dst + off, RUN_ALIGN), piece)], sem))
            off = off + jnp.where(has, piece, 0)

    def start(e, carry):
        pieces(e, lambda cp: cp.start())
        return carry

    def wait(e, carry):
        pieces(e, lambda cp: cp.wait())
        return carry

    lax.fori_loop(0, N_EXPERTS, start, 0)
    lax.fori_loop(0, N_EXPERTS, wait, 0)


def _unused_fill(nreal_ref, zeros_ref, xs_hbm, sem):
    eb = EXPERT_BLOCK
    n_blocks = xs_hbm.shape[0] // eb

    def copy(i):
        return pltpu.make_async_copy(zeros_ref.at[pl.ds(0, eb)], xs_hbm.at[pl.ds(pl.multiple_of(i * eb, eb), eb)], sem)

    def start(i, carry):
        copy(i).start()
        return carry

    def wait(i, carry):
        copy(i).wait()
        return carry

    lax.fori_loop(nreal_ref[0], n_blocks, start, 0)
    lax.fori_loop(nreal_ref[0], n_blocks, wait, 0)


def _dispatch_kernel(run_ref, tail_ref, nreal_ref, x_ref, lrank_ref, xs_hbm, loc, sem):
    @pl.when(jnp.logical_and(pl.program_id(0) == 0, pl.program_id(1) == 0))
    def _():
        loc[0:EXPERT_BLOCK, :] = jnp.zeros((EXPERT_BLOCK, loc.shape[1]), loc.dtype)
        _tail_fill(tail_ref, loc, xs_hbm, sem)
        _unused_fill(nreal_ref, loc, xs_hbm, sem)

    tm = x_ref.shape[0]
    n_loc = LOCAL_ROWS
    lr = lrank_ref[...].astype(F32).T
    rows = lax.broadcasted_iota(jnp.int32, (n_loc, tm), 0).astype(F32)
    onehot = jnp.zeros((n_loc, tm), F32)
    for kk in range(TOP_K):
        onehot = onehot + jnp.where(rows == lr[kk:kk + 1, :], 1.0, 0.0)
    loc[...] = _pack_halves(_dot(onehot.astype(BF16), x_ref[...]))
    _run_copies(run_ref, lambda s, d, n: pltpu.make_async_copy(loc.at[pl.ds(s, n)], xs_hbm.at[pl.ds(d, n)], sem))
    pltpu.make_async_copy(loc, xs_hbm.at[pl.ds(0, n_loc)], sem).wait()


def _dispatch(runs, tails, n_real, x2, lrank, n_rows):
    bsz, t, d = x2.shape
    tm = ROW_TILE
    w = d // 2
    return pl.pallas_call(
        _dispatch_kernel,
        grid=(bsz, t // tm),
        in_specs=[pl.BlockSpec((None, None, 3, N_RUNS), lambda b, j: (b, j, 0, 0), memory_space=pltpu.SMEM),
                  pl.BlockSpec(memory_space=pltpu.SMEM), pl.BlockSpec(memory_space=pltpu.SMEM),
                  pl.BlockSpec((None, tm, d), lambda b, j: (b, j, 0)),
                  pl.BlockSpec((None, tm, LANE), lambda b, j: (b, j, 0))],
        out_specs=pl.BlockSpec(memory_space=pl.ANY),
        out_shape=jax.ShapeDtypeStruct((n_rows, w), jnp.uint32),
        scratch_shapes=[pltpu.VMEM((LOCAL_ROWS, w), jnp.uint32), pltpu.SemaphoreType.DMA(())],
        compiler_params=_cparams(("arbitrary", "arbitrary")),
    )(runs, tails, n_real, x2, lrank)


def _expert_kernel(be_ref, nreal_ref, xs_ref, w1_ref, b1_ref, w2_ref, b2_ref, y_ref):
    i = pl.program_id(0)
    half = w1_ref.shape[0] // 2

    @pl.when(i < nreal_ref[0])
    def _():
        lo, hi = _unpack_halves(xs_ref[...])
        hdn = (_dot(lo.astype(BF16), w1_ref[0:half, :]) + _dot(hi.astype(BF16), w1_ref[half:, :])) + b1_ref[...]
        acts = []
        for g in range(hdn.shape[1] // GLU_GROUP):
            glu = jnp.minimum(hdn[:, g * GLU_GROUP:g * GLU_GROUP + LANE], SWIGLU_LIMIT)
            lin = jnp.clip(hdn[:, g * GLU_GROUP + LANE:(g + 1) * GLU_GROUP], -SWIGLU_LIMIT, SWIGLU_LIMIT)
            acts.append((glu * _sigmoid(SWIGLU_ALPHA * glu) * (lin + 1.0)).astype(BF16))
        y_ref[...] = _pack_halves(_dot(jnp.concatenate(acts, axis=1), w2_ref[...].astype(BF16)) + b2_ref[...])

    @pl.when(i >= nreal_ref[0])
    def _():
        y_ref[...] = jnp.zeros_like(y_ref)


def _experts(block_expert, n_real, xs, w1, b1, expert_w2, b2, l):
    n_blocks = block_expert.shape[0]
    f, d = expert_w2.shape[2:]
    eb = EXPERT_BLOCK
    wspec = lambda k, n: pl.BlockSpec((None, k, n), lambda i, be, nr: (be[i], 0, 0))
    w2spec = pl.BlockSpec((None, None, f, d), lambda i, be, nr: (l, be[i], 0, 0))
    gs = pltpu.PrefetchScalarGridSpec(
        num_scalar_prefetch=2,
        grid=(n_blocks,),
        in_specs=[pl.BlockSpec((eb, d // 2), lambda i, be, nr: (jnp.minimum(i, nr[0] - 1), 0)),
                  wspec(d, 2 * f), wspec(1, 2 * f), w2spec, wspec(1, d)],
        out_specs=pl.BlockSpec((eb, d // 2), lambda i, be, nr: (i, 0)),
    )
    return pl.pallas_call(
        _expert_kernel,
        grid_spec=gs,
        out_shape=jax.ShapeDtypeStruct((n_blocks * eb, d // 2), jnp.uint32),
        compiler_params=_cparams(("arbitrary",)),
    )(block_expert, n_real, xs, w1, b1, expert_w2, b2)


def _combine_kernel(final, run_ref, lrank_ref, y_hbm, h_ref, mod_ref, tg_ref, fg_ref, o_ref, loc, sem):
    tm = h_ref.shape[0]
    n_loc = LOCAL_ROWS
    _run_copies(run_ref, lambda s, d, n: pltpu.make_async_copy(y_hbm.at[pl.ds(d, n)], loc.at[pl.ds(s, n)], sem))
    lr = lrank_ref[...]
    tg = tg_ref[...]
    cols = lax.broadcasted_iota(jnp.int32, (tm, n_loc), 1)
    wsel = jnp.zeros((tm, n_loc), F32)
    for kk in range(TOP_K):
        wsel = wsel + jnp.where(cols == lr[:, kk:kk + 1], tg[:, kk:kk + 1], 0.0)
    wsel = wsel.astype(BF16)
    pltpu.make_async_copy(y_hbm.at[pl.ds(0, n_loc)], loc, sem).wait()
    lo, hi = _unpack_halves(loc[...])
    y = jnp.concatenate([_dot(wsel, lo.astype(BF16)), _dot(wsel, hi.astype(BF16))], axis=1)
    hn = h_ref[...] + mod_ref[5:6, :] * y
    if final:
        hn = _rms(hn, fg_ref[...])
    o_ref[...] = hn


def _combine(runs, lrank, y_rows, h, mod, tg, fg, final, j0):
    bsz, t, d = h.shape
    tm = ROW_TILE
    nj = t // tm
    row = lambda w: pl.BlockSpec((None, tm, w), lambda b, j: (b, j, 0))
    return pl.pallas_call(
        functools.partial(_combine_kernel, final),
        grid=(bsz, nj),
        in_specs=[pl.BlockSpec((None, None, 3, N_RUNS), lambda b, j: (b, j, 0, 0), memory_space=pltpu.SMEM),
                  row(LANE), pl.BlockSpec(memory_space=pl.ANY),
                  row(d), pl.BlockSpec((None, None, 6, d), lambda b, j: (b, jnp.minimum(j + j0, 1), 0, 0)),
                  row(LANE), pl.BlockSpec(fg.shape, lambda b, j: (0, 0))],
        out_specs=row(d),
        out_shape=jax.ShapeDtypeStruct((bsz, t, d), F32),
        scratch_shapes=[pltpu.VMEM((LOCAL_ROWS, d // 2), jnp.uint32), pltpu.SemaphoreType.DMA(())],
        compiler_params=_cparams(("arbitrary", "arbitrary")),
    )(runs, lrank, y_rows, h, mod, tg, fg)


def _route(info, cnt, n_assign):
    eb = EXPERT_BLOCK
    n_tiles = info.shape[0] * info.shape[1]
    max_rows = n_assign + n_tiles * N_EXPERTS * (RUN_ALIGN - 1)
    filler_blocks = -(-LOCAL_ROWS // eb)
    n_blocks = -(-max_rows // eb) + N_EXPERTS + filler_blocks
    filler_row = (n_blocks - filler_blocks) * eb
    counts = cnt[0, :N_EXPERTS].astype(jnp.int32)
    padded = (counts + eb - 1) // eb * eb
    pad_end = jnp.cumsum(padded)
    pad_start = pad_end - padded
    tcnt, tstart, tcarry = (info[:, :, r, :N_EXPERTS] for r in range(3))
    used = jnp.sum(tcnt, axis=-1, keepdims=True)
    runs = jnp.stack([jnp.concatenate([tcnt, LOCAL_ROWS - used], axis=-1),
                      jnp.concatenate([tstart, used], axis=-1),
                      jnp.concatenate([pad_start + tcarry, jnp.full_like(used, filler_row)], axis=-1)],
                     axis=2).astype(jnp.int32)
    block_row = jnp.arange(n_blocks, dtype=jnp.int32) * eb
    block_expert = jnp.minimum(jnp.sum((pad_end[None, :] <= block_row[:, None]).astype(jnp.int32), axis=1),
                               N_EXPERTS - 1)
    n_real = (pad_end[-1] // eb).astype(jnp.int32).reshape(1)
    tails = jnp.stack([pad_start + counts, padded - counts]).astype(jnp.int32)
    return runs, tails, block_expert, n_real, n_blocks


def _rope_tables(t_lat, t_ctx):
    rows = t_lat // GRID_W
    row = jnp.repeat(jnp.arange(rows, dtype=F32), GRID_W)
    col = jnp.tile(jnp.arange(GRID_W, dtype=F32), rows)
    axis_dims = MLA_ROPE // 2
    inv = ROPE_BASE ** (-jnp.arange(0, axis_dims, 2, dtype=F32) / axis_dims)
    ar, ac = row[:, None] * inv, col[:, None] * inv
    cos64 = jnp.concatenate([jnp.cos(ar), jnp.cos(ar), jnp.cos(ac), jnp.cos(ac)], axis=-1)
    sin64 = jnp.concatenate([-jnp.sin(ar), jnp.sin(ar), -jnp.sin(ac), jnp.sin(ac)], axis=-1)
    pad = jnp.zeros((t_lat, LANE - MLA_ROPE), F32)
    cos_l = jnp.concatenate([cos64, pad], axis=-1)
    sin_l = jnp.concatenate([sin64, pad], axis=-1)
    cos_c = jnp.concatenate([jnp.ones((t_ctx, MLA_ROPE), F32), jnp.zeros((t_ctx, LANE - MLA_ROPE), F32)], axis=-1)
    sin_c = jnp.zeros((t_ctx, LANE), F32)
    return jnp.concatenate([cos_c, cos_l], axis=0), jnp.concatenate([sin_c, sin_l], axis=0)


def _rope_swap_perm():
    half = MLA_ROPE // 4
    perm = []
    for a in range(2):
        base = a * 2 * half
        perm += list(range(base + half, base + 2 * half)) + list(range(base, base + half))
    return jnp.array(perm, jnp.int32)


def _pad_cols(w, width):
    return jnp.concatenate([w, jnp.zeros(w.shape[:-1] + (width - w.shape[-1],), w.dtype)], axis=-1)


def _layer_weights(l, w_in, dn_a_log, dn_dt_bias, mla_w_qb, router_w, router_b, expert_w1, expert_b1):
    nqk = DN_HEADS * DN_DK
    wi = w_in[l]
    o = 0
    dn_qkv = wi[:, o:o + QKV_W]; o += QKV_W
    dn_z = wi[:, o:o + Z_W]; o += Z_W
    dn_ab = wi[:, o:o + 4 * DN_HEADS]; o += 4 * DN_HEADS
    sc3 = wi[:, o:o + SC3_W]; o += SC3_W
    qa = wi[:, o:o + MLA_Q_LORA]; o += MLA_Q_LORA
    ckv = wi[:, o:o + MLA_KV_LORA]; o += MLA_KV_LORA
    kr = wi[:, o:o + MLA_ROPE]
    perm = _rope_swap_perm()
    w_cat = jnp.concatenate([dn_qkv, dn_z, sc3, qa, ckv, _pad_cols(kr, LANE), _pad_cols(kr[:, perm], LANE),
                             _pad_cols(dn_ab, LANE)], axis=-1).astype(BF16)
    dq = MLA_NOPE + MLA_ROPE
    wq = mla_w_qb[l].reshape(MLA_Q_LORA, MLA_HEADS, dq)
    wq_rope = wq[:, :, MLA_NOPE:]
    wq_cat = jnp.concatenate([wq[:, :, :MLA_NOPE], _pad_cols(wq_rope, LANE), _pad_cols(wq_rope[:, :, perm], LANE)],
                             axis=-1).reshape(MLA_Q_LORA, MLA_HEADS * QH_W).astype(BF16)
    alog_row = _pad_cols(jnp.concatenate([dn_a_log[l].reshape(1, -1), jnp.zeros((1, 2 * DN_HEADS), F32)], -1), LANE)
    dtb_row = _pad_cols(jnp.concatenate([dn_dt_bias[l].reshape(1, -1), jnp.zeros((1, 2 * DN_HEADS), F32)], -1), LANE)
    rw = _pad_cols(router_w[l], LANE)
    rw_hi = rw.astype(BF16)
    rw_lo = (rw - rw_hi.astype(F32)).astype(BF16)
    rw_cat = jnp.concatenate([rw_hi, rw_lo], axis=-1)
    rb = jnp.concatenate([router_b[l], jnp.full((LANE - N_EXPERTS,), 0.0, F32)]).reshape(1, LANE)
    w1 = _deinterleave_w1(expert_w1, l)
    n_e, f2 = expert_b1.shape[1:]
    b1 = expert_b1[l].reshape(n_e, f2 // GLU_GROUP, LANE, 2).transpose(0, 1, 3, 2).reshape(n_e, 1, f2)
    return w_cat, wq_cat, alog_row, dtb_row, rw_cat, rw_hi, rb, w1, b1


def kernel(x, c, ctx, c_ctx, ada_w, ada_b, norm1_g, norm2_g, w_in, dn_conv_w, dn_a_log, dn_dt_bias, dn_norm_g, sc_conv_w, mla_q_norm_g, mla_w_qb, mla_kv_norm_g, mla_w_kvb, w_branch_gate, b_branch_gate, w_branch_dn, w_branch_sc, w_branch_mla, w_out, router_w, router_b, expert_w1, expert_b1, expert_w2, expert_b2, final_norm_g):
    bsz, t_lat, d = x.shape
    t_ctx = ctx.shape[1]
    depth = ada_w.shape[0]
    assert t_ctx == ROW_TILE and t_lat % ROW_TILE == 0 and t_lat % GRID_W == 0
    t = t_ctx + t_lat

    cos_t, sin_t = _rope_tables(t_lat, t_ctx)
    n_mod_rows = -(-(bsz + 1) // 8) * 8
    cvec = jnp.concatenate([c, c_ctx[None], jnp.zeros((n_mod_rows - bsz - 1, d), F32)], axis=0)
    mods = _ada_mods(cvec, ada_w, ada_b)

    h = jnp.concatenate([ctx, x], axis=1)
    row2 = lambda v: v.reshape(1, -1)
    for l in range(depth):
        mod_lat = mods[l, :bsz].reshape(bsz, 1, 6, d)
        mod_ctx = jnp.broadcast_to(mods[l, bsz].reshape(1, 1, 6, d), (bsz, 1, 6, d))
        mod = jnp.concatenate([mod_ctx, mod_lat], axis=1)
        (w_cat, wq_cat, alog_row, dtb_row, rw_cat, rw_hi, rb, w1, b1) = _layer_weights(
            l, w_in, dn_a_log, dn_dt_bias, mla_w_qb, router_w, router_b, expert_w1, expert_b1)

        qkv, z, sc, ab, q, k, v = _inproj(h, mod, row2(norm1_g[l]), w_cat, row2(mla_q_norm_g[l]), wq_cat,
                                          row2(mla_kv_norm_g[l]), mla_w_kvb[l].astype(BF16), cos_t, sin_t)
        qkvn, ysc, gb = _prep(qkv, sc, ab, dn_conv_w[l], sc_conv_w[l], alog_row, dtb_row)
        o_f, o_b = _deltanet(qkvn, gb, t_ctx // DN_CHUNK)
        ymla = _mla(q, k, v)
        last = l == depth - 1
        j0 = 1 if last else 0
        hn, xm2, tg, lrank, info, cnt = _merge(h, mod, row2(norm1_g[l]), o_f, o_b, z, row2(dn_norm_g[l]), ysc, ymla,
                                 w_branch_gate[l].astype(BF16), row2(b_branch_gate[l]),
                                 w_branch_dn[l].astype(BF16), w_branch_sc[l].astype(BF16),
                                 w_branch_mla[l].astype(BF16), w_out[l].astype(BF16), row2(norm2_g[l]),
                                 rw_cat, rw_hi, rb, j0)
        runs, tails, block_expert, n_real, n_blocks = _route(info, cnt, bsz * (t - j0 * ROW_TILE) * TOP_K)
        xs = _dispatch(runs, tails, n_real, xm2, lrank, n_blocks * EXPERT_BLOCK)
        y_rows = _experts(block_expert, n_real, xs, w1, b1, expert_w2, expert_b2[l][:, None, :], l)
        h = _combine(runs, lrank, y_rows, hn, mod, tg, row2(final_norm_g), last, j0)
    return h
```

```python
import functools
import math

import jax
import jax.numpy as jnp
from jax import lax
from jax.experimental import pallas as pl
from jax.experimental.pallas import tpu as pltpu

F32 = jnp.float32
BF16 = jnp.bfloat16

GRID_W = 64
NORM_EPS = 1e-6
DN_HEADS = 4
DN_DK = 128
DN_DV = 128
DN_CHUNK = 64
SC_WIDTH = 512
MLA_HEADS = 4
MLA_Q_LORA = 256
MLA_KV_LORA = 128
MLA_NOPE = 128
MLA_ROPE = 64
MLA_V = 128
MLA_SCALE = (MLA_NOPE + MLA_ROPE) ** -0.5
Q_SCALE = MLA_SCALE * math.log2(math.e)
ROPE_BASE = 10000.0
N_EXPERTS = 32
TOP_K = 4
EXPERT_FF = 1024
SWIGLU_ALPHA = 1.702
SWIGLU_LIMIT = 7.0

LANE = 128
ROW_TILE = 256
HALO = 16
EXPERT_BLOCK = 512
PROJ_BATCH = 2
DN_BATCH = 4
RUN_ALIGN = 8
LOCAL_ROWS = TOP_K * ROW_TILE + N_EXPERTS * RUN_ALIGN
N_RUNS = N_EXPERTS + 1
VMEM_LIMIT = 56 * 1024 * 1024

QKV_W = 3 * DN_HEADS * DN_DK
Z_W = DN_HEADS * DN_DV
SC3_W = 3 * SC_WIDTH
KVA_W = 3 * LANE
AB_W = LANE
IN_W = QKV_W + Z_W + SC3_W + MLA_Q_LORA + KVA_W + AB_W
QH_W = 3 * LANE
QK_W = 2 * LANE


def _cparams(sem):
    return pltpu.CompilerParams(dimension_semantics=sem, vmem_limit_bytes=VMEM_LIMIT)


def _dot(a, b):
    return jnp.dot(a, b, preferred_element_type=F32)


def _dot_nt(a, b):
    return lax.dot_general(a, b, (((1,), (1,)), ((), ())), preferred_element_type=F32)


def _rms(x, g):
    return x * lax.rsqrt(jnp.mean(x * x, axis=-1, keepdims=True) + NORM_EPS) * g


def _sigmoid(x):
    return 1.0 / (1.0 + jnp.exp(-x))


def _ada_kernel(c_ref, w_ref, b_ref, o_ref):
    cv = c_ref[...]
    s = cv * _sigmoid(cv)
    o_ref[...] = _dot(s.astype(BF16), w_ref[...].astype(BF16)) + b_ref[...]


def _ada_mods(cvec, ada_w, ada_b):
    n_layers, d, d6 = ada_w.shape
    r = cvec.shape[0]
    tn = 512
    return pl.pallas_call(
        _ada_kernel,
        grid=(n_layers, d6 // tn),
        in_specs=[pl.BlockSpec((r, d), lambda l, n: (0, 0)),
                  pl.BlockSpec((None, d, tn), lambda l, n: (l, 0, n)),
                  pl.BlockSpec((None, 1, tn), lambda l, n: (l, 0, n))],
        out_specs=pl.BlockSpec((None, r, tn), lambda l, n: (l, 0, n)),
        out_shape=jax.ShapeDtypeStruct((n_layers, r, d6), F32),
        compiler_params=_cparams(("arbitrary", "arbitrary")),
    )(cvec, ada_w, ada_b.reshape(n_layers, 1, d6))


def _inproj_kernel(h_ref, mod_ref, g1_ref, w_ref, gq_ref, wq_ref, gkv_ref, wkv_ref, cos_ref, sin_ref,
                   qkv_ref, z_ref, sc_ref, ab_ref, q_ref, k_ref, v_ref):
    nb, tm = h_ref.shape[0], h_ref.shape[1]
    xm = jnp.concatenate([_rms(h_ref[bb], g1_ref[...]) * (1.0 + mod_ref[bb, 1:2, :]) + mod_ref[bb, 0:1, :]
                          for bb in range(nb)], axis=0).astype(BF16)
    p = _dot(xm, w_ref[...])
    rows = lambda bb: slice(bb * tm, (bb + 1) * tm)
    o = 0
    qa = p[:, o:o + MLA_Q_LORA]
    o += MLA_Q_LORA
    ckv = p[:, o:o + LANE]
    kr = p[:, o + LANE:o + 2 * LANE]
    krs = p[:, o + 2 * LANE:o + 3 * LANE]
    o += KVA_W
    for bb in range(nb):
        qkv_ref[bb] = p[rows(bb), o:o + QKV_W].astype(BF16)
        z_ref[bb] = p[rows(bb), o + QKV_W:o + QKV_W + Z_W].astype(BF16)
        sc_ref[bb] = p[rows(bb), o + QKV_W + Z_W:o + QKV_W + Z_W + SC3_W].astype(BF16)
        ab_ref[bb] = p[rows(bb), o + QKV_W + Z_W + SC3_W:o + QKV_W + Z_W + SC3_W + AB_W]

    cos = jnp.concatenate([cos_ref[...]] * nb, axis=0)
    sin = jnp.concatenate([sin_ref[...]] * nb, axis=0)
    k_rope = (kr * cos + krs * sin).astype(BF16)
    qn = _rms(qa, gq_ref[...]).astype(BF16)
    qf = _dot(qn, wq_ref[...])
    kvn = _rms(ckv, gkv_ref[...]).astype(BF16)
    kv = _dot(kvn, wkv_ref[...])
    for hh in range(MLA_HEADS):
        b0 = hh * QH_W
        q_nope = (qf[:, b0:b0 + LANE] * Q_SCALE).astype(BF16)
        q_rope = qf[:, b0 + LANE:b0 + 2 * LANE] * cos + qf[:, b0 + 2 * LANE:b0 + 3 * LANE] * sin
        q_rope = (q_rope * Q_SCALE).astype(BF16)
        c0 = hh * (MLA_NOPE + MLA_V)
        k_nope = kv[:, c0:c0 + MLA_NOPE].astype(BF16)
        vv = kv[:, c0 + MLA_NOPE:c0 + MLA_NOPE + MLA_V].astype(BF16)
        for bb in range(nb):
            q_ref[bb, hh, :, 0:LANE] = q_nope[rows(bb)]
            q_ref[bb, hh, :, LANE:2 * LANE] = q_rope[rows(bb)]
            k_ref[bb, hh, :, 0:LANE] = k_nope[rows(bb)]
            k_ref[bb, hh, :, LANE:2 * LANE] = k_rope[rows(bb)]
            v_ref[bb, hh] = vv[rows(bb)]


def _inproj(h, mod, g1, w_cat, gq, wq_cat, gkv, wkv, cos_t, sin_t):
    bsz, t, d = h.shape
    tm = ROW_TILE
    nj = t // tm
    nb = PROJ_BATCH if bsz % PROJ_BATCH == 0 else 1
    row = lambda w: pl.BlockSpec((nb, tm, w), lambda b, j: (b, j, 0))
    full = lambda a: pl.BlockSpec(a.shape, lambda b, j: (0,) * a.ndim)
    head = lambda w: pl.BlockSpec((nb, MLA_HEADS, tm, w), lambda b, j: (b, 0, j, 0))
    sds = jax.ShapeDtypeStruct
    return pl.pallas_call(
        _inproj_kernel,
        grid=(bsz // nb, nj),
        in_specs=[row(d),
                  pl.BlockSpec((nb, None, 6, d), lambda b, j: (b, jnp.minimum(j, 1), 0, 0)),
                  full(g1), full(w_cat), full(gq), full(wq_cat), full(gkv), full(wkv),
                  pl.BlockSpec((tm, LANE), lambda b, j: (j, 0)),
                  pl.BlockSpec((tm, LANE), lambda b, j: (j, 0))],
        out_specs=[row(QKV_W), row(Z_W), row(SC3_W), row(AB_W), head(QK_W), head(QK_W), head(MLA_V)],
        out_shape=[sds((bsz, t, QKV_W), BF16), sds((bsz, t, Z_W), BF16), sds((bsz, t, SC3_W), BF16),
                   sds((bsz, t, AB_W), F32),
                   sds((bsz, MLA_HEADS, t, QK_W), BF16), sds((bsz, MLA_HEADS, t, QK_W), BF16),
                   sds((bsz, MLA_HEADS, t, MLA_V), BF16)],
        compiler_params=_cparams(("parallel", "arbitrary")),
    )(h, mod, g1, w_cat, gq, wq_cat, gkv, wkv, cos_t, sin_t)


def _shift_rows(x, prev_row, next_row):
    tm = x.shape[0]
    rid = lax.broadcasted_iota(jnp.int32, x.shape, 0)
    xp = jnp.where(rid == 0, prev_row, pltpu.roll(x, 1, 0))
    xn = jnp.where(rid == tm - 1, next_row, pltpu.roll(x, tm - 1, 0))
    return xp, xn


def _prep_kernel(qkv_ref, qkv_p_ref, qkv_n_ref, sc_ref, sc_p_ref, sc_n_ref, ab_ref,
                 dnw_ref, scw_ref, alog_ref, dtb_ref, qkvn_ref, ysc_ref, gb_ref):
    j = pl.program_id(1)
    nj = pl.num_programs(1)
    pv = (j >= 2).astype(F32)
    nv = jnp.logical_and(j >= 1, j <= nj - 2).astype(F32)

    x = qkv_ref[...].astype(F32)
    xp, xn = _shift_rows(x, qkv_p_ref[HALO - 1:HALO, :].astype(F32) * pv, qkv_n_ref[0:1, :].astype(F32) * nv)
    y = xp * dnw_ref[0:1, :] + x * dnw_ref[1:2, :] + xn * dnw_ref[2:3, :]
    y = y * _sigmoid(y)
    nqk = DN_HEADS * DN_DK
    for g in range(2 * DN_HEADS):
        yy = y[:, g * DN_DK:(g + 1) * DN_DK]
        yy = yy * lax.rsqrt(jnp.sum(yy * yy, axis=-1, keepdims=True) + 1e-6)
        if g < DN_HEADS:
            yy = yy * (DN_DK ** -0.5)
        qkvn_ref[:, g * DN_DK:(g + 1) * DN_DK] = yy.astype(BF16)
    qkvn_ref[:, 2 * nqk:] = y[:, 2 * nqk:].astype(BF16)

    s = sc_ref[...].astype(F32)
    sp = sc_p_ref[HALO - 1:HALO, :].astype(F32) * pv
    sn = sc_n_ref[0:1, :].astype(F32) * nv
    w = SC_WIDTH
    u = s[:, 2 * w:3 * w] * s[:, 0:w]
    up, un = _shift_rows(u, sp[:, 2 * w:3 * w] * sp[:, 0:w], sn[:, 2 * w:3 * w] * sn[:, 0:w])
    conv = up * scw_ref[0:1, :] + u * scw_ref[1:2, :] + un * scw_ref[2:3, :]
    ysc_ref[...] = (s[:, w:2 * w] * conv).astype(BF16)

    ab = ab_ref[...]
    sp_arg = ab + dtb_ref[...]
    softplus = jnp.maximum(sp_arg, 0.0) + jnp.log1p(jnp.exp(-jnp.abs(sp_arg)))
    gval = -jnp.exp(alog_ref[...]) * softplus
    lane = lax.broadcasted_iota(jnp.int32, ab.shape, 1)
    gb_ref[...] = jnp.where(lane < 2 * DN_HEADS, gval, _sigmoid(ab))


def _prep(qkv, sc, ab, dn_conv_w, sc_conv_w, alog_row, dtb_row):
    bsz, t, _ = qkv.shape
    tm = ROW_TILE
    nj = t // tm
    hb = tm // HALO
    nh = t // HALO
    row = lambda w: pl.BlockSpec((None, tm, w), lambda b, j: (b, j, 0))
    prev = lambda w: pl.BlockSpec((None, HALO, w), lambda b, j: (b, jnp.maximum(j * hb - 1, 0), 0))
    nxt = lambda w: pl.BlockSpec((None, HALO, w), lambda b, j: (b, jnp.minimum((j + 1) * hb, nh - 1), 0))
    full = lambda a: pl.BlockSpec(a.shape, lambda b, j: (0,) * a.ndim)
    sds = jax.ShapeDtypeStruct
    return pl.pallas_call(
        _prep_kernel,
        grid=(bsz, nj),
        in_specs=[row(QKV_W), prev(QKV_W), nxt(QKV_W), row(SC3_W), prev(SC3_W), nxt(SC3_W), row(AB_W),
                  full(dn_conv_w), full(sc_conv_w), full(alog_row), full(dtb_row)],
        out_specs=[row(QKV_W), row(SC_WIDTH), row(AB_W)],
        out_shape=[sds((bsz, t, QKV_W), BF16), sds((bsz, t, SC_WIDTH), BF16), sds((bsz, t, AB_W), F32)],
        compiler_params=_cparams(("parallel", "arbitrary")),
    )(qkv, qkv, qkv, sc, sc, sc, ab, dn_conv_w, sc_conv_w, alog_row, dtb_row)


DN_GROUP = 2


def _stack_heads(x, h0):
    return jnp.concatenate([x[:, (h0 + i) * LANE:(h0 + i + 1) * LANE] for i in range(DN_GROUP)], axis=0)


def _stack_cols(x, c0):
    return jnp.concatenate([x[:, c0 + i:c0 + i + 1] for i in range(DN_GROUP)], axis=0)


def _dn_direction(d, h0, x, gbt, tri, m_causal, m_strict, eye, s_ref):
    c = DN_CHUNK
    grp = range(DN_GROUP)
    nqk = DN_HEADS * DN_DK
    qst = _stack_heads(x[:, 0:nqk], h0)
    kst = _stack_heads(x[:, nqk:2 * nqk], h0)
    vst = _stack_heads(x[:, 2 * nqk:], h0)
    gc_all = jnp.dot(tri, gbt, preferred_element_type=F32, precision=lax.Precision.HIGHEST)
    col0 = d * DN_HEADS + h0
    gc = _stack_cols(gc_all, col0)
    beta = _stack_cols(gbt, 2 * DN_HEADS + col0)
    last = c - 1 if d == 0 else 0
    gl_heads = [gc_all[last:last + 1, col0 + i:col0 + i + 1] for i in grp]
    gl = jnp.concatenate([jnp.broadcast_to(g1, (c, 1)) for g1 in gl_heads], axis=0)
    kf = kst.astype(F32)
    kb = kf * beta
    kk = _dot_nt(kb.astype(BF16), kst)
    qk = _dot_nt(qst, kst)
    yield

    hc = DN_GROUP * c
    gmat = jnp.broadcast_to(gc, (hc, hc))
    dec = jnp.exp(jnp.minimum(gmat - gmat.T, 0.0))
    xpow = -(kk * jnp.where(m_strict > 0.5, dec, 0.0))
    tinv = eye + xpow
    intra = (qk * jnp.where(m_causal > 0.5, dec, 0.0)).astype(BF16)
    eg = jnp.exp(gc)
    rhs = jnp.concatenate([vst.astype(F32) * beta, kb * eg], axis=1).astype(BF16)
    qd = qst.astype(F32) * eg
    kdt = (kf * jnp.exp(gl - gc)).T.astype(BF16)
    for _ in range(int(math.log2(c)) - 1):
        xb = xpow.astype(BF16)
        xpow = _dot(xb, xb)
        yield
        tinv = tinv + _dot(tinv.astype(BF16), xpow.astype(BF16))
        yield
    sol = _dot(tinv.astype(BF16), rhs)
    yield
    u = sol[:, 0:DN_DV]
    w = sol[:, DN_DV:]

    s_cat = s_ref[...]
    wq = jnp.concatenate([w, qd], axis=0).astype(BF16)
    full = _dot(wq, s_cat.astype(BF16))
    yield
    ws = jnp.concatenate([full[i * c:(i + 1) * c, i * DN_DV:(i + 1) * DN_DV] for i in grp], axis=0)
    qs = jnp.concatenate([full[hc + i * c:hc + (i + 1) * c, i * DN_DV:(i + 1) * DN_DV] for i in grp], axis=0)
    vnew = u - ws
    o = qs + _dot(intra, vnew.astype(BF16))
    rhead = lax.broadcasted_iota(jnp.int32, (hc, DN_DV), 0) // c
    vblk = jnp.concatenate([jnp.where(rhead == i, vnew, 0.0) for i in grp], axis=1).astype(BF16)
    gt = jnp.concatenate([jnp.broadcast_to(jnp.exp(g1), (1, DN_DV)) for g1 in gl_heads], axis=1)
    s_ref[...] = s_cat * gt + _dot(kdt, vblk)
    yield
    return jnp.concatenate([o[i * c:(i + 1) * c] for i in grp], axis=1)


def _lockstep(gens):
    outs = [None] * len(gens)
    live = list(range(len(gens)))
    while live:
        for i in list(live):
            try:
                next(gens[i])
            except StopIteration as done:
                outs[i] = done.value
                live.remove(i)
    return outs


def _dn_kernel(xf_ref, gf_ref, xb_ref, gb_ref, trif_ref, trib_ref, mcf_ref, msf_ref, mcb_ref, msb_ref, eye_ref,
               of_ref, ob_ref, s_ref):
    @pl.when(pl.program_id(1) == 0)
    def _():
        s_ref[...] = jnp.zeros_like(s_ref)

    eye = eye_ref[...]
    gw = DN_GROUP * DN_DV
    gens, dsts = [], []
    for bb in range(xf_ref.shape[0]):
        for g in range(DN_HEADS // DN_GROUP):
            gens.append(_dn_direction(0, g * DN_GROUP, xf_ref[bb], gf_ref[bb], trif_ref[...], mcf_ref[...],
                                      msf_ref[...], eye, s_ref.at[bb, 0, :, g * gw:(g + 1) * gw]))
            dsts.append((of_ref, bb, g))
            gens.append(_dn_direction(1, g * DN_GROUP, xb_ref[bb], gb_ref[bb], trib_ref[...], mcb_ref[...],
                                      msb_ref[...], eye, s_ref.at[bb, 1, :, g * gw:(g + 1) * gw]))
            dsts.append((ob_ref, bb, g))
    for (o_ref, bb, g), o in zip(dsts, _lockstep(gens)):
        o_ref[bb, :, g * gw:(g + 1) * gw] = o.astype(BF16)


def _deltanet(qkvn, gb, n_ctx_chunks):
    bsz, t, _ = qkvn.shape
    c = DN_CHUNK
    nc = t // c
    hc = DN_GROUP * c
    bmap = lambda i: jnp.where(i < n_ctx_chunks, n_ctx_chunks - 1 - i, nc + n_ctx_chunks - 1 - i)
    ii = jnp.arange(c)
    tri_f = (ii[None, :] <= ii[:, None]).astype(F32)
    tri_b = (ii[None, :] >= ii[:, None]).astype(F32)
    r = jnp.arange(hc)
    same = (r[:, None] // c) == (r[None, :] // c)
    pi, pj = r[:, None] % c, r[None, :] % c
    mc_f = (same & (pj <= pi)).astype(F32)
    ms_f = (same & (pj < pi)).astype(F32)
    mc_b = (same & (pj >= pi)).astype(F32)
    ms_b = (same & (pj > pi)).astype(F32)
    eye = jnp.eye(hc, dtype=F32)
    full = lambda a: pl.BlockSpec(a.shape, lambda b, i: (0,) * a.ndim)
    nb = DN_BATCH if bsz % DN_BATCH == 0 else 1
    fw = lambda w: pl.BlockSpec((nb, c, w), lambda b, i: (b, i, 0))
    bw = lambda w: pl.BlockSpec((nb, c, w), lambda b, i: (b, bmap(i), 0))
    ow = DN_HEADS * DN_DV
    sds = jax.ShapeDtypeStruct
    return pl.pallas_call(
        _dn_kernel,
        grid=(bsz // nb, nc),
        in_specs=[fw(QKV_W), fw(AB_W), bw(QKV_W), bw(AB_W), full(tri_f), full(tri_b),
                  full(mc_f), full(ms_f), full(mc_b), full(ms_b), full(eye)],
        out_specs=[fw(ow), bw(ow)],
        out_shape=[sds((bsz, t, ow), BF16), sds((bsz, t, ow), BF16)],
        scratch_shapes=[pltpu.VMEM((nb, 2, DN_DK, DN_HEADS * DN_DV), F32)],
        compiler_params=_cparams(("arbitrary", "arbitrary")),
    )(qkvn, gb, qkvn, gb, tri_f, tri_b, mc_f, ms_f, mc_b, ms_b, eye)


MLA_GROUP = 2


def _softmax_av(q, k, v):
    s = _dot_nt(q, k)
    yield
    m = jnp.max(s, axis=-1, keepdims=True)
    p = jnp.exp2(s - m)
    l = jnp.sum(p, axis=-1, keepdims=True)
    p = p.astype(BF16)
    yield
    return _dot(p, v) / l


def _mla_kernel(q_ref, k_ref, v_ref, o_ref):
    j = pl.program_id(2)
    tc = q_ref.shape[1]

    def attend(n_keys):
        outs = _lockstep([_softmax_av(q_ref[g], k_ref[g, 0:n_keys, :], v_ref[g, 0:n_keys, :])
                          for g in range(MLA_GROUP)])
        for g, o in enumerate(outs):
            o_ref[:, g * MLA_V:(g + 1) * MLA_V] = o.astype(BF16)

    @pl.when(j == 0)
    def _():
        attend(tc)

    @pl.when(j > 0)
    def _():
        attend(k_ref.shape[1])


def _mla(q, k, v):
    bsz, nh, t, _ = q.shape
    tq = ROW_TILE
    g = MLA_GROUP
    return pl.pallas_call(
        _mla_kernel,
        grid=(bsz, nh // g, t // tq),
        in_specs=[pl.BlockSpec((None, g, tq, QK_W), lambda b, hp, j: (b, hp, j, 0)),
                  pl.BlockSpec((None, g, t, QK_W), lambda b, hp, j: (b, hp, 0, 0)),
                  pl.BlockSpec((None, g, t, MLA_V), lambda b, hp, j: (b, hp, 0, 0))],
        out_specs=pl.BlockSpec((None, tq, g * MLA_V), lambda b, hp, j: (b, j, hp)),
        out_shape=jax.ShapeDtypeStruct((bsz, t, nh * MLA_V), BF16),
        compiler_params=_cparams(("parallel", "parallel", "arbitrary")),
    )(q, k, v)


def _pack_halves(x):
    w = x.shape[1] // 2
    lo = lax.bitcast_convert_type(x[:, :w].astype(BF16).astype(F32), jnp.uint32)
    hi = lax.bitcast_convert_type(x[:, w:].astype(BF16).astype(F32), jnp.uint32)
    return (hi & jnp.uint32(0xFFFF0000)) | (lo >> 16)


def _unpack_halves(p):
    lo = lax.bitcast_convert_type(p << 16, F32)
    hi = lax.bitcast_convert_type(p & jnp.uint32(0xFFFF0000), F32)
    return lo, hi


def _merge_kernel(h_ref, mod_ref, g1_ref, of_ref, ob_ref, z_ref, dng_ref, ysc_ref, ymla_ref,
                  wg_ref, bg_ref, wdn_ref, wsc_ref, wmla_ref, wout_ref, g2_ref, rw_ref, rwh_ref, rb_ref, tril_ref, triu_ref,
                  hn_ref, xm2_ref, tg_ref, lrank_ref, info_ref, cnt_ref, cnt_scr):
    @pl.when(jnp.logical_and(pl.program_id(0) == 0, pl.program_id(1) == 0))
    def _():
        cnt_scr[...] = jnp.zeros_like(cnt_scr)

    nb, tm, d = h_ref.shape
    rows = lambda bb: slice(bb * tm, (bb + 1) * tm)
    cat = lambda ref: jnp.concatenate([ref[bb] for bb in range(nb)], axis=0)
    xm = jnp.concatenate([_rms(h_ref[bb], g1_ref[...]) * (1.0 + mod_ref[bb, 1:2, :]) + mod_ref[bb, 0:1, :]
                          for bb in range(nb)], axis=0).astype(BF16)
    o = cat(of_ref).astype(F32) + cat(ob_ref).astype(F32)
    z = cat(z_ref).astype(F32)
    parts = []
    for hh in range(DN_HEADS):
        oh = o[:, hh * DN_DV:(hh + 1) * DN_DV]
        zh = z[:, hh * DN_DV:(hh + 1) * DN_DV]
        parts.append(_rms(oh, dng_ref[...]) * (zh * _sigmoid(zh)))
    ydn = jnp.concatenate(parts, axis=1).astype(BF16)
    gates = _sigmoid(_dot(xm, wg_ref[...]) + bg_ref[...])
    merged = (gates[:, 0:d] * _dot(ydn, wdn_ref[...]) + gates[:, d:2 * d] * _dot(cat(ysc_ref), wsc_ref[...])
              + gates[:, 2 * d:3 * d] * _dot(cat(ymla_ref), wmla_ref[...]))
    y = _dot(merged.astype(BF16), wout_ref[...])
    xm2s = []
    for bb in range(nb):
        hn = h_ref[bb] + mod_ref[bb, 2:3, :] * y[rows(bb)]
        hn_ref[bb] = hn
        xm2s.append(_rms(hn, g2_ref[...]) * (1.0 + mod_ref[bb, 4:5, :]) + mod_ref[bb, 3:4, :])
    xm2 = jnp.concatenate(xm2s, axis=0)
    xh = xm2.astype(BF16)
    for bb in range(nb):
        xm2_ref[bb] = xh[rows(bb)]

    xl = (xm2 - xh.astype(F32)).astype(BF16)
    a = _dot(xh, rw_ref[...])
    logits = a[:, 0:LANE] + a[:, LANE:2 * LANE] + _dot(xl, rwh_ref[...]) + rb_ref[...]
    lane = lax.broadcasted_iota(jnp.int32, logits.shape, 1)
    work = jnp.where(lane < N_EXPERTS, logits, -jnp.inf)
    tv = jnp.zeros(logits.shape, F32)
    vals, idxs = [], []
    for kk in range(TOP_K):
        m = jnp.max(work, axis=-1, keepdims=True)
        idx = jnp.min(jnp.where(work == m, lane, LANE), axis=-1, keepdims=True)
        work = jnp.where(lane == idx, -jnp.inf, work)
        vals.append(m)
        idxs.append(idx)
    es = [jnp.exp(vv - vals[0]) for vv in vals]
    tot = es[0] + es[1] + es[2] + es[3]
    for kk in range(TOP_K):
        tv = jnp.where(lane == kk, es[kk] / tot, tv)
    sels = [(lane == idx).astype(F32) for idx in idxs]
    sel = sels[0] + sels[1] + sels[2] + sels[3]

    lane_t = lax.broadcasted_iota(jnp.int32, (tm, LANE), 1)
    sub = lax.broadcasted_iota(jnp.int32, (8, LANE), 0)
    for bb in range(nb):
        tg_ref[bb] = tv[rows(bb)]
        sel_t = sel[rows(bb)]
        tcnt = jnp.floor((jnp.sum(sel_t, axis=0, keepdims=True) + (RUN_ALIGN - 1.0)) * (1.0 / RUN_ALIGN)) * RUN_ALIGN
        tstart = _dot(jnp.broadcast_to(tcnt, (8, LANE)).astype(BF16), triu_ref[...])[0:1]
        lpos = _dot(tril_ref[...], sel_t.astype(BF16)) + tstart
        lrank = jnp.zeros((tm, LANE), F32)
        for kk in range(TOP_K):
            lp = jnp.sum(sels[kk][rows(bb)] * lpos, axis=-1, keepdims=True)
            lrank = jnp.where(lane_t == kk, lp, lrank)
        lrank_ref[bb] = lrank.astype(jnp.int32)
        carry = cnt_scr[...]
        info = jnp.where(sub == 0, tcnt, jnp.where(sub == 1, tstart, jnp.where(sub == 2, carry, 0.0)))
        info_ref[bb] = info.astype(jnp.int32)
        cnt_scr[...] = carry + tcnt
    cnt_ref[...] = cnt_scr[...]


def _merge(h, mod, g1, o_f, o_b, z, dng, ysc, ymla, wg, bg, wdn, wsc, wmla, wout, g2, rw_cat, rw_hi, rb, j0):
    bsz, t, d = h.shape
    tm = ROW_TILE
    nj = t // tm - j0
    to = nj * tm
    nb = PROJ_BATCH if bsz % PROJ_BATCH == 0 else 1
    row = lambda w: pl.BlockSpec((nb, tm, w), lambda b, j: (b, j + j0, 0))
    orow = lambda w: pl.BlockSpec((nb, tm, w), lambda b, j: (b, j, 0))
    full = lambda a: pl.BlockSpec(a.shape, lambda b, j: (0,) * a.ndim)
    ow = DN_HEADS * DN_DV
    sds = jax.ShapeDtypeStruct
    ii = jnp.arange(tm)
    tril = (ii[None, :] < ii[:, None]).astype(BF16)
    ee = jnp.arange(LANE)
    triu = (ee[:, None] < ee[None, :]).astype(BF16)
    tile_i32 = pl.BlockSpec((nb, None, 8, LANE), lambda b, j: (b, j, 0, 0))
    return pl.pallas_call(
        _merge_kernel,
        grid=(bsz // nb, nj),
        in_specs=[row(d), pl.BlockSpec((nb, None, 6, d), lambda b, j: (b, jnp.minimum(j + j0, 1), 0, 0)), full(g1),
                  row(ow), row(ow), row(Z_W), full(dng), row(SC_WIDTH), row(MLA_HEADS * MLA_V),
                  full(wg), full(bg), full(wdn), full(wsc), full(wmla), full(wout), full(g2),
                  full(rw_cat), full(rw_hi), full(rb), full(tril), full(triu)],
        out_specs=[orow(d), orow(d), orow(LANE), orow(LANE), tile_i32,
                   pl.BlockSpec((1, LANE), lambda b, j: (0, 0))],
        out_shape=[sds((bsz, to, d), F32), sds((bsz, to, d), BF16), sds((bsz, to, LANE), F32),
                   sds((bsz, to, LANE), jnp.int32), sds((bsz, nj, 8, LANE), jnp.int32), sds((1, LANE), F32)],
        scratch_shapes=[pltpu.VMEM((1, LANE), F32)],
        compiler_params=_cparams(("arbitrary", "arbitrary")),
    )(h, mod, g1, o_f, o_b, z, dng, ysc, ymla, wg, bg, wdn, wsc, wmla, wout, g2, rw_cat, rw_hi, rb, tril, triu)


GLU_GROUP = 2 * LANE


def _deinterleave_kernel(w_ref, p_ref, o_ref):
    perm = p_ref[...]
    for g in range(w_ref.shape[1] // GLU_GROUP):
        sl = slice(g * GLU_GROUP, (g + 1) * GLU_GROUP)
        o_ref[:, sl] = _dot(w_ref[:, sl].astype(BF16), perm).astype(BF16)


def _deinterleave_w1(expert_w1, l):
    _, n_e, d, f2 = expert_w1.shape
    i = jnp.arange(GLU_GROUP)
    perm = (jnp.where(i % 2 == 0, i // 2, LANE + i // 2)[:, None] == i[None, :]).astype(BF16)
    return pl.pallas_call(
        _deinterleave_kernel,
        grid=(n_e,),
        in_specs=[pl.BlockSpec((None, None, d, f2), lambda e: (l, e, 0, 0)),
                  pl.BlockSpec(perm.shape, lambda e: (0, 0))],
        out_specs=pl.BlockSpec((None, d, f2), lambda e: (e, 0, 0)),
        out_shape=jax.ShapeDtypeStruct((n_e, d, f2), BF16),
        compiler_params=_cparams(("arbitrary",)),
    )(expert_w1, perm)


RUN_PIECES = tuple(RUN_ALIGN << i for i in range((LOCAL_ROWS // RUN_ALIGN).bit_length()))


def _run_copies(run_ref, make_copy):
    def per_run(e, carry):
        cnt, src, dst = run_ref[0, e], run_ref[1, e], run_ref[2, e]
        off = jnp.int32(0)
        for piece in RUN_PIECES:
            has = (cnt & piece) != 0

            @pl.when(has)
            def _(off=off, piece=piece):
                make_copy(pl.multiple_of(src + off, RUN_ALIGN), pl.multiple_of(dst + off, RUN_ALIGN), piece).start()
            off = off + jnp.where(has, piece, 0)
        return carry
    lax.fori_loop(0, N_RUNS, per_run, 0)


TAIL_PIECES = tuple(RUN_ALIGN << i for i in range((EXPERT_BLOCK // RUN_ALIGN).bit_length() - 1))


def _tail_fill(tail_ref, zeros_ref, xs_hbm, sem):
    def pieces(e, act):
        dst, cnt = tail_ref[0, e], tail_ref[1, e]
        off = jnp.int32(0)
        for piece in TAIL_PIECES:
            has = (cnt & piece) != 0

            @pl.when(has)
            def _(off=off, piece=piece):
                act(pltpu.make_async_copy(zeros_ref.at[pl.ds(0, piece)],
                                          xs_hbm.at[pl.ds(pl.multiple_of(dst + off, RUN_ALIGN), piece)], sem))
            off = off + jnp.where(has, piece, 0)

    def start(e, carry):
        pieces(e, lambda cp: cp.start())
        return carry

    def wait(e, carry):
        pieces(e, lambda cp: cp.wait())
        return carry

    lax.fori_loop(0, N_EXPERTS, start, 0)
    lax.fori_loop(0, N_EXPERTS, wait, 0)


def _unused_fill(nreal_ref, zeros_ref, xs_hbm, sem):
    eb = EXPERT_BLOCK
    n_blocks = xs_hbm.shape[0] // eb

    def copy(i):
        return pltpu.make_async_copy(zeros_ref.at[pl.ds(0, eb)], xs_hbm.at[pl.ds(pl.multiple_of(i * eb, eb), eb)], sem)

    def start(i, carry):
        copy(i).start()
        return carry

    def wait(i, carry):
        copy(i).wait()
        return carry

    lax.fori_loop(nreal_ref[0], n_blocks, start, 0)
    lax.fori_loop(nreal_ref[0], n_blocks, wait, 0)


def _dispatch_kernel(run_ref, tail_ref, nreal_ref, x_ref, lrank_ref, xs_hbm, loc, sem):
    step = pl.program_id(0) * pl.num_programs(1) + pl.program_id(1)
    n_steps = pl.num_programs(0) * pl.num_programs(1)
    slot = step % 2
    tm = x_ref.shape[0]
    n_loc = LOCAL_ROWS

    def wait_runs(sl):
        pltpu.make_async_copy(loc.at[sl], xs_hbm.at[pl.ds(0, n_loc)], sem.at[sl]).wait()

    @pl.when(step == 0)
    def _():
        loc[0, 0:EXPERT_BLOCK, :] = jnp.zeros((EXPERT_BLOCK, loc.shape[2]), loc.dtype)
        _tail_fill(tail_ref, loc.at[0], xs_hbm, sem.at[0])
        _unused_fill(nreal_ref, loc.at[0], xs_hbm, sem.at[0])

    @pl.when(step >= 2)
    def _():
        wait_runs(slot)

    lr = lrank_ref[...].astype(F32).T
    rows = lax.broadcasted_iota(jnp.int32, (n_loc, tm), 0).astype(F32)
    onehot = jnp.zeros((n_loc, tm), F32)
    for kk in range(TOP_K):
        onehot = jnp.where(rows == lr[kk:kk + 1, :], 1.0, onehot)
    loc[slot] = _pack_halves(_dot(onehot.astype(BF16), x_ref[...]))
    _run_copies(run_ref, lambda s, d, n: pltpu.make_async_copy(
        loc.at[slot, pl.ds(s, n)], xs_hbm.at[pl.ds(d, n)], sem.at[slot]))

    @pl.when(step == n_steps - 1)
    def _():
        @pl.when(n_steps > 1)
        def _():
            wait_runs(1 - slot)
        wait_runs(slot)


def _dispatch(runs, tails, n_real, x2, lrank, n_rows):
    bsz, t, d = x2.shape
    tm = ROW_TILE
    w = d // 2
    return pl.pallas_call(
        _dispatch_kernel,
        grid=(bsz, t // tm),
        in_specs=[pl.BlockSpec((None, None, 3, N_RUNS), lambda b, j: (b, j, 0, 0), memory_space=pltpu.SMEM),
                  pl.BlockSpec(memory_space=pltpu.SMEM), pl.BlockSpec(memory_space=pltpu.SMEM),
                  pl.BlockSpec((None, tm, d), lambda b, j: (b, j, 0)),
                  pl.BlockSpec((None, tm, LANE), lambda b, j: (b, j, 0))],
        out_specs=pl.BlockSpec(memory_space=pl.ANY),
        out_shape=jax.ShapeDtypeStruct((n_rows, w), jnp.uint32),
        scratch_shapes=[pltpu.VMEM((2, LOCAL_ROWS, w), jnp.uint32), pltpu.SemaphoreType.DMA((2,))],
        compiler_params=_cparams(("arbitrary", "arbitrary")),
    )(runs, tails, n_real, x2, lrank)


def _expert_kernel(be_ref, nreal_ref, xs_ref, w1_ref, b1_ref, w2_ref, b2_ref, y_ref):
    i = pl.program_id(0)
    half = w1_ref.shape[0] // 2

    @pl.when(i < nreal_ref[0])
    def _():
        lo, hi = _unpack_halves(xs_ref[...])
        hdn = (_dot(lo.astype(BF16), w1_ref[0:half, :]) + _dot(hi.astype(BF16), w1_ref[half:, :])) + b1_ref[...]
        acts = []
        for g in range(hdn.shape[1] // GLU_GROUP):
            glu = jnp.minimum(hdn[:, g * GLU_GROUP:g * GLU_GROUP + LANE], SWIGLU_LIMIT)
            lin = jnp.clip(hdn[:, g * GLU_GROUP + LANE:(g + 1) * GLU_GROUP], -SWIGLU_LIMIT, SWIGLU_LIMIT)
            acts.append((glu * _sigmoid(SWIGLU_ALPHA * glu) * (lin + 1.0)).astype(BF16))
        y_ref[...] = _pack_halves(_dot(jnp.concatenate(acts, axis=1), w2_ref[...].astype(BF16)) + b2_ref[...])

    @pl.when(i >= nreal_ref[0])
    def _():
        y_ref[...] = jnp.zeros_like(y_ref)


def _experts(block_expert, n_real, xs, w1, b1, expert_w2, b2, l):
    n_blocks = block_expert.shape[0]
    f, d = expert_w2.shape[2:]
    eb = EXPERT_BLOCK
    wspec = lambda k, n: pl.BlockSpec((None, k, n), lambda i, be, nr: (be[i], 0, 0))
    w2spec = pl.BlockSpec((None, None, f, d), lambda i, be, nr: (l, be[i], 0, 0))
    gs = pltpu.PrefetchScalarGridSpec(
        num_scalar_prefetch=2,
        grid=(n_blocks,),
        in_specs=[pl.BlockSpec((eb, d // 2), lambda i, be, nr: (jnp.minimum(i, nr[0] - 1), 0)),
                  wspec(d, 2 * f), wspec(1, 2 * f), w2spec, wspec(1, d)],
        out_specs=pl.BlockSpec((eb, d // 2), lambda i, be, nr: (i, 0)),
    )
    return pl.pallas_call(
        _expert_kernel,
        grid_spec=gs,
        out_shape=jax.ShapeDtypeStruct((n_blocks * eb, d // 2), jnp.uint32),
        compiler_params=_cparams(("arbitrary",)),
    )(block_expert, n_real, xs, w1, b1, expert_w2, b2)


def _combine_kernel(final, run_ref, nrun_ref, lrank_ref, y_hbm, h_ref, mod_ref, tg_ref, fg_ref, o_ref, loc, sem):
    tm = h_ref.shape[0]
    n_loc = LOCAL_ROWS
    step = pl.program_id(0) * pl.num_programs(1) + pl.program_id(1)
    n_steps = pl.num_programs(0) * pl.num_programs(1)
    slot = step % 2

    def fetch(table_ref, sl):
        _run_copies(table_ref, lambda s, d, n: pltpu.make_async_copy(
            y_hbm.at[pl.ds(d, n)], loc.at[sl, pl.ds(s, n)], sem.at[sl]))

    @pl.when(step == 0)
    def _():
        fetch(run_ref, slot)

    @pl.when(step + 1 < n_steps)
    def _():
        fetch(nrun_ref, 1 - slot)

    lr = lrank_ref[...]
    tg = tg_ref[...]
    cols = lax.broadcasted_iota(jnp.int32, (tm, n_loc), 1)
    wsel = jnp.zeros((tm, n_loc), F32)
    for kk in range(TOP_K):
        wsel = jnp.where(cols == lr[:, kk:kk + 1], tg[:, kk:kk + 1], wsel)
    wsel = wsel.astype(BF16)
    pltpu.make_async_copy(y_hbm.at[pl.ds(0, n_loc)], loc.at[slot], sem.at[slot]).wait()
    lo, hi = _unpack_halves(loc[slot])
    y = jnp.concatenate([_dot(wsel, lo.astype(BF16)), _dot(wsel, hi.astype(BF16))], axis=1)
    hn = h_ref[...] + mod_ref[5:6, :] * y
    if final:
        hn = _rms(hn, fg_ref[...])
    o_ref[...] = hn


def _combine(runs, lrank, y_rows, h, mod, tg, fg, final, j0):
    bsz, t, d = h.shape
    tm = ROW_TILE
    nj = t // tm
    row = lambda w: pl.BlockSpec((None, tm, w), lambda b, j: (b, j, 0))
    run_spec = lambda imap: pl.BlockSpec((None, None, 3, N_RUNS), imap, memory_space=pltpu.SMEM)
    next_tile = lambda b, j: (jnp.where(j + 1 < nj, b, jnp.minimum(b + 1, bsz - 1)), jnp.where(j + 1 < nj, j + 1, 0),
                              0, 0)
    return pl.pallas_call(
        functools.partial(_combine_kernel, final),
        grid=(bsz, nj),
        in_specs=[run_spec(lambda b, j: (b, j, 0, 0)), run_spec(next_tile),
                  row(LANE), pl.BlockSpec(memory_space=pl.ANY),
                  row(d), pl.BlockSpec((None, None, 6, d), lambda b, j: (b, jnp.minimum(j + j0, 1), 0, 0)),
                  row(LANE), pl.BlockSpec(fg.shape, lambda b, j: (0, 0))],
        out_specs=row(d),
        out_shape=jax.ShapeDtypeStruct((bsz, t, d), F32),
        scratch_shapes=[pltpu.VMEM((2, LOCAL_ROWS, d // 2), jnp.uint32), pltpu.SemaphoreType.DMA((2,))],
        compiler_params=_cparams(("arbitrary", "arbitrary")),
    )(runs, runs, lrank, y_rows, h, mod, tg, fg)


def _route(info, cnt, n_assign):
    eb = EXPERT_BLOCK
    n_tiles = info.shape[0] * info.shape[1]
    max_rows = n_assign + n_tiles * N_EXPERTS * (RUN_ALIGN - 1)
    filler_blocks = -(-LOCAL_ROWS // eb)
    n_blocks = -(-max_rows // eb) + N_EXPERTS + 2 * filler_blocks
    tile_parity = (jnp.arange(n_tiles, dtype=jnp.int32) % 2).reshape(info.shape[0], info.shape[1], 1)
    filler_row = (n_blocks - 2 * filler_blocks + tile_parity * filler_blocks) * eb
    counts = cnt[0, :N_EXPERTS].astype(jnp.int32)
    padded = (counts + eb - 1) // eb * eb
    pad_end = jnp.cumsum(padded)
    pad_start = pad_end - padded
    tcnt, tstart, tcarry = (info[:, :, r, :N_EXPERTS] for r in range(3))
    used = jnp.sum(tcnt, axis=-1, keepdims=True)
    runs = jnp.stack([jnp.concatenate([tcnt, LOCAL_ROWS - used], axis=-1),
                      jnp.concatenate([tstart, used], axis=-1),
                      jnp.concatenate([pad_start + tcarry, filler_row], axis=-1)],
                     axis=2).astype(jnp.int32)
    block_row = jnp.arange(n_blocks, dtype=jnp.int32) * eb
    block_expert = jnp.minimum(jnp.sum((pad_end[None, :] <= block_row[:, None]).astype(jnp.int32), axis=1),
                               N_EXPERTS - 1)
    n_real = (pad_end[-1] // eb).astype(jnp.int32).reshape(1)
    tails = jnp.stack([pad_start + counts, padded - counts]).astype(jnp.int32)
    return runs, tails, block_expert, n_real, n_blocks


def _rope_tables(t_lat, t_ctx):
    rows = t_lat // GRID_W
    row = jnp.repeat(jnp.arange(rows, dtype=F32), GRID_W)
    col = jnp.tile(jnp.arange(GRID_W, dtype=F32), rows)
    axis_dims = MLA_ROPE // 2
    inv = ROPE_BASE ** (-jnp.arange(0, axis_dims, 2, dtype=F32) / axis_dims)
    ar, ac = row[:, None] * inv, col[:, None] * inv
    cos64 = jnp.concatenate([jnp.cos(ar), jnp.cos(ar), jnp.cos(ac), jnp.cos(ac)], axis=-1)
    sin64 = jnp.concatenate([-jnp.sin(ar), jnp.sin(ar), -jnp.sin(ac), jnp.sin(ac)], axis=-1)
    pad = jnp.zeros((t_lat, LANE - MLA_ROPE), F32)
    cos_l = jnp.concatenate([cos64, pad], axis=-1)
    sin_l = jnp.concatenate([sin64, pad], axis=-1)
    cos_c = jnp.concatenate([jnp.ones((t_ctx, MLA_ROPE), F32), jnp.zeros((t_ctx, LANE - MLA_ROPE), F32)], axis=-1)
    sin_c = jnp.zeros((t_ctx, LANE), F32)
    return jnp.concatenate([cos_c, cos_l], axis=0), jnp.concatenate([sin_c, sin_l], axis=0)


def _rope_swap_perm():
    half = MLA_ROPE // 4
    perm = []
    for a in range(2):
        base = a * 2 * half
        perm += list(range(base + half, base + 2 * half)) + list(range(base, base + half))
    return jnp.array(perm, jnp.int32)


def _pad_cols(w, width):
    return jnp.concatenate([w, jnp.zeros(w.shape[:-1] + (width - w.shape[-1],), w.dtype)], axis=-1)


def _layer_weights(l, w_in, dn_a_log, dn_dt_bias, mla_w_qb, router_w, router_b, expert_w1, expert_b1):
    nqk = DN_HEADS * DN_DK
    wi = w_in[l]
    o = 0
    dn_qkv = wi[:, o:o + QKV_W]; o += QKV_W
    dn_z = wi[:, o:o + Z_W]; o += Z_W
    dn_ab = wi[:, o:o + 4 * DN_HEADS]; o += 4 * DN_HEADS
    sc3 = wi[:, o:o + SC3_W]; o += SC3_W
    qa = wi[:, o:o + MLA_Q_LORA]; o += MLA_Q_LORA
    ckv = wi[:, o:o + MLA_KV_LORA]; o += MLA_KV_LORA
    kr = wi[:, o:o + MLA_ROPE]
    perm = _rope_swap_perm()
    w_cat = jnp.concatenate([qa, ckv, _pad_cols(kr, LANE), _pad_cols(kr[:, perm], LANE), dn_qkv, dn_z, sc3,
                             _pad_cols(dn_ab, LANE)], axis=-1).astype(BF16)
    dq = MLA_NOPE + MLA_ROPE
    wq = mla_w_qb[l].reshape(MLA_Q_LORA, MLA_HEADS, dq)
    wq_rope = wq[:, :, MLA_NOPE:]
    wq_cat = jnp.concatenate([wq[:, :, :MLA_NOPE], _pad_cols(wq_rope, LANE), _pad_cols(wq_rope[:, :, perm], LANE)],
                             axis=-1).reshape(MLA_Q_LORA, MLA_HEADS * QH_W).astype(BF16)
    alog_row = _pad_cols(jnp.concatenate([dn_a_log[l].reshape(1, -1), jnp.zeros((1, 2 * DN_HEADS), F32)], -1), LANE)
    dtb_row = _pad_cols(jnp.concatenate([dn_dt_bias[l].reshape(1, -1), jnp.zeros((1, 2 * DN_HEADS), F32)], -1), LANE)
    rw = _pad_cols(router_w[l], LANE)
    rw_hi = rw.astype(BF16)
    rw_lo = (rw - rw_hi.astype(F32)).astype(BF16)
    rw_cat = jnp.concatenate([rw_hi, rw_lo], axis=-1)
    rb = jnp.concatenate([router_b[l], jnp.full((LANE - N_EXPERTS,), 0.0, F32)]).reshape(1, LANE)
    w1 = _deinterleave_w1(expert_w1, l)
    n_e, f2 = expert_b1.shape[1:]
    b1 = expert_b1[l].reshape(n_e, f2 // GLU_GROUP, LANE, 2).transpose(0, 1, 3, 2).reshape(n_e, 1, f2)
    return w_cat, wq_cat, alog_row, dtb_row, rw_cat, rw_hi, rb, w1, b1


def kernel(x, c, ctx, c_ctx, ada_w, ada_b, norm1_g, norm2_g, w_in, dn_conv_w, dn_a_log, dn_dt_bias, dn_norm_g, sc_conv_w, mla_q_norm_g, mla_w_qb, mla_kv_norm_g, mla_w_kvb, w_branch_gate, b_branch_gate, w_branch_dn, w_branch_sc, w_branch_mla, w_out, router_w, router_b, expert_w1, expert_b1, expert_w2, expert_b2, final_norm_g):
    bsz, t_lat, d = x.shape
    t_ctx = ctx.shape[1]
    depth = ada_w.shape[0]
    assert t_ctx == ROW_TILE and t_lat % ROW_TILE == 0 and t_lat % GRID_W == 0
    t = t_ctx + t_lat

    cos_t, sin_t = _rope_tables(t_lat, t_ctx)
    n_mod_rows = -(-(bsz + 1) // 8) * 8
    cvec = jnp.concatenate([c, c_ctx[None], jnp.zeros((n_mod_rows - bsz - 1, d), F32)], axis=0)
    mods = _ada_mods(cvec, ada_w, ada_b)

    h = jnp.concatenate([ctx, x], axis=1)
    row2 = lambda v: v.reshape(1, -1)
    for l in range(depth):
        mod_lat = mods[l, :bsz].reshape(bsz, 1, 6, d)
        mod_ctx = jnp.broadcast_to(mods[l, bsz].reshape(1, 1, 6, d), (bsz, 1, 6, d))
        mod = jnp.concatenate([mod_ctx, mod_lat], axis=1)
        (w_cat, wq_cat, alog_row, dtb_row, rw_cat, rw_hi, rb, w1, b1) = _layer_weights(
            l, w_in, dn_a_log, dn_dt_bias, mla_w_qb, router_w, router_b, expert_w1, expert_b1)

        qkv, z, sc, ab, q, k, v = _inproj(h, mod, row2(norm1_g[l]), w_cat, row2(mla_q_norm_g[l]), wq_cat,
                                          row2(mla_kv_norm_g[l]), mla_w_kvb[l].astype(BF16), cos_t, sin_t)
        qkvn, ysc, gb = _prep(qkv, sc, ab, dn_conv_w[l], sc_conv_w[l], alog_row, dtb_row)
        o_f, o_b = _deltanet(qkvn, gb, t_ctx // DN_CHUNK)
        ymla = _mla(q, k, v)
        last = l == depth - 1
        j0 = 1 if last else 0
        hn, xm2, tg, lrank, info, cnt = _merge(h, mod, row2(norm1_g[l]), o_f, o_b, z, row2(dn_norm_g[l]), ysc, ymla,
                                 w_branch_gate[l].astype(BF16), row2(b_branch_gate[l]),
                                 w_branch_dn[l].astype(BF16), w_branch_sc[l].astype(BF16),
                                 w_branch_mla[l].astype(BF16), w_out[l].astype(BF16), row2(norm2_g[l]),
                                 rw_cat, rw_hi, rb, j0)
        runs, tails, block_expert, n_real, n_blocks = _route(info, cnt, bsz * (t - j0 * ROW_TILE) * TOP_K)
        xs = _dispatch(runs, tails, n_real, xm2, lrank, n_blocks * EXPERT_BLOCK)
        y_rows = _experts(block_expert, n_real, xs, w1, b1, expert_w2, expert_b2[l][:, None, :], l)
        h = _combine(runs, lrank, y_rows, hn, mod, tg, row2(final_norm_g), last, j0)
    return h
```

```python
import functools
import math

import jax
import jax.numpy as jnp
from jax import lax
from jax.experimental import pallas as pl
from jax.experimental.pallas import tpu as pltpu

F32 = jnp.float32
BF16 = jnp.bfloat16

GRID_W = 64
NORM_EPS = 1e-6
DN_HEADS = 4
DN_DK = 128
DN_DV = 128
DN_CHUNK = 64
SC_WIDTH = 512
MLA_HEADS = 4
MLA_Q_LORA = 256
MLA_KV_LORA = 128
MLA_NOPE = 128
MLA_ROPE = 64
MLA_V = 128
MLA_SCALE = (MLA_NOPE + MLA_ROPE) ** -0.5
Q_SCALE = MLA_SCALE * math.log2(math.e)
ROPE_BASE = 10000.0
N_EXPERTS = 32
TOP_K = 4
EXPERT_FF = 1024
SWIGLU_ALPHA = 1.702
SWIGLU_LIMIT = 7.0

LANE = 128
ROW_TILE = 256
HALO = 16
EXPERT_BLOCK = 512
PROJ_BATCH = 2
MERGE_BATCH = 2
DN_BATCH = 4
RUN_ALIGN = 8
LOCAL_ROWS = TOP_K * ROW_TILE + N_EXPERTS * RUN_ALIGN
N_RUNS = N_EXPERTS + 1
VMEM_LIMIT = 56 * 1024 * 1024

QKV_W = 3 * DN_HEADS * DN_DK
Z_W = DN_HEADS * DN_DV
SC3_W = 3 * SC_WIDTH
KVA_W = 3 * LANE
AB_W = LANE
IN_W = QKV_W + Z_W + SC3_W + MLA_Q_LORA + KVA_W + AB_W
QH_W = 3 * LANE
QK_W = 2 * LANE


def _cparams(sem):
    return pltpu.CompilerParams(dimension_semantics=sem, vmem_limit_bytes=VMEM_LIMIT)


def _dot(a, b):
    return jnp.dot(a, b, preferred_element_type=F32)


def _dot_nt(a, b):
    return lax.dot_general(a, b, (((1,), (1,)), ((), ())), preferred_element_type=F32)


def _rms(x, g):
    return x * lax.rsqrt(jnp.mean(x * x, axis=-1, keepdims=True) + NORM_EPS) * g


def _sigmoid(x):
    return 1.0 / (1.0 + jnp.exp(-x))


def _ada_kernel(c_ref, w_ref, b_ref, o_ref):
    cv = c_ref[...]
    s = cv * _sigmoid(cv)
    o_ref[...] = _dot(s.astype(BF16), w_ref[...].astype(BF16)) + b_ref[...]


def _ada_mods(cvec, ada_w, ada_b):
    n_layers, d, d6 = ada_w.shape
    r = cvec.shape[0]
    tn = 512
    return pl.pallas_call(
        _ada_kernel,
        grid=(n_layers, d6 // tn),
        in_specs=[pl.BlockSpec((r, d), lambda l, n: (0, 0)),
                  pl.BlockSpec((None, d, tn), lambda l, n: (l, 0, n)),
                  pl.BlockSpec((None, 1, tn), lambda l, n: (l, 0, n))],
        out_specs=pl.BlockSpec((None, r, tn), lambda l, n: (l, 0, n)),
        out_shape=jax.ShapeDtypeStruct((n_layers, r, d6), F32),
        compiler_params=_cparams(("arbitrary", "arbitrary")),
    )(cvec, ada_w, ada_b.reshape(n_layers, 1, d6))


def _inproj_kernel(split, *refs):
    if split:
        x_ref, ctx_ref, *refs, hout_ref = refs
    else:
        h_ref, *refs = refs
    (mod_ref, g1_ref, w_ref, gq_ref, wq_ref, gkv_ref, wkv_ref, cos_ref, sin_ref,
     qkv_ref, z_ref, sc_ref, ab_ref, q_ref, k_ref, v_ref) = refs
    nb, tm = qkv_ref.shape[0], qkv_ref.shape[1]
    if split:
        is_ctx = pl.program_id(1) == 0
        tiles = [jnp.where(is_ctx, ctx_ref[bb], x_ref[bb]) for bb in range(nb)]
        for bb in range(nb):
            hout_ref[bb] = tiles[bb]
    else:
        tiles = [h_ref[bb] for bb in range(nb)]
    xm = jnp.concatenate([_rms(tiles[bb], g1_ref[...]) * (1.0 + mod_ref[bb, 1:2, :]) + mod_ref[bb, 0:1, :]
                          for bb in range(nb)], axis=0).astype(BF16)
    p = _dot(xm, w_ref[...])
    rows = lambda bb: slice(bb * tm, (bb + 1) * tm)
    o = 0
    qa = p[:, o:o + MLA_Q_LORA]
    o += MLA_Q_LORA
    ckv = p[:, o:o + LANE]
    kr = p[:, o + LANE:o + 2 * LANE]
    krs = p[:, o + 2 * LANE:o + 3 * LANE]
    o += KVA_W
    for bb in range(nb):
        qkv_ref[bb] = p[rows(bb), o:o + QKV_W].astype(BF16)
        z_ref[bb] = p[rows(bb), o + QKV_W:o + QKV_W + Z_W].astype(BF16)
        sc_ref[bb] = p[rows(bb), o + QKV_W + Z_W:o + QKV_W + Z_W + SC3_W].astype(BF16)
        ab_ref[bb] = p[rows(bb), o + QKV_W + Z_W + SC3_W:o + QKV_W + Z_W + SC3_W + AB_W]

    cos = jnp.concatenate([cos_ref[...]] * nb, axis=0)
    sin = jnp.concatenate([sin_ref[...]] * nb, axis=0)
    k_rope = (kr * cos + krs * sin).astype(BF16)
    qn = _rms(qa, gq_ref[...]).astype(BF16)
    qf = _dot(qn, wq_ref[...])
    kvn = _rms(ckv, gkv_ref[...]).astype(BF16)
    kv = _dot(kvn, wkv_ref[...])
    for hh in range(MLA_HEADS):
        b0 = hh * QH_W
        q_nope = (qf[:, b0:b0 + LANE] * Q_SCALE).astype(BF16)
        q_rope = qf[:, b0 + LANE:b0 + 2 * LANE] * cos + qf[:, b0 + 2 * LANE:b0 + 3 * LANE] * sin
        q_rope = (q_rope * Q_SCALE).astype(BF16)
        c0 = hh * (MLA_NOPE + MLA_V)
        k_nope = kv[:, c0:c0 + MLA_NOPE].astype(BF16)
        vv = kv[:, c0 + MLA_NOPE:c0 + MLA_NOPE + MLA_V].astype(BF16)
        for bb in range(nb):
            q_ref[bb, hh, :, 0:LANE] = q_nope[rows(bb)]
            q_ref[bb, hh, :, LANE:2 * LANE] = q_rope[rows(bb)]
            k_ref[bb, hh, :, 0:LANE] = k_nope[rows(bb)]
            k_ref[bb, hh, :, LANE:2 * LANE] = k_rope[rows(bb)]
            v_ref[bb, hh] = vv[rows(bb)]


def _inproj(h, mod, g1, w_cat, gq, wq_cat, gkv, wkv, cos_t, sin_t):
    split = isinstance(h, tuple)
    tm = ROW_TILE
    if split:
        x, ctx = h
        bsz, t_lat, d = x.shape
        t = t_lat + ctx.shape[1]
    else:
        bsz, t, d = h.shape
    nj = t // tm
    nb = PROJ_BATCH if bsz % PROJ_BATCH == 0 else 1
    row = lambda w: pl.BlockSpec((nb, tm, w), lambda b, j: (b, j, 0))
    if split:
        h_specs = [pl.BlockSpec((nb, tm, d), lambda b, j: (b, jnp.maximum(j - 1, 0), 0)),
                   pl.BlockSpec((nb, tm, d), lambda b, j: (b, 0, 0))]
        h_args = (x, ctx)
    else:
        h_specs, h_args = [row(d)], (h,)
    full = lambda a: pl.BlockSpec(a.shape, lambda b, j: (0,) * a.ndim, pipeline_mode=pl.Buffered(1))
    head = lambda w: pl.BlockSpec((nb, MLA_HEADS, tm, w), lambda b, j: (b, 0, j, 0))
    sds = jax.ShapeDtypeStruct
    return pl.pallas_call(
        functools.partial(_inproj_kernel, split),
        grid=(bsz // nb, nj),
        in_specs=h_specs + [
            pl.BlockSpec((nb, None, 6, d), lambda b, j: (b, jnp.minimum(j, 1), 0, 0)),
            full(g1), full(w_cat), full(gq), full(wq_cat), full(gkv), full(wkv),
            pl.BlockSpec((tm, LANE), lambda b, j: (j, 0)),
            pl.BlockSpec((tm, LANE), lambda b, j: (j, 0))],
        out_specs=[row(QKV_W), row(Z_W), row(SC3_W), row(AB_W), head(QK_W), head(QK_W), head(MLA_V)]
        + ([row(d)] if split else []),
        out_shape=[sds((bsz, t, QKV_W), BF16), sds((bsz, t, Z_W), BF16), sds((bsz, t, SC3_W), BF16),
                   sds((bsz, t, AB_W), F32),
                   sds((bsz, MLA_HEADS, t, QK_W), BF16), sds((bsz, MLA_HEADS, t, QK_W), BF16),
                   sds((bsz, MLA_HEADS, t, MLA_V), BF16)] + ([sds((bsz, t, d), F32)] if split else []),
        compiler_params=_cparams(("parallel", "arbitrary")),
    )(*h_args, mod, g1, w_cat, gq, wq_cat, gkv, wkv, cos_t, sin_t)


def _shift_rows(x, prev_row, next_row):
    tm = x.shape[0]
    rid = lax.broadcasted_iota(jnp.int32, x.shape, 0)
    xp = jnp.where(rid == 0, prev_row, pltpu.roll(x, 1, 0))
    xn = jnp.where(rid == tm - 1, next_row, pltpu.roll(x, tm - 1, 0))
    return xp, xn


def _prep_kernel(qkv_ref, qkv_p_ref, qkv_n_ref, sc_ref, sc_p_ref, sc_n_ref, ab_ref,
                 dnw_ref, scw_ref, alog_ref, dtb_ref, qkvn_ref, ysc_ref, gb_ref):
    j = pl.program_id(1)
    nj = pl.num_programs(1)
    pv = (j >= 2).astype(F32)
    nv = jnp.logical_and(j >= 1, j <= nj - 2).astype(F32)

    x = qkv_ref[...].astype(F32)
    xp, xn = _shift_rows(x, qkv_p_ref[HALO - 1:HALO, :].astype(F32) * pv, qkv_n_ref[0:1, :].astype(F32) * nv)
    y = xp * dnw_ref[0:1, :] + x * dnw_ref[1:2, :] + xn * dnw_ref[2:3, :]
    y = y * _sigmoid(y)
    nqk = DN_HEADS * DN_DK
    for g in range(2 * DN_HEADS):
        yy = y[:, g * DN_DK:(g + 1) * DN_DK]
        yy = yy * lax.rsqrt(jnp.sum(yy * yy, axis=-1, keepdims=True) + 1e-6)
        if g < DN_HEADS:
            yy = yy * (DN_DK ** -0.5)
        qkvn_ref[:, g * DN_DK:(g + 1) * DN_DK] = yy.astype(BF16)
    qkvn_ref[:, 2 * nqk:] = y[:, 2 * nqk:].astype(BF16)

    s = sc_ref[...].astype(F32)
    sp = sc_p_ref[HALO - 1:HALO, :].astype(F32) * pv
    sn = sc_n_ref[0:1, :].astype(F32) * nv
    w = SC_WIDTH
    u = s[:, 2 * w:3 * w] * s[:, 0:w]
    up, un = _shift_rows(u, sp[:, 2 * w:3 * w] * sp[:, 0:w], sn[:, 2 * w:3 * w] * sn[:, 0:w])
    conv = up * scw_ref[0:1, :] + u * scw_ref[1:2, :] + un * scw_ref[2:3, :]
    ysc_ref[...] = (s[:, w:2 * w] * conv).astype(BF16)

    ab = ab_ref[...]
    sp_arg = ab + dtb_ref[...]
    softplus = jnp.maximum(sp_arg, 0.0) + jnp.log1p(jnp.exp(-jnp.abs(sp_arg)))
    gval = -jnp.exp(alog_ref[...]) * softplus
    lane = lax.broadcasted_iota(jnp.int32, ab.shape, 1)
    gb_ref[...] = jnp.where(lane < 2 * DN_HEADS, gval, _sigmoid(ab))


def _prep(qkv, sc, ab, dn_conv_w, sc_conv_w, alog_row, dtb_row):
    bsz, t, _ = qkv.shape
    tm = ROW_TILE
    nj = t // tm
    hb = tm // HALO
    nh = t // HALO
    row = lambda w: pl.BlockSpec((None, tm, w), lambda b, j: (b, j, 0))
    prev = lambda w: pl.BlockSpec((None, HALO, w), lambda b, j: (b, jnp.maximum(j * hb - 1, 0), 0))
    nxt = lambda w: pl.BlockSpec((None, HALO, w), lambda b, j: (b, jnp.minimum((j + 1) * hb, nh - 1), 0))
    full = lambda a: pl.BlockSpec(a.shape, lambda b, j: (0,) * a.ndim)
    sds = jax.ShapeDtypeStruct
    return pl.pallas_call(
        _prep_kernel,
        grid=(bsz, nj),
        in_specs=[row(QKV_W), prev(QKV_W), nxt(QKV_W), row(SC3_W), prev(SC3_W), nxt(SC3_W), row(AB_W),
                  full(dn_conv_w), full(sc_conv_w), full(alog_row), full(dtb_row)],
        out_specs=[row(QKV_W), row(SC_WIDTH), row(AB_W)],
        out_shape=[sds((bsz, t, QKV_W), BF16), sds((bsz, t, SC_WIDTH), BF16), sds((bsz, t, AB_W), F32)],
        compiler_params=_cparams(("parallel", "arbitrary")),
    )(qkv, qkv, qkv, sc, sc, sc, ab, dn_conv_w, sc_conv_w, alog_row, dtb_row)


DN_GROUP = 2


def _stack_heads(x, h0):
    return jnp.concatenate([x[:, (h0 + i) * LANE:(h0 + i + 1) * LANE] for i in range(DN_GROUP)], axis=0)


def _stack_cols(x, c0):
    return jnp.concatenate([x[:, c0 + i:c0 + i + 1] for i in range(DN_GROUP)], axis=0)


def _dn_direction(d, h0, x, gbt, tri, m_causal, m_strict, eye, s_ref):
    c = DN_CHUNK
    grp = range(DN_GROUP)
    nqk = DN_HEADS * DN_DK
    qst = _stack_heads(x[:, 0:nqk], h0)
    kst = _stack_heads(x[:, nqk:2 * nqk], h0)
    vst = _stack_heads(x[:, 2 * nqk:], h0)
    gc_all = jnp.dot(tri, gbt, preferred_element_type=F32, precision=lax.Precision.HIGHEST)
    col0 = d * DN_HEADS + h0
    gc = _stack_cols(gc_all, col0)
    beta = _stack_cols(gbt, 2 * DN_HEADS + col0)
    last = c - 1 if d == 0 else 0
    gl_heads = [gc_all[last:last + 1, col0 + i:col0 + i + 1] for i in grp]
    gl = jnp.concatenate([jnp.broadcast_to(g1, (c, 1)) for g1 in gl_heads], axis=0)
    kf = kst.astype(F32)
    kb = kf * beta
    kk = _dot_nt(kb.astype(BF16), kst)
    qk = _dot_nt(qst, kst)
    yield

    hc = DN_GROUP * c
    gmat = jnp.broadcast_to(gc, (hc, hc))
    dec = jnp.exp(jnp.minimum(gmat - gmat.T, 0.0))
    xpow = -(kk * jnp.where(m_strict > 0.5, dec, 0.0))
    tinv = eye + xpow
    intra = (qk * jnp.where(m_causal > 0.5, dec, 0.0)).astype(BF16)
    eg = jnp.exp(gc)
    rhs = jnp.concatenate([vst.astype(F32) * beta, kb * eg], axis=1).astype(BF16)
    qd = qst.astype(F32) * eg
    kdt = (kf * jnp.exp(gl - gc)).T.astype(BF16)
    for _ in range(int(math.log2(c)) - 1):
        xb = xpow.astype(BF16)
        xpow = _dot(xb, xb)
        yield
        tinv = tinv + _dot(tinv.astype(BF16), xpow.astype(BF16))
        yield
    sol = _dot(tinv.astype(BF16), rhs)
    yield
    u = sol[:, 0:DN_DV]
    w = sol[:, DN_DV:]

    s_cat = s_ref[...]
    wq = jnp.concatenate([w, qd], axis=0).astype(BF16)
    full = _dot(wq, s_cat.astype(BF16))
    yield
    ws = jnp.concatenate([full[i * c:(i + 1) * c, i * DN_DV:(i + 1) * DN_DV] for i in grp], axis=0)
    qs = jnp.concatenate([full[hc + i * c:hc + (i + 1) * c, i * DN_DV:(i + 1) * DN_DV] for i in grp], axis=0)
    vnew = u - ws
    o = qs + _dot(intra, vnew.astype(BF16))
    rhead = lax.broadcasted_iota(jnp.int32, (hc, DN_DV), 0) // c
    vblk = jnp.concatenate([jnp.where(rhead == i, vnew, 0.0) for i in grp], axis=1).astype(BF16)
    gt = jnp.concatenate([jnp.broadcast_to(jnp.exp(g1), (1, DN_DV)) for g1 in gl_heads], axis=1)
    s_ref[...] = s_cat * gt + _dot(kdt, vblk)
    yield
    return jnp.concatenate([o[i * c:(i + 1) * c] for i in grp], axis=1)


def _lockstep(gens):
    outs = [None] * len(gens)
    live = list(range(len(gens)))
    while live:
        for i in list(live):
            try:
                next(gens[i])
            except StopIteration as done:
                outs[i] = done.value
                live.remove(i)
    return outs


def _dn_kernel(xf_ref, gf_ref, xb_ref, gb_ref, trif_ref, trib_ref, mcf_ref, msf_ref, mcb_ref, msb_ref, eye_ref,
               of_ref, ob_ref, s_ref):
    @pl.when(pl.program_id(1) == 0)
    def _():
        s_ref[...] = jnp.zeros_like(s_ref)

    eye = eye_ref[...]
    gw = DN_GROUP * DN_DV
    gens, dsts = [], []
    for bb in range(xf_ref.shape[0]):
        for g in range(DN_HEADS // DN_GROUP):
            gens.append(_dn_direction(0, g * DN_GROUP, xf_ref[bb], gf_ref[bb], trif_ref[...], mcf_ref[...],
                                      msf_ref[...], eye, s_ref.at[bb, 0, :, g * gw:(g + 1) * gw]))
            dsts.append((of_ref, bb, g))
            gens.append(_dn_direction(1, g * DN_GROUP, xb_ref[bb], gb_ref[bb], trib_ref[...], mcb_ref[...],
                                      msb_ref[...], eye, s_ref.at[bb, 1, :, g * gw:(g + 1) * gw]))
            dsts.append((ob_ref, bb, g))
    for (o_ref, bb, g), o in zip(dsts, _lockstep(gens)):
        o_ref[bb, :, g * gw:(g + 1) * gw] = o.astype(BF16)


def _deltanet(qkvn, gb, n_ctx_chunks):
    bsz, t, _ = qkvn.shape
    c = DN_CHUNK
    nc = t // c
    hc = DN_GROUP * c
    bmap = lambda i: jnp.where(i < n_ctx_chunks, n_ctx_chunks - 1 - i, nc + n_ctx_chunks - 1 - i)
    ii = jnp.arange(c)
    tri_f = (ii[None, :] <= ii[:, None]).astype(F32)
    tri_b = (ii[None, :] >= ii[:, None]).astype(F32)
    r = jnp.arange(hc)
    same = (r[:, None] // c) == (r[None, :] // c)
    pi, pj = r[:, None] % c, r[None, :] % c
    mc_f = (same & (pj <= pi)).astype(F32)
    ms_f = (same & (pj < pi)).astype(F32)
    mc_b = (same & (pj >= pi)).astype(F32)
    ms_b = (same & (pj > pi)).astype(F32)
    eye = jnp.eye(hc, dtype=F32)
    full = lambda a: pl.BlockSpec(a.shape, lambda b, i: (0,) * a.ndim)
    nb = DN_BATCH if bsz % DN_BATCH == 0 else 1
    fw = lambda w: pl.BlockSpec((nb, c, w), lambda b, i: (b, i, 0))
    bw = lambda w: pl.BlockSpec((nb, c, w), lambda b, i: (b, bmap(i), 0))
    ow = DN_HEADS * DN_DV
    sds = jax.ShapeDtypeStruct
    return pl.pallas_call(
        _dn_kernel,
        grid=(bsz // nb, nc),
        in_specs=[fw(QKV_W), fw(AB_W), bw(QKV_W), bw(AB_W), full(tri_f), full(tri_b),
                  full(mc_f), full(ms_f), full(mc_b), full(ms_b), full(eye)],
        out_specs=[fw(ow), bw(ow)],
        out_shape=[sds((bsz, t, ow), BF16), sds((bsz, t, ow), BF16)],
        scratch_shapes=[pltpu.VMEM((nb, 2, DN_DK, DN_HEADS * DN_DV), F32)],
        compiler_params=_cparams(("arbitrary", "arbitrary")),
    )(qkvn, gb, qkvn, gb, tri_f, tri_b, mc_f, ms_f, mc_b, ms_b, eye)


MLA_GROUP = 4


def _softmax_av(q, k, v):
    s = _dot_nt(q, k)
    yield
    m = jnp.max(s, axis=-1, keepdims=True)
    p = jnp.exp2(s - m)
    l = jnp.sum(p, axis=-1, keepdims=True)
    p = p.astype(BF16)
    yield
    return _dot(p, v) / l


def _mla_kernel(q_ref, k_ref, v_ref, o_ref):
    j = pl.program_id(2)
    tc = q_ref.shape[1]

    def attend(n_keys):
        outs = _lockstep([_softmax_av(q_ref[g], k_ref[g, 0:n_keys, :], v_ref[g, 0:n_keys, :])
                          for g in range(MLA_GROUP)])
        for g, o in enumerate(outs):
            o_ref[:, g * MLA_V:(g + 1) * MLA_V] = o.astype(BF16)

    @pl.when(j == 0)
    def _():
        attend(tc)

    @pl.when(j > 0)
    def _():
        attend(k_ref.shape[1])


def _mla(q, k, v):
    bsz, nh, t, _ = q.shape
    tq = ROW_TILE
    g = MLA_GROUP
    return pl.pallas_call(
        _mla_kernel,
        grid=(bsz, nh // g, t // tq),
        in_specs=[pl.BlockSpec((None, g, tq, QK_W), lambda b, hp, j: (b, hp, j, 0)),
                  pl.BlockSpec((None, g, t, QK_W), lambda b, hp, j: (b, hp, 0, 0), pipeline_mode=pl.Buffered(1)),
                  pl.BlockSpec((None, g, t, MLA_V), lambda b, hp, j: (b, hp, 0, 0), pipeline_mode=pl.Buffered(1))],
        out_specs=pl.BlockSpec((None, tq, g * MLA_V), lambda b, hp, j: (b, j, hp)),
        out_shape=jax.ShapeDtypeStruct((bsz, t, nh * MLA_V), BF16),
        compiler_params=_cparams(("parallel", "parallel", "arbitrary")),
    )(q, k, v)


def _pack_halves(x):
    w = x.shape[1] // 2
    lo = lax.bitcast_convert_type(x[:, :w].astype(BF16).astype(F32), jnp.uint32)
    hi = lax.bitcast_convert_type(x[:, w:].astype(BF16).astype(F32), jnp.uint32)
    return (hi & jnp.uint32(0xFFFF0000)) | (lo >> 16)


def _unpack_halves(p):
    lo = lax.bitcast_convert_type(p << 16, F32)
    hi = lax.bitcast_convert_type(p & jnp.uint32(0xFFFF0000), F32)
    return lo, hi


def _merge_kernel(h_ref, mod_ref, g1_ref, of_ref, ob_ref, z_ref, dng_ref, ysc_ref, ymla_ref,
                  wg_ref, bg_ref, wdn_ref, wsc_ref, wmla_ref, wout_ref, g2_ref, rw_ref, rwh_ref, rb_ref, tril_ref, triu_ref,
                  hn_ref, xm2_ref, tg_ref, lrank_ref, info_ref, cnt_ref, cnt_scr):
    @pl.when(jnp.logical_and(pl.program_id(0) == 0, pl.program_id(1) == 0))
    def _():
        cnt_scr[...] = jnp.zeros_like(cnt_scr)

    nb, tm, d = h_ref.shape
    rows = lambda bb: slice(bb * tm, (bb + 1) * tm)
    cat = lambda ref: jnp.concatenate([ref[bb] for bb in range(nb)], axis=0)
    xm = jnp.concatenate([_rms(h_ref[bb], g1_ref[...]) * (1.0 + mod_ref[bb, 1:2, :]) + mod_ref[bb, 0:1, :]
                          for bb in range(nb)], axis=0).astype(BF16)
    o = cat(of_ref).astype(F32) + cat(ob_ref).astype(F32)
    z = cat(z_ref).astype(F32)
    parts = []
    for hh in range(DN_HEADS):
        oh = o[:, hh * DN_DV:(hh + 1) * DN_DV]
        zh = z[:, hh * DN_DV:(hh + 1) * DN_DV]
        parts.append(_rms(oh, dng_ref[...]) * (zh * _sigmoid(zh)))
    ydn = jnp.concatenate(parts, axis=1).astype(BF16)
    gates = _sigmoid(_dot(xm, wg_ref[...]) + bg_ref[...])
    merged = (gates[:, 0:d] * _dot(ydn, wdn_ref[...]) + gates[:, d:2 * d] * _dot(cat(ysc_ref), wsc_ref[...])
              + gates[:, 2 * d:3 * d] * _dot(cat(ymla_ref), wmla_ref[...]))
    y = _dot(merged.astype(BF16), wout_ref[...])
    xm2s = []
    for bb in range(nb):
        hn = h_ref[bb] + mod_ref[bb, 2:3, :] * y[rows(bb)]
        hn_ref[bb] = hn
        xm2s.append(_rms(hn, g2_ref[...]) * (1.0 + mod_ref[bb, 4:5, :]) + mod_ref[bb, 3:4, :])
    xm2 = jnp.concatenate(xm2s, axis=0)
    xh = xm2.astype(BF16)
    for bb in range(nb):
        xm2_ref[bb] = xh[rows(bb)]

    xl = (xm2 - xh.astype(F32)).astype(BF16)
    a = _dot(xh, rw_ref[...])
    logits = a[:, 0:LANE] + a[:, LANE:2 * LANE] + _dot(xl, rwh_ref[...]) + rb_ref[...]
    lane = lax.broadcasted_iota(jnp.int32, logits.shape, 1)
    work = jnp.where(lane < N_EXPERTS, logits, -jnp.inf)
    tv = jnp.zeros(logits.shape, F32)
    vals, idxs = [], []
    for kk in range(TOP_K):
        m = jnp.max(work, axis=-1, keepdims=True)
        idx = jnp.min(jnp.where(work == m, lane, LANE), axis=-1, keepdims=True)
        work = jnp.where(lane == idx, -jnp.inf, work)
        vals.append(m)
        idxs.append(idx)
    es = [jnp.exp(vv - vals[0]) for vv in vals]
    tot = es[0] + es[1] + es[2] + es[3]
    for kk in range(TOP_K):
        tv = jnp.where(lane == kk, es[kk] / tot, tv)
    sels = [(lane == idx).astype(F32) for idx in idxs]
    sel = sels[0] + sels[1] + sels[2] + sels[3]

    lane_t = lax.broadcasted_iota(jnp.int32, (tm, LANE), 1)
    sub = lax.broadcasted_iota(jnp.int32, (8, LANE), 0)
    for bb in range(nb):
        tg_ref[bb] = tv[rows(bb)]
        sel_t = sel[rows(bb)]
        tcnt = jnp.floor((jnp.sum(sel_t, axis=0, keepdims=True) + (RUN_ALIGN - 1.0)) * (1.0 / RUN_ALIGN)) * RUN_ALIGN
        tstart = _dot(jnp.broadcast_to(tcnt, (8, LANE)).astype(BF16), triu_ref[...])[0:1]
        lpos = _dot(tril_ref[...], sel_t.astype(BF16)) + tstart
        lrank = jnp.zeros((tm, LANE), F32)
        for kk in range(TOP_K):
            lp = jnp.sum(sels[kk][rows(bb)] * lpos, axis=-1, keepdims=True)
            lrank = jnp.where(lane_t == kk, lp, lrank)
        lrank_ref[bb] = lrank.astype(jnp.int32)
        carry = cnt_scr[...]
        info = jnp.where(sub == 0, tcnt, jnp.where(sub == 1, tstart, jnp.where(sub == 2, carry, 0.0)))
        info_ref[bb] = info.astype(jnp.int32)
        cnt_scr[...] = carry + tcnt
    cnt_ref[...] = cnt_scr[...]


def _merge(h, mod, g1, o_f, o_b, z, dng, ysc, ymla, wg, bg, wdn, wsc, wmla, wout, g2, rw_cat, rw_hi, rb, j0):
    bsz, t, d = h.shape
    tm = ROW_TILE
    nj = t // tm - j0
    to = nj * tm
    nb = MERGE_BATCH if bsz % MERGE_BATCH == 0 else 1
    row = lambda w: pl.BlockSpec((nb, tm, w), lambda b, j: (b, j + j0, 0))
    orow = lambda w: pl.BlockSpec((nb, tm, w), lambda b, j: (b, j, 0))
    full = lambda a: pl.BlockSpec(a.shape, lambda b, j: (0,) * a.ndim, pipeline_mode=pl.Buffered(1))
    ow = DN_HEADS * DN_DV
    sds = jax.ShapeDtypeStruct
    ii = jnp.arange(tm)
    tril = (ii[None, :] < ii[:, None]).astype(BF16)
    ee = jnp.arange(LANE)
    triu = (ee[:, None] < ee[None, :]).astype(BF16)
    tile_i32 = pl.BlockSpec((nb, None, 8, LANE), lambda b, j: (b, j, 0, 0))
    return pl.pallas_call(
        _merge_kernel,
        grid=(bsz // nb, nj),
        in_specs=[row(d), pl.BlockSpec((nb, None, 6, d), lambda b, j: (b, jnp.minimum(j + j0, 1), 0, 0)), full(g1),
                  row(ow), row(ow), row(Z_W), full(dng), row(SC_WIDTH), row(MLA_HEADS * MLA_V),
                  full(wg), full(bg), full(wdn), full(wsc), full(wmla), full(wout), full(g2),
                  full(rw_cat), full(rw_hi), full(rb), full(tril), full(triu)],
        out_specs=[orow(d), orow(d), orow(LANE), orow(LANE), tile_i32,
                   pl.BlockSpec((1, LANE), lambda b, j: (0, 0))],
        out_shape=[sds((bsz, to, d), F32), sds((bsz, to, d), BF16), sds((bsz, to, LANE), F32),
                   sds((bsz, to, LANE), jnp.int32), sds((bsz, nj, 8, LANE), jnp.int32), sds((1, LANE), F32)],
        scratch_shapes=[pltpu.VMEM((1, LANE), F32)],
        compiler_params=_cparams(("arbitrary", "arbitrary")),
    )(h, mod, g1, o_f, o_b, z, dng, ysc, ymla, wg, bg, wdn, wsc, wmla, wout, g2, rw_cat, rw_hi, rb, tril, triu)


GLU_GROUP = 2 * LANE


def _deinterleave_kernel(w_ref, p_ref, o_ref):
    perm = p_ref[...]
    for g in range(w_ref.shape[1] // GLU_GROUP):
        sl = slice(g * GLU_GROUP, (g + 1) * GLU_GROUP)
        o_ref[:, sl] = _dot(w_ref[:, sl].astype(BF16), perm).astype(BF16)


def _deinterleave_w1(expert_w1, l):
    _, n_e, d, f2 = expert_w1.shape
    i = jnp.arange(GLU_GROUP)
    perm = (jnp.where(i % 2 == 0, i // 2, LANE + i // 2)[:, None] == i[None, :]).astype(BF16)
    return pl.pallas_call(
        _deinterleave_kernel,
        grid=(n_e,),
        in_specs=[pl.BlockSpec((None, None, d, f2), lambda e: (l, e, 0, 0)),
                  pl.BlockSpec(perm.shape, lambda e: (0, 0))],
        out_specs=pl.BlockSpec((None, d, f2), lambda e: (e, 0, 0)),
        out_shape=jax.ShapeDtypeStruct((n_e, d, f2), BF16),
        compiler_params=_cparams(("arbitrary",)),
    )(expert_w1, perm)


RUN_PIECES = tuple(RUN_ALIGN << i for i in range((LOCAL_ROWS // RUN_ALIGN).bit_length()))


def _run_copies(run_ref, make_copy):
    def per_run(e, carry):
        cnt, src, dst = run_ref[0, e], run_ref[1, e], run_ref[2, e]
        off = jnp.int32(0)
        for piece in RUN_PIECES:
            has = (cnt & piece) != 0

            @pl.when(has)
            def _(off=off, piece=piece):
                make_copy(pl.multiple_of(src + off, RUN_ALIGN), pl.multiple_of(dst + off, RUN_ALIGN), piece).start()
            off = off + jnp.where(has, piece, 0)
        return carry
    lax.fori_loop(0, N_RUNS, per_run, 0)


TAIL_PIECES = tuple(RUN_ALIGN << i for i in range((EXPERT_BLOCK // RUN_ALIGN).bit_length() - 1))


def _tail_fill(tail_ref, zeros_ref, xs_hbm, sem):
    def pieces(e, act):
        dst, cnt = tail_ref[0, e], tail_ref[1, e]
        off = jnp.int32(0)
        for piece in TAIL_PIECES:
            has = (cnt & piece) != 0

            @pl.when(has)
            def _(off=off, piece=piece):
                act(pltpu.make_async_copy(zeros_ref.at[pl.ds(0, piece)],
                                          xs_hbm.at[pl.ds(pl.multiple_of(dst + off, RUN_ALIGN), piece)], sem))
            off = off + jnp.where(has, piece, 0)

    def start(e, carry):
        pieces(e, lambda cp: cp.start())
        return carry

    def wait(e, carry):
        pieces(e, lambda cp: cp.wait())
        return carry

    lax.fori_loop(0, N_EXPERTS, start, 0)
    lax.fori_loop(0, N_EXPERTS, wait, 0)


def _unused_fill(nreal_ref, zeros_ref, xs_hbm, sem):
    eb = EXPERT_BLOCK
    n_blocks = xs_hbm.shape[0] // eb

    def copy(i):
        return pltpu.make_async_copy(zeros_ref.at[pl.ds(0, eb)], xs_hbm.at[pl.ds(pl.multiple_of(i * eb, eb), eb)], sem)

    def start(i, carry):
        copy(i).start()
        return carry

    def wait(i, carry):
        copy(i).wait()
        return carry

    lax.fori_loop(nreal_ref[0], n_blocks, start, 0)
    lax.fori_loop(nreal_ref[0], n_blocks, wait, 0)


def _dispatch_kernel(run_ref, tail_ref, nreal_ref, x_ref, lrank_ref, xs_hbm, loc, sem):
    step = pl.program_id(0) * pl.num_programs(1) + pl.program_id(1)
    n_steps = pl.num_programs(0) * pl.num_programs(1)
    slot = step % 2
    tm = x_ref.shape[0]
    n_loc = LOCAL_ROWS

    def wait_runs(sl):
        pltpu.make_async_copy(loc.at[sl], xs_hbm.at[pl.ds(0, n_loc)], sem.at[sl]).wait()

    @pl.when(step == 0)
    def _():
        loc[0, 0:EXPERT_BLOCK, :] = jnp.zeros((EXPERT_BLOCK, loc.shape[2]), loc.dtype)
        _tail_fill(tail_ref, loc.at[0], xs_hbm, sem.at[0])
        _unused_fill(nreal_ref, loc.at[0], xs_hbm, sem.at[0])

    @pl.when(step >= 2)
    def _():
        wait_runs(slot)

    lr = lrank_ref[...].astype(F32).T
    rows = lax.broadcasted_iota(jnp.int32, (n_loc, tm), 0).astype(F32)
    onehot = jnp.zeros((n_loc, tm), F32)
    for kk in range(TOP_K):
        onehot = jnp.where(rows == lr[kk:kk + 1, :], 1.0, onehot)
    loc[slot] = _pack_halves(_dot(onehot.astype(BF16), x_ref[...]))
    _run_copies(run_ref, lambda s, d, n: pltpu.make_async_copy(
        loc.at[slot, pl.ds(s, n)], xs_hbm.at[pl.ds(d, n)], sem.at[slot]))

    @pl.when(step == n_steps - 1)
    def _():
        @pl.when(n_steps > 1)
        def _():
            wait_runs(1 - slot)
        wait_runs(slot)


def _dispatch(runs, tails, n_real, x2, lrank, n_rows):
    bsz, t, d = x2.shape
    tm = ROW_TILE
    w = d // 2
    return pl.pallas_call(
        _dispatch_kernel,
        grid=(bsz, t // tm),
        in_specs=[pl.BlockSpec((None, None, 3, N_RUNS), lambda b, j: (b, j, 0, 0), memory_space=pltpu.SMEM),
                  pl.BlockSpec(memory_space=pltpu.SMEM), pl.BlockSpec(memory_space=pltpu.SMEM),
                  pl.BlockSpec((None, tm, d), lambda b, j: (b, j, 0)),
                  pl.BlockSpec((None, tm, LANE), lambda b, j: (b, j, 0))],
        out_specs=pl.BlockSpec(memory_space=pl.ANY),
        out_shape=jax.ShapeDtypeStruct((n_rows, w), jnp.uint32),
        scratch_shapes=[pltpu.VMEM((2, LOCAL_ROWS, w), jnp.uint32), pltpu.SemaphoreType.DMA((2,))],
        compiler_params=_cparams(("arbitrary", "arbitrary")),
    )(runs, tails, n_real, x2, lrank)


def _expert_kernel(be_ref, nreal_ref, xs_ref, w1_ref, b1_ref, w2_ref, b2_ref, y_ref):
    i = pl.program_id(0)
    half = w1_ref.shape[0] // 2

    @pl.when(i < nreal_ref[0])
    def _():
        lo, hi = _unpack_halves(xs_ref[...])
        hdn = (_dot(lo.astype(BF16), w1_ref[0:half, :]) + _dot(hi.astype(BF16), w1_ref[half:, :])) + b1_ref[...]
        acts = []
        for g in range(hdn.shape[1] // GLU_GROUP):
            glu = jnp.minimum(hdn[:, g * GLU_GROUP:g * GLU_GROUP + LANE], SWIGLU_LIMIT)
            lin = jnp.clip(hdn[:, g * GLU_GROUP + LANE:(g + 1) * GLU_GROUP], -SWIGLU_LIMIT, SWIGLU_LIMIT)
            acts.append((glu * _sigmoid(SWIGLU_ALPHA * glu) * (lin + 1.0)).astype(BF16))
        y_ref[...] = _pack_halves(_dot(jnp.concatenate(acts, axis=1), w2_ref[...].astype(BF16)) + b2_ref[...])

    @pl.when(i >= nreal_ref[0])
    def _():
        y_ref[...] = jnp.zeros_like(y_ref)


def _experts(block_expert, n_real, xs, w1, b1, expert_w2, b2, l):
    n_blocks = block_expert.shape[0]
    f, d = expert_w2.shape[2:]
    eb = EXPERT_BLOCK
    wspec = lambda k, n: pl.BlockSpec((None, k, n), lambda i, be, nr: (be[i], 0, 0))
    w2spec = pl.BlockSpec((None, None, f, d), lambda i, be, nr: (l, be[i], 0, 0))
    gs = pltpu.PrefetchScalarGridSpec(
        num_scalar_prefetch=2,
        grid=(n_blocks,),
        in_specs=[pl.BlockSpec((eb, d // 2), lambda i, be, nr: (jnp.minimum(i, nr[0] - 1), 0)),
                  wspec(d, 2 * f), wspec(1, 2 * f), w2spec, wspec(1, d)],
        out_specs=pl.BlockSpec((eb, d // 2), lambda i, be, nr: (i, 0)),
    )
    return pl.pallas_call(
        _expert_kernel,
        grid_spec=gs,
        out_shape=jax.ShapeDtypeStruct((n_blocks * eb, d // 2), jnp.uint32),
        compiler_params=_cparams(("arbitrary",)),
    )(block_expert, n_real, xs, w1, b1, expert_w2, b2)


def _combine_kernel(final, run_ref, nrun_ref, lrank_ref, y_hbm, h_ref, mod_ref, tg_ref, fg_ref, o_ref, loc, sem):
    tm = h_ref.shape[0]
    n_loc = LOCAL_ROWS
    step = pl.program_id(0) * pl.num_programs(1) + pl.program_id(1)
    n_steps = pl.num_programs(0) * pl.num_programs(1)
    slot = step % 2

    def fetch(table_ref, sl):
        _run_copies(table_ref, lambda s, d, n: pltpu.make_async_copy(
            y_hbm.at[pl.ds(d, n)], loc.at[sl, pl.ds(s, n)], sem.at[sl]))

    @pl.when(step == 0)
    def _():
        fetch(run_ref, slot)

    @pl.when(step + 1 < n_steps)
    def _():
        fetch(nrun_ref, 1 - slot)

    lr = lrank_ref[...]
    tg = tg_ref[...]
    cols = lax.broadcasted_iota(jnp.int32, (tm, n_loc), 1)
    wsel = jnp.zeros((tm, n_loc), F32)
    for kk in range(TOP_K):
        wsel = jnp.where(cols == lr[:, kk:kk + 1], tg[:, kk:kk + 1], wsel)
    wsel = wsel.astype(BF16)
    pltpu.make_async_copy(y_hbm.at[pl.ds(0, n_loc)], loc.at[slot], sem.at[slot]).wait()
    lo, hi = _unpack_halves(loc[slot])
    y = jnp.concatenate([_dot(wsel, lo.astype(BF16)), _dot(wsel, hi.astype(BF16))], axis=1)
    hn = h_ref[...] + mod_ref[5:6, :] * y
    if final:
        hn = _rms(hn, fg_ref[...])
    o_ref[...] = hn


def _combine(runs, lrank, y_rows, h, mod, tg, fg, final, j0):
    bsz, t, d = h.shape
    tm = ROW_TILE
    nj = t // tm
    row = lambda w: pl.BlockSpec((None, tm, w), lambda b, j: (b, j, 0))
    run_spec = lambda imap: pl.BlockSpec((None, None, 3, N_RUNS), imap, memory_space=pltpu.SMEM)
    next_tile = lambda b, j: (jnp.where(j + 1 < nj, b, jnp.minimum(b + 1, bsz - 1)), jnp.where(j + 1 < nj, j + 1, 0),
                              0, 0)
    return pl.pallas_call(
        functools.partial(_combine_kernel, final),
        grid=(bsz, nj),
        in_specs=[run_spec(lambda b, j: (b, j, 0, 0)), run_spec(next_tile),
                  row(LANE), pl.BlockSpec(memory_space=pl.ANY),
                  row(d), pl.BlockSpec((None, None, 6, d), lambda b, j: (b, jnp.minimum(j + j0, 1), 0, 0)),
                  row(LANE), pl.BlockSpec(fg.shape, lambda b, j: (0, 0))],
        out_specs=row(d),
        out_shape=jax.ShapeDtypeStruct((bsz, t, d), F32),
        scratch_shapes=[pltpu.VMEM((2, LOCAL_ROWS, d // 2), jnp.uint32), pltpu.SemaphoreType.DMA((2,))],
        compiler_params=_cparams(("arbitrary", "arbitrary")),
    )(runs, runs, lrank, y_rows, h, mod, tg, fg)


def _route(info, cnt, n_assign):
    eb = EXPERT_BLOCK
    n_tiles = info.shape[0] * info.shape[1]
    max_rows = n_assign + n_tiles * N_EXPERTS * (RUN_ALIGN - 1)
    filler_blocks = -(-LOCAL_ROWS // eb)
    n_blocks = -(-max_rows // eb) + N_EXPERTS + 2 * filler_blocks
    tile_parity = (jnp.arange(n_tiles, dtype=jnp.int32) % 2).reshape(info.shape[0], info.shape[1], 1)
    filler_row = (n_blocks - 2 * filler_blocks + tile_parity * filler_blocks) * eb
    counts = cnt[0, :N_EXPERTS].astype(jnp.int32)
    padded = (counts + eb - 1) // eb * eb
    pad_end = jnp.cumsum(padded)
    pad_start = pad_end - padded
    tcnt, tstart, tcarry = (info[:, :, r, :N_EXPERTS] for r in range(3))
    used = jnp.sum(tcnt, axis=-1, keepdims=True)
    runs = jnp.stack([jnp.concatenate([tcnt, LOCAL_ROWS - used], axis=-1),
                      jnp.concatenate([tstart, used], axis=-1),
                      jnp.concatenate([pad_start + tcarry, filler_row], axis=-1)],
                     axis=2).astype(jnp.int32)
    block_row = jnp.arange(n_blocks, dtype=jnp.int32) * eb
    block_expert = jnp.minimum(jnp.sum((pad_end[None, :] <= block_row[:, None]).astype(jnp.int32), axis=1),
                               N_EXPERTS - 1)
    n_real = (pad_end[-1] // eb).astype(jnp.int32).reshape(1)
    tails = jnp.stack([pad_start + counts, padded - counts]).astype(jnp.int32)
    return runs, tails, block_expert, n_real, n_blocks


def _rope_tables(t_lat, t_ctx):
    rows = t_lat // GRID_W
    row = jnp.repeat(jnp.arange(rows, dtype=F32), GRID_W)
    col = jnp.tile(jnp.arange(GRID_W, dtype=F32), rows)
    axis_dims = MLA_ROPE // 2
    inv = ROPE_BASE ** (-jnp.arange(0, axis_dims, 2, dtype=F32) / axis_dims)
    ar, ac = row[:, None] * inv, col[:, None] * inv
    cos64 = jnp.concatenate([jnp.cos(ar), jnp.cos(ar), jnp.cos(ac), jnp.cos(ac)], axis=-1)
    sin64 = jnp.concatenate([-jnp.sin(ar), jnp.sin(ar), -jnp.sin(ac), jnp.sin(ac)], axis=-1)
    pad = jnp.zeros((t_lat, LANE - MLA_ROPE), F32)
    cos_l = jnp.concatenate([cos64, pad], axis=-1)
    sin_l = jnp.concatenate([sin64, pad], axis=-1)
    cos_c = jnp.concatenate([jnp.ones((t_ctx, MLA_ROPE), F32), jnp.zeros((t_ctx, LANE - MLA_ROPE), F32)], axis=-1)
    sin_c = jnp.zeros((t_ctx, LANE), F32)
    return jnp.concatenate([cos_c, cos_l], axis=0), jnp.concatenate([sin_c, sin_l], axis=0)


def _rope_swap_perm():
    half = MLA_ROPE // 4
    perm = []
    for a in range(2):
        base = a * 2 * half
        perm += list(range(base + half, base + 2 * half)) + list(range(base, base + half))
    return jnp.array(perm, jnp.int32)


def _pad_cols(w, width):
    return jnp.concatenate([w, jnp.zeros(w.shape[:-1] + (width - w.shape[-1],), w.dtype)], axis=-1)


def _layer_weights(l, w_in, dn_a_log, dn_dt_bias, mla_w_qb, router_w, router_b, expert_w1, expert_b1):
    nqk = DN_HEADS * DN_DK
    wi = w_in[l]
    o = 0
    dn_qkv = wi[:, o:o + QKV_W]; o += QKV_W
    dn_z = wi[:, o:o + Z_W]; o += Z_W
    dn_ab = wi[:, o:o + 4 * DN_HEADS]; o += 4 * DN_HEADS
    sc3 = wi[:, o:o + SC3_W]; o += SC3_W
    qa = wi[:, o:o + MLA_Q_LORA]; o += MLA_Q_LORA
    ckv = wi[:, o:o + MLA_KV_LORA]; o += MLA_KV_LORA
    kr = wi[:, o:o + MLA_ROPE]
    perm = _rope_swap_perm()
    w_cat = jnp.concatenate([qa, ckv, _pad_cols(kr, LANE), _pad_cols(kr[:, perm], LANE), dn_qkv, dn_z, sc3,
                             _pad_cols(dn_ab, LANE)], axis=-1).astype(BF16)
    dq = MLA_NOPE + MLA_ROPE
    wq = mla_w_qb[l].reshape(MLA_Q_LORA, MLA_HEADS, dq)
    wq_rope = wq[:, :, MLA_NOPE:]
    wq_cat = jnp.concatenate([wq[:, :, :MLA_NOPE], _pad_cols(wq_rope, LANE), _pad_cols(wq_rope[:, :, perm], LANE)],
                             axis=-1).reshape(MLA_Q_LORA, MLA_HEADS * QH_W).astype(BF16)
    alog_row = _pad_cols(jnp.concatenate([dn_a_log[l].reshape(1, -1), jnp.zeros((1, 2 * DN_HEADS), F32)], -1), LANE)
    dtb_row = _pad_cols(jnp.concatenate([dn_dt_bias[l].reshape(1, -1), jnp.zeros((1, 2 * DN_HEADS), F32)], -1), LANE)
    rw = _pad_cols(router_w[l], LANE)
    rw_hi = rw.astype(BF16)
    rw_lo = (rw - rw_hi.astype(F32)).astype(BF16)
    rw_cat = jnp.concatenate([rw_hi, rw_lo], axis=-1)
    rb = jnp.concatenate([router_b[l], jnp.full((LANE - N_EXPERTS,), 0.0, F32)]).reshape(1, LANE)
    w1 = _deinterleave_w1(expert_w1, l)
    n_e, f2 = expert_b1.shape[1:]
    b1 = expert_b1[l].reshape(n_e, f2 // GLU_GROUP, LANE, 2).transpose(0, 1, 3, 2).reshape(n_e, 1, f2)
    return w_cat, wq_cat, alog_row, dtb_row, rw_cat, rw_hi, rb, w1, b1


def kernel(x, c, ctx, c_ctx, ada_w, ada_b, norm1_g, norm2_g, w_in, dn_conv_w, dn_a_log, dn_dt_bias, dn_norm_g, sc_conv_w, mla_q_norm_g, mla_w_qb, mla_kv_norm_g, mla_w_kvb, w_branch_gate, b_branch_gate, w_branch_dn, w_branch_sc, w_branch_mla, w_out, router_w, router_b, expert_w1, expert_b1, expert_w2, expert_b2, final_norm_g):
    bsz, t_lat, d = x.shape
    t_ctx = ctx.shape[1]
    depth = ada_w.shape[0]
    assert t_ctx == ROW_TILE and t_lat % ROW_TILE == 0 and t_lat % GRID_W == 0
    t = t_ctx + t_lat

    cos_t, sin_t = _rope_tables(t_lat, t_ctx)
    n_mod_rows = -(-(bsz + 1) // 8) * 8
    cvec = jnp.concatenate([c, c_ctx[None], jnp.zeros((n_mod_rows - bsz - 1, d), F32)], axis=0)
    mods = _ada_mods(cvec, ada_w, ada_b)

    h = (x, ctx)
    row2 = lambda v: v.reshape(1, -1)
    for l in range(depth):
        mod_lat = mods[l, :bsz].reshape(bsz, 1, 6, d)
        mod_ctx = jnp.broadcast_to(mods[l, bsz].reshape(1, 1, 6, d), (bsz, 1, 6, d))
        mod = jnp.concatenate([mod_ctx, mod_lat], axis=1)
        (w_cat, wq_cat, alog_row, dtb_row, rw_cat, rw_hi, rb, w1, b1) = _layer_weights(
            l, w_in, dn_a_log, dn_dt_bias, mla_w_qb, router_w, router_b, expert_w1, expert_b1)

        proj = _inproj(h, mod, row2(norm1_g[l]), w_cat, row2(mla_q_norm_g[l]), wq_cat,
                       row2(mla_kv_norm_g[l]), mla_w_kvb[l].astype(BF16), cos_t, sin_t)
        if l == 0:
            h = proj[-1]
        qkv, z, sc, ab, q, k, v = proj[:7]
        qkvn, ysc, gb = _prep(qkv, sc, ab, dn_conv_w[l], sc_conv_w[l], alog_row, dtb_row)
        o_f, o_b = _deltanet(qkvn, gb, t_ctx // DN_CHUNK)
        ymla = _mla(q, k, v)
        last = l == depth - 1
        j0 = 1 if last else 0
        hn, xm2, tg, lrank, info, cnt = _merge(h, mod, row2(norm1_g[l]), o_f, o_b, z, row2(dn_norm_g[l]), ysc, ymla,
                                 w_branch_gate[l].astype(BF16), row2(b_branch_gate[l]),
                                 w_branch_dn[l].astype(BF16), w_branch_sc[l].astype(BF16),
                                 w_branch_mla[l].astype(BF16), w_out[l].astype(BF16), row2(norm2_g[l]),
                                 rw_cat, rw_hi, rb, j0)
        runs, tails, block_expert, n_real, n_blocks = _route(info, cnt, bsz * (t - j0 * ROW_TILE) * TOP_K)
        xs = _dispatch(runs, tails, n_real, xm2, lrank, n_blocks * EXPERT_BLOCK)
        y_rows = _experts(block_expert, n_real, xs, w1, b1, expert_w2, expert_b2[l][:, None, :], l)
        h = _combine(runs, lrank, y_rows, hn, mod, tg, row2(final_norm_g), last, j0)
    return h
```

```python
import functools
import math

import jax
import jax.numpy as jnp
from jax import lax
from jax.experimental import pallas as pl
from jax.experimental.pallas import tpu as pltpu

F32 = jnp.float32
BF16 = jnp.bfloat16

GRID_W = 64
NORM_EPS = 1e-6
DN_HEADS = 4
DN_DK = 128
DN_DV = 128
DN_CHUNK = 64
SC_WIDTH = 512
MLA_HEADS = 4
MLA_Q_LORA = 256
MLA_KV_LORA = 128
MLA_NOPE = 128
MLA_ROPE = 64
MLA_V = 128
MLA_SCALE = (MLA_NOPE + MLA_ROPE) ** -0.5
Q_SCALE = MLA_SCALE * math.log2(math.e)
ROPE_BASE = 10000.0
N_EXPERTS = 32
TOP_K = 4
EXPERT_FF = 1024
SWIGLU_ALPHA = 1.702
SWIGLU_LIMIT = 7.0

LANE = 128
ROW_TILE = 256
HALO = 16
EXPERT_BLOCK = 512
PROJ_BATCH = 2
MERGE_BATCH = 2
DN_BATCH = 8
RUN_ALIGN = 8
LOCAL_ROWS = TOP_K * ROW_TILE + N_EXPERTS * RUN_ALIGN
N_RUNS = N_EXPERTS + 1
VMEM_LIMIT = 56 * 1024 * 1024

QKV_W = 3 * DN_HEADS * DN_DK
Z_W = DN_HEADS * DN_DV
SC3_W = 3 * SC_WIDTH
KVA_W = 3 * LANE
AB_W = LANE
IN_W = QKV_W + Z_W + SC3_W + MLA_Q_LORA + KVA_W + AB_W
QH_W = 3 * LANE
QK_W = 2 * LANE


def _cparams(sem):
    return pltpu.CompilerParams(dimension_semantics=sem, vmem_limit_bytes=VMEM_LIMIT)


def _dot(a, b):
    return jnp.dot(a, b, preferred_element_type=F32)


def _dot_nt(a, b):
    return lax.dot_general(a, b, (((1,), (1,)), ((), ())), preferred_element_type=F32)


def _rms(x, g):
    return x * lax.rsqrt(jnp.mean(x * x, axis=-1, keepdims=True) + NORM_EPS) * g


def _sigmoid(x):
    return 1.0 / (1.0 + jnp.exp(-x))


def _ada_kernel(c_ref, w_ref, b_ref, o_ref):
    cv = c_ref[...]
    s = cv * _sigmoid(cv)
    o_ref[...] = _dot(s.astype(BF16), w_ref[...].astype(BF16)) + b_ref[...]


def _ada_mods(cvec, ada_w, ada_b):
    n_layers, d, d6 = ada_w.shape
    r = cvec.shape[0]
    tn = 512
    return pl.pallas_call(
        _ada_kernel,
        grid=(n_layers, d6 // tn),
        in_specs=[pl.BlockSpec((r, d), lambda l, n: (0, 0)),
                  pl.BlockSpec((None, d, tn), lambda l, n: (l, 0, n)),
                  pl.BlockSpec((None, 1, tn), lambda l, n: (l, 0, n))],
        out_specs=pl.BlockSpec((None, r, tn), lambda l, n: (l, 0, n)),
        out_shape=jax.ShapeDtypeStruct((n_layers, r, d6), F32),
        compiler_params=_cparams(("arbitrary", "arbitrary")),
    )(cvec, ada_w, ada_b.reshape(n_layers, 1, d6))


def _inproj_kernel(split, *refs):
    if split:
        x_ref, ctx_ref, *refs, hout_ref = refs
    else:
        h_ref, *refs = refs
    (mod_ref, g1_ref, w_ref, gq_ref, wq_ref, gkv_ref, wkv_ref, cos_ref, sin_ref,
     qkv_ref, z_ref, sc_ref, ab_ref, q_ref, k_ref, v_ref) = refs
    nb, tm = qkv_ref.shape[0], qkv_ref.shape[1]
    if split:
        is_ctx = pl.program_id(1) == 0
        tiles = [jnp.where(is_ctx, ctx_ref[bb], x_ref[bb]) for bb in range(nb)]
        for bb in range(nb):
            hout_ref[bb] = tiles[bb]
    else:
        tiles = [h_ref[bb] for bb in range(nb)]
    xm = jnp.concatenate([_rms(tiles[bb], g1_ref[...]) * (1.0 + mod_ref[bb, 1:2, :]) + mod_ref[bb, 0:1, :]
                          for bb in range(nb)], axis=0).astype(BF16)
    p = _dot(xm, w_ref[...])
    rows = lambda bb: slice(bb * tm, (bb + 1) * tm)
    o = 0
    qa = p[:, o:o + MLA_Q_LORA]
    o += MLA_Q_LORA
    ckv = p[:, o:o + LANE]
    kr = p[:, o + LANE:o + 2 * LANE]
    krs = p[:, o + 2 * LANE:o + 3 * LANE]
    o += KVA_W
    for bb in range(nb):
        qkv_ref[bb] = p[rows(bb), o:o + QKV_W].astype(BF16)
        z_ref[bb] = p[rows(bb), o + QKV_W:o + QKV_W + Z_W].astype(BF16)
        sc_ref[bb] = p[rows(bb), o + QKV_W + Z_W:o + QKV_W + Z_W + SC3_W].astype(BF16)
        ab_ref[bb] = p[rows(bb), o + QKV_W + Z_W + SC3_W:o + QKV_W + Z_W + SC3_W + AB_W]

    cos = jnp.concatenate([cos_ref[...]] * nb, axis=0)
    sin = jnp.concatenate([sin_ref[...]] * nb, axis=0)
    k_rope = (kr * cos + krs * sin).astype(BF16)
    qn = _rms(qa, gq_ref[...]).astype(BF16)
    qf = _dot(qn, wq_ref[...])
    kvn = _rms(ckv, gkv_ref[...]).astype(BF16)
    kv = _dot(kvn, wkv_ref[...])
    for hh in range(MLA_HEADS):
        b0 = hh * QH_W
        q_nope = (qf[:, b0:b0 + LANE] * Q_SCALE).astype(BF16)
        q_rope = qf[:, b0 + LANE:b0 + 2 * LANE] * cos + qf[:, b0 + 2 * LANE:b0 + 3 * LANE] * sin
        q_rope = (q_rope * Q_SCALE).astype(BF16)
        c0 = hh * (MLA_NOPE + MLA_V)
        k_nope = kv[:, c0:c0 + MLA_NOPE].astype(BF16)
        vv = kv[:, c0 + MLA_NOPE:c0 + MLA_NOPE + MLA_V].astype(BF16)
        for bb in range(nb):
            q_ref[bb, hh, :, 0:LANE] = q_nope[rows(bb)]
            q_ref[bb, hh, :, LANE:2 * LANE] = q_rope[rows(bb)]
            k_ref[bb, hh, :, 0:LANE] = k_nope[rows(bb)]
            k_ref[bb, hh, :, LANE:2 * LANE] = k_rope[rows(bb)]
            v_ref[bb, hh] = vv[rows(bb)]


def _inproj(h, mod, g1, w_cat, gq, wq_cat, gkv, wkv, cos_t, sin_t):
    split = isinstance(h, tuple)
    tm = ROW_TILE
    if split:
        x, ctx = h
        bsz, t_lat, d = x.shape
        t = t_lat + ctx.shape[1]
    else:
        bsz, t, d = h.shape
    nj = t // tm
    nb = PROJ_BATCH if bsz % PROJ_BATCH == 0 else 1
    row = lambda w: pl.BlockSpec((nb, tm, w), lambda b, j: (b, j, 0))
    if split:
        h_specs = [pl.BlockSpec((nb, tm, d), lambda b, j: (b, jnp.maximum(j - 1, 0), 0)),
                   pl.BlockSpec((nb, tm, d), lambda b, j: (b, 0, 0))]
        h_args = (x, ctx)
    else:
        h_specs, h_args = [row(d)], (h,)
    full = lambda a: pl.BlockSpec(a.shape, lambda b, j: (0,) * a.ndim, pipeline_mode=pl.Buffered(1))
    head = lambda w: pl.BlockSpec((nb, MLA_HEADS, tm, w), lambda b, j: (b, 0, j, 0))
    sds = jax.ShapeDtypeStruct
    return pl.pallas_call(
        functools.partial(_inproj_kernel, split),
        grid=(bsz // nb, nj),
        in_specs=h_specs + [
            pl.BlockSpec((nb, None, 6, d), lambda b, j: (b, jnp.minimum(j, 1), 0, 0)),
            full(g1), full(w_cat), full(gq), full(wq_cat), full(gkv), full(wkv),
            pl.BlockSpec((tm, LANE), lambda b, j: (j, 0)),
            pl.BlockSpec((tm, LANE), lambda b, j: (j, 0))],
        out_specs=[row(QKV_W), row(Z_W), row(SC3_W), row(AB_W), head(QK_W), head(QK_W), head(MLA_V)]
        + ([row(d)] if split else []),
        out_shape=[sds((bsz, t, QKV_W), BF16), sds((bsz, t, Z_W), BF16), sds((bsz, t, SC3_W), BF16),
                   sds((bsz, t, AB_W), F32),
                   sds((bsz, MLA_HEADS, t, QK_W), BF16), sds((bsz, MLA_HEADS, t, QK_W), BF16),
                   sds((bsz, MLA_HEADS, t, MLA_V), BF16)] + ([sds((bsz, t, d), F32)] if split else []),
        compiler_params=_cparams(("parallel", "arbitrary")),
    )(*h_args, mod, g1, w_cat, gq, wq_cat, gkv, wkv, cos_t, sin_t)


def _shift_rows(x, prev_row, next_row):
    tm = x.shape[0]
    rid = lax.broadcasted_iota(jnp.int32, x.shape, 0)
    xp = jnp.where(rid == 0, prev_row, pltpu.roll(x, 1, 0))
    xn = jnp.where(rid == tm - 1, next_row, pltpu.roll(x, tm - 1, 0))
    return xp, xn


def _prep_kernel(qkv_ref, qkv_p_ref, qkv_n_ref, sc_ref, sc_p_ref, sc_n_ref, ab_ref,
                 dnw_ref, scw_ref, alog_ref, dtb_ref, qkvn_ref, ysc_ref, gb_ref):
    j = pl.program_id(1)
    nj = pl.num_programs(1)
    pv = (j >= 2).astype(F32)
    nv = jnp.logical_and(j >= 1, j <= nj - 2).astype(F32)

    x = qkv_ref[...].astype(F32)
    xp, xn = _shift_rows(x, qkv_p_ref[HALO - 1:HALO, :].astype(F32) * pv, qkv_n_ref[0:1, :].astype(F32) * nv)
    y = xp * dnw_ref[0:1, :] + x * dnw_ref[1:2, :] + xn * dnw_ref[2:3, :]
    y = y * _sigmoid(y)
    nqk = DN_HEADS * DN_DK
    for g in range(2 * DN_HEADS):
        yy = y[:, g * DN_DK:(g + 1) * DN_DK]
        yy = yy * lax.rsqrt(jnp.sum(yy * yy, axis=-1, keepdims=True) + 1e-6)
        if g < DN_HEADS:
            yy = yy * (DN_DK ** -0.5)
        qkvn_ref[:, g * DN_DK:(g + 1) * DN_DK] = yy.astype(BF16)
    qkvn_ref[:, 2 * nqk:] = y[:, 2 * nqk:].astype(BF16)

    s = sc_ref[...].astype(F32)
    sp = sc_p_ref[HALO - 1:HALO, :].astype(F32) * pv
    sn = sc_n_ref[0:1, :].astype(F32) * nv
    w = SC_WIDTH
    u = s[:, 2 * w:3 * w] * s[:, 0:w]
    up, un = _shift_rows(u, sp[:, 2 * w:3 * w] * sp[:, 0:w], sn[:, 2 * w:3 * w] * sn[:, 0:w])
    conv = up * scw_ref[0:1, :] + u * scw_ref[1:2, :] + un * scw_ref[2:3, :]
    ysc_ref[...] = (s[:, w:2 * w] * conv).astype(BF16)

    ab = ab_ref[...]
    sp_arg = ab + dtb_ref[...]
    softplus = jnp.maximum(sp_arg, 0.0) + jnp.log1p(jnp.exp(-jnp.abs(sp_arg)))
    gval = -jnp.exp(alog_ref[...]) * softplus
    lane = lax.broadcasted_iota(jnp.int32, ab.shape, 1)
    gb_ref[...] = jnp.where(lane < 2 * DN_HEADS, gval, _sigmoid(ab))


def _prep(qkv, sc, ab, dn_conv_w, sc_conv_w, alog_row, dtb_row):
    bsz, t, _ = qkv.shape
    tm = ROW_TILE
    nj = t // tm
    hb = tm // HALO
    nh = t // HALO
    row = lambda w: pl.BlockSpec((None, tm, w), lambda b, j: (b, j, 0))
    prev = lambda w: pl.BlockSpec((None, HALO, w), lambda b, j: (b, jnp.maximum(j * hb - 1, 0), 0))
    nxt = lambda w: pl.BlockSpec((None, HALO, w), lambda b, j: (b, jnp.minimum((j + 1) * hb, nh - 1), 0))
    full = lambda a: pl.BlockSpec(a.shape, lambda b, j: (0,) * a.ndim)
    sds = jax.ShapeDtypeStruct
    return pl.pallas_call(
        _prep_kernel,
        grid=(bsz, nj),
        in_specs=[row(QKV_W), prev(QKV_W), nxt(QKV_W), row(SC3_W), prev(SC3_W), nxt(SC3_W), row(AB_W),
                  full(dn_conv_w), full(sc_conv_w), full(alog_row), full(dtb_row)],
        out_specs=[row(QKV_W), row(SC_WIDTH), row(AB_W)],
        out_shape=[sds((bsz, t, QKV_W), BF16), sds((bsz, t, SC_WIDTH), BF16), sds((bsz, t, AB_W), F32)],
        compiler_params=_cparams(("parallel", "arbitrary")),
    )(qkv, qkv, qkv, sc, sc, sc, ab, dn_conv_w, sc_conv_w, alog_row, dtb_row)


DN_GROUP = 2


def _stack_heads(x, h0):
    return jnp.concatenate([x[:, (h0 + i) * LANE:(h0 + i + 1) * LANE] for i in range(DN_GROUP)], axis=0)


def _stack_cols(x, c0):
    return jnp.concatenate([x[:, c0 + i:c0 + i + 1] for i in range(DN_GROUP)], axis=0)


def _dn_direction(d, h0, x, gbt, tri, m_causal, m_strict, eye, s_ref):
    c = DN_CHUNK
    grp = range(DN_GROUP)
    nqk = DN_HEADS * DN_DK
    qst = _stack_heads(x[:, 0:nqk], h0)
    kst = _stack_heads(x[:, nqk:2 * nqk], h0)
    vst = _stack_heads(x[:, 2 * nqk:], h0)
    gc_all = jnp.dot(tri, gbt, preferred_element_type=F32, precision=lax.Precision.HIGHEST)
    col0 = d * DN_HEADS + h0
    gc = _stack_cols(gc_all, col0)
    beta = _stack_cols(gbt, 2 * DN_HEADS + col0)
    last = c - 1 if d == 0 else 0
    gl_heads = [gc_all[last:last + 1, col0 + i:col0 + i + 1] for i in grp]
    gl = jnp.concatenate([jnp.broadcast_to(g1, (c, 1)) for g1 in gl_heads], axis=0)
    kf = kst.astype(F32)
    kb = kf * beta
    kk = _dot_nt(kb.astype(BF16), kst)
    qk = _dot_nt(qst, kst)
    yield

    hc = DN_GROUP * c
    gmat = jnp.broadcast_to(gc, (hc, hc))
    dec = jnp.exp(jnp.minimum(gmat - gmat.T, 0.0))
    xpow = -(kk * jnp.where(m_strict > 0.5, dec, 0.0))
    tinv = eye + xpow
    intra = (qk * jnp.where(m_causal > 0.5, dec, 0.0)).astype(BF16)
    eg = jnp.exp(gc)
    rhs = jnp.concatenate([vst.astype(F32) * beta, kb * eg], axis=1).astype(BF16)
    qd = qst.astype(F32) * eg
    kdt = (kf * jnp.exp(gl - gc)).T.astype(BF16)
    for _ in range(int(math.log2(c)) - 1):
        xb = xpow.astype(BF16)
        xpow = _dot(xb, xb)
        yield
        tinv = tinv + _dot(tinv.astype(BF16), xpow.astype(BF16))
        yield
    sol = _dot(tinv.astype(BF16), rhs)
    yield
    u = sol[:, 0:DN_DV]
    w = sol[:, DN_DV:]

    s_cat = s_ref[...]
    wq = jnp.concatenate([w, qd], axis=0).astype(BF16)
    full = _dot(wq, s_cat.astype(BF16))
    yield
    ws = jnp.concatenate([full[i * c:(i + 1) * c, i * DN_DV:(i + 1) * DN_DV] for i in grp], axis=0)
    qs = jnp.concatenate([full[hc + i * c:hc + (i + 1) * c, i * DN_DV:(i + 1) * DN_DV] for i in grp], axis=0)
    vnew = u - ws
    o = qs + _dot(intra, vnew.astype(BF16))
    rhead = lax.broadcasted_iota(jnp.int32, (hc, DN_DV), 0) // c
    vblk = jnp.concatenate([jnp.where(rhead == i, vnew, 0.0) for i in grp], axis=1).astype(BF16)
    gt = jnp.concatenate([jnp.broadcast_to(jnp.exp(g1), (1, DN_DV)) for g1 in gl_heads], axis=1)
    s_ref[...] = s_cat * gt + _dot(kdt, vblk)
    yield
    return jnp.concatenate([o[i * c:(i + 1) * c] for i in grp], axis=1)


def _lockstep(gens):
    outs = [None] * len(gens)
    live = list(range(len(gens)))
    while live:
        for i in list(live):
            try:
                next(gens[i])
            except StopIteration as done:
                outs[i] = done.value
                live.remove(i)
    return outs


def _dn_kernel(xf_ref, gf_ref, xb_ref, gb_ref, trif_ref, trib_ref, mcf_ref, msf_ref, mcb_ref, msb_ref, eye_ref,
               of_ref, ob_ref, s_ref):
    @pl.when(pl.program_id(1) == 0)
    def _():
        s_ref[...] = jnp.zeros_like(s_ref)

    eye = eye_ref[...]
    gw = DN_GROUP * DN_DV
    gens, dsts = [], []
    for bb in range(xf_ref.shape[0]):
        for g in range(DN_HEADS // DN_GROUP):
            gens.append(_dn_direction(0, g * DN_GROUP, xf_ref[bb], gf_ref[bb], trif_ref[...], mcf_ref[...],
                                      msf_ref[...], eye, s_ref.at[bb, 0, :, g * gw:(g + 1) * gw]))
            dsts.append((of_ref, bb, g))
            gens.append(_dn_direction(1, g * DN_GROUP, xb_ref[bb], gb_ref[bb], trib_ref[...], mcb_ref[...],
                                      msb_ref[...], eye, s_ref.at[bb, 1, :, g * gw:(g + 1) * gw]))
            dsts.append((ob_ref, bb, g))
    for (o_ref, bb, g), o in zip(dsts, _lockstep(gens)):
        o_ref[bb, :, g * gw:(g + 1) * gw] = o.astype(BF16)


def _deltanet(qkvn, gb, n_ctx_chunks):
    bsz, t, _ = qkvn.shape
    c = DN_CHUNK
    nc = t // c
    hc = DN_GROUP * c
    bmap = lambda i: jnp.where(i < n_ctx_chunks, n_ctx_chunks - 1 - i, nc + n_ctx_chunks - 1 - i)
    ii = jnp.arange(c)
    tri_f = (ii[None, :] <= ii[:, None]).astype(F32)
    tri_b = (ii[None, :] >= ii[:, None]).astype(F32)
    r = jnp.arange(hc)
    same = (r[:, None] // c) == (r[None, :] // c)
    pi, pj = r[:, None] % c, r[None, :] % c
    mc_f = (same & (pj <= pi)).astype(F32)
    ms_f = (same & (pj < pi)).astype(F32)
    mc_b = (same & (pj >= pi)).astype(F32)
    ms_b = (same & (pj > pi)).astype(F32)
    eye = jnp.eye(hc, dtype=F32)
    full = lambda a: pl.BlockSpec(a.shape, lambda b, i: (0,) * a.ndim)
    nb = DN_BATCH if bsz % DN_BATCH == 0 else 1
    fw = lambda w: pl.BlockSpec((nb, c, w), lambda b, i: (b, i, 0))
    bw = lambda w: pl.BlockSpec((nb, c, w), lambda b, i: (b, bmap(i), 0))
    ow = DN_HEADS * DN_DV
    sds = jax.ShapeDtypeStruct
    return pl.pallas_call(
        _dn_kernel,
        grid=(bsz // nb, nc),
        in_specs=[fw(QKV_W), fw(AB_W), bw(QKV_W), bw(AB_W), full(tri_f), full(tri_b),
                  full(mc_f), full(ms_f), full(mc_b), full(ms_b), full(eye)],
        out_specs=[fw(ow), bw(ow)],
        out_shape=[sds((bsz, t, ow), BF16), sds((bsz, t, ow), BF16)],
        scratch_shapes=[pltpu.VMEM((nb, 2, DN_DK, DN_HEADS * DN_DV), F32)],
        compiler_params=_cparams(("arbitrary", "arbitrary")),
    )(qkvn, gb, qkvn, gb, tri_f, tri_b, mc_f, ms_f, mc_b, ms_b, eye)


MLA_GROUP = 4


def _softmax_av(q, k, v):
    s = _dot_nt(q, k)
    yield
    m = jnp.max(s, axis=-1, keepdims=True)
    p = jnp.exp2(s - m)
    l = jnp.sum(p, axis=-1, keepdims=True)
    p = p.astype(BF16)
    yield
    return _dot(p, v) / l


def _mla_kernel(q_ref, k_ref, v_ref, o_ref):
    j = pl.program_id(2)
    tc = q_ref.shape[1]

    def attend(n_keys):
        outs = _lockstep([_softmax_av(q_ref[g], k_ref[g, 0:n_keys, :], v_ref[g, 0:n_keys, :])
                          for g in range(MLA_GROUP)])
        for g, o in enumerate(outs):
            o_ref[:, g * MLA_V:(g + 1) * MLA_V] = o.astype(BF16)

    @pl.when(j == 0)
    def _():
        attend(tc)

    @pl.when(j > 0)
    def _():
        attend(k_ref.shape[1])


def _mla(q, k, v):
    bsz, nh, t, _ = q.shape
    tq = ROW_TILE
    g = MLA_GROUP
    return pl.pallas_call(
        _mla_kernel,
        grid=(bsz, nh // g, t // tq),
        in_specs=[pl.BlockSpec((None, g, tq, QK_W), lambda b, hp, j: (b, hp, j, 0)),
                  pl.BlockSpec((None, g, t, QK_W), lambda b, hp, j: (b, hp, 0, 0), pipeline_mode=pl.Buffered(1)),
                  pl.BlockSpec((None, g, t, MLA_V), lambda b, hp, j: (b, hp, 0, 0), pipeline_mode=pl.Buffered(1))],
        out_specs=pl.BlockSpec((None, tq, g * MLA_V), lambda b, hp, j: (b, j, hp)),
        out_shape=jax.ShapeDtypeStruct((bsz, t, nh * MLA_V), BF16),
        compiler_params=_cparams(("parallel", "parallel", "arbitrary")),
    )(q, k, v)


def _pack_halves(x):
    w = x.shape[1] // 2
    lo = lax.bitcast_convert_type(x[:, :w].astype(BF16).astype(F32), jnp.uint32)
    hi = lax.bitcast_convert_type(x[:, w:].astype(BF16).astype(F32), jnp.uint32)
    return (hi & jnp.uint32(0xFFFF0000)) | (lo >> 16)


def _unpack_halves(p):
    lo = lax.bitcast_convert_type(p << 16, F32)
    hi = lax.bitcast_convert_type(p & jnp.uint32(0xFFFF0000), F32)
    return lo, hi


def _merge_kernel(h_ref, mod_ref, g1_ref, of_ref, ob_ref, z_ref, dng_ref, ysc_ref, ymla_ref,
                  wg_ref, bg_ref, wdn_ref, wsc_ref, wmla_ref, wout_ref, g2_ref, rw_ref, rwh_ref, rb_ref, tril_ref, triu_ref,
                  hn_ref, xm2_ref, tg_ref, lrank_ref, info_ref, cnt_ref, cnt_scr):
    @pl.when(jnp.logical_and(pl.program_id(0) == 0, pl.program_id(1) == 0))
    def _():
        cnt_scr[...] = jnp.zeros_like(cnt_scr)

    nb, tm, d = h_ref.shape
    rows = lambda bb: slice(bb * tm, (bb + 1) * tm)
    cat = lambda ref: jnp.concatenate([ref[bb] for bb in range(nb)], axis=0)
    xm = jnp.concatenate([_rms(h_ref[bb], g1_ref[...]) * (1.0 + mod_ref[bb, 1:2, :]) + mod_ref[bb, 0:1, :]
                          for bb in range(nb)], axis=0).astype(BF16)
    o = cat(of_ref).astype(F32) + cat(ob_ref).astype(F32)
    z = cat(z_ref).astype(F32)
    parts = []
    for hh in range(DN_HEADS):
        oh = o[:, hh * DN_DV:(hh + 1) * DN_DV]
        zh = z[:, hh * DN_DV:(hh + 1) * DN_DV]
        parts.append(_rms(oh, dng_ref[...]) * (zh * _sigmoid(zh)))
    ydn = jnp.concatenate(parts, axis=1).astype(BF16)
    gates = _sigmoid(_dot(xm, wg_ref[...]) + bg_ref[...])
    merged = (gates[:, 0:d] * _dot(ydn, wdn_ref[...]) + gates[:, d:2 * d] * _dot(cat(ysc_ref), wsc_ref[...])
              + gates[:, 2 * d:3 * d] * _dot(cat(ymla_ref), wmla_ref[...]))
    y = _dot(merged.astype(BF16), wout_ref[...])
    xm2s = []
    for bb in range(nb):
        hn = h_ref[bb] + mod_ref[bb, 2:3, :] * y[rows(bb)]
        hn_ref[bb] = hn
        xm2s.append(_rms(hn, g2_ref[...]) * (1.0 + mod_ref[bb, 4:5, :]) + mod_ref[bb, 3:4, :])
    xm2 = jnp.concatenate(xm2s, axis=0)
    xh = xm2.astype(BF16)
    for bb in range(nb):
        xm2_ref[bb] = xh[rows(bb)]

    xl = (xm2 - xh.astype(F32)).astype(BF16)
    a = _dot(xh, rw_ref[...])
    logits = a[:, 0:LANE] + a[:, LANE:2 * LANE] + _dot(xl, rwh_ref[...]) + rb_ref[...]
    lane = lax.broadcasted_iota(jnp.int32, logits.shape, 1)
    work = jnp.where(lane < N_EXPERTS, logits, -jnp.inf)
    tv = jnp.zeros(logits.shape, F32)
    vals, idxs = [], []
    for kk in range(TOP_K):
        m = jnp.max(work, axis=-1, keepdims=True)
        idx = jnp.min(jnp.where(work == m, lane, LANE), axis=-1, keepdims=True)
        work = jnp.where(lane == idx, -jnp.inf, work)
        vals.append(m)
        idxs.append(idx)
    es = [jnp.exp(vv - vals[0]) for vv in vals]
    tot = es[0] + es[1] + es[2] + es[3]
    for kk in range(TOP_K):
        tv = jnp.where(lane == kk, es[kk] / tot, tv)
    sels = [(lane == idx).astype(F32) for idx in idxs]
    sel = sels[0] + sels[1] + sels[2] + sels[3]

    lane_t = lax.broadcasted_iota(jnp.int32, (tm, LANE), 1)
    sub = lax.broadcasted_iota(jnp.int32, (8, LANE), 0)
    for bb in range(nb):
        tg_ref[bb] = tv[rows(bb)]
        sel_t = sel[rows(bb)]
        tcnt = jnp.floor((jnp.sum(sel_t, axis=0, keepdims=True) + (RUN_ALIGN - 1.0)) * (1.0 / RUN_ALIGN)) * RUN_ALIGN
        tstart = _dot(jnp.broadcast_to(tcnt, (8, LANE)).astype(BF16), triu_ref[...])[0:1]
        lpos = _dot(tril_ref[...], sel_t.astype(BF16)) + tstart
        lrank = jnp.zeros((tm, LANE), F32)
        for kk in range(TOP_K):
            lp = jnp.sum(sels[kk][rows(bb)] * lpos, axis=-1, keepdims=True)
            lrank = jnp.where(lane_t == kk, lp, lrank)
        lrank_ref[bb] = lrank.astype(jnp.int32)
        carry = cnt_scr[...]
        info = jnp.where(sub == 0, tcnt, jnp.where(sub == 1, tstart, jnp.where(sub == 2, carry, 0.0)))
        info_ref[bb] = info.astype(jnp.int32)
        cnt_scr[...] = carry + tcnt
    cnt_ref[...] = cnt_scr[...]


def _merge(h, mod, g1, o_f, o_b, z, dng, ysc, ymla, wg, bg, wdn, wsc, wmla, wout, g2, rw_cat, rw_hi, rb, j0):
    bsz, t, d = h.shape
    tm = ROW_TILE
    nj = t // tm - j0
    to = nj * tm
    nb = MERGE_BATCH if bsz % MERGE_BATCH == 0 else 1
    row = lambda w: pl.BlockSpec((nb, tm, w), lambda b, j: (b, j + j0, 0))
    orow = lambda w: pl.BlockSpec((nb, tm, w), lambda b, j: (b, j, 0))
    full = lambda a: pl.BlockSpec(a.shape, lambda b, j: (0,) * a.ndim, pipeline_mode=pl.Buffered(1))
    ow = DN_HEADS * DN_DV
    sds = jax.ShapeDtypeStruct
    ii = jnp.arange(tm)
    tril = (ii[None, :] < ii[:, None]).astype(BF16)
    ee = jnp.arange(LANE)
    triu = (ee[:, None] < ee[None, :]).astype(BF16)
    tile_i32 = pl.BlockSpec((nb, None, 8, LANE), lambda b, j: (b, j, 0, 0))
    return pl.pallas_call(
        _merge_kernel,
        grid=(bsz // nb, nj),
        in_specs=[row(d), pl.BlockSpec((nb, None, 6, d), lambda b, j: (b, jnp.minimum(j + j0, 1), 0, 0)), full(g1),
                  row(ow), row(ow), row(Z_W), full(dng), row(SC_WIDTH), row(MLA_HEADS * MLA_V),
                  full(wg), full(bg), full(wdn), full(wsc), full(wmla), full(wout), full(g2),
                  full(rw_cat), full(rw_hi), full(rb), full(tril), full(triu)],
        out_specs=[orow(d), orow(d), orow(LANE), orow(LANE), tile_i32,
                   pl.BlockSpec((1, LANE), lambda b, j: (0, 0))],
        out_shape=[sds((bsz, to, d), F32), sds((bsz, to, d), BF16), sds((bsz, to, LANE), F32),
                   sds((bsz, to, LANE), jnp.int32), sds((bsz, nj, 8, LANE), jnp.int32), sds((1, LANE), F32)],
        scratch_shapes=[pltpu.VMEM((1, LANE), F32)],
        compiler_params=_cparams(("arbitrary", "arbitrary")),
    )(h, mod, g1, o_f, o_b, z, dng, ysc, ymla, wg, bg, wdn, wsc, wmla, wout, g2, rw_cat, rw_hi, rb, tril, triu)


GLU_GROUP = 2 * LANE


def _deinterleave(w_ref, perm, o_ref):
    for g in range(w_ref.shape[1] // GLU_GROUP):
        sl = slice(g * GLU_GROUP, (g + 1) * GLU_GROUP)
        o_ref[:, sl] = _dot(w_ref[:, sl].astype(BF16), perm).astype(BF16)


def _glu_perm():
    i = jnp.arange(GLU_GROUP)
    return (jnp.where(i % 2 == 0, i // 2, LANE + i // 2)[:, None] == i[None, :]).astype(BF16)


EXPERT_RUN_PIECES = tuple(RUN_ALIGN << i for i in range((ROW_TILE // RUN_ALIGN).bit_length()))
FILLER_RUN_PIECES = tuple(RUN_ALIGN << i for i in range((LOCAL_ROWS // RUN_ALIGN).bit_length()))


def _run_copies(run_ref, make_copy):
    def one_run(e, pieces):
        cnt, src, dst = run_ref[0, e], run_ref[1, e], run_ref[2, e]
        off = jnp.int32(0)
        for piece in pieces:
            has = (cnt & piece) != 0

            @pl.when(has)
            def _(off=off, piece=piece):
                make_copy(pl.multiple_of(src + off, RUN_ALIGN), pl.multiple_of(dst + off, RUN_ALIGN), piece).start()
            off = off + jnp.where(has, piece, 0)

    def expert_run(e):
        cnt, src, dst = run_ref[0, e], run_ref[1, e], run_ref[2, e]

        def copies(n_rows):
            def start_all():
                off = 0
                for piece in EXPERT_RUN_PIECES:
                    if n_rows & piece:
                        make_copy(pl.multiple_of(src + off, RUN_ALIGN), pl.multiple_of(dst + off, RUN_ALIGN),
                                  piece).start()
                        off += piece
            return start_all
        lax.switch(cnt // RUN_ALIGN, [copies(n) for n in range(0, ROW_TILE + 1, RUN_ALIGN)])

    def per_expert(e, carry):
        expert_run(e)
        return carry
    lax.fori_loop(0, N_EXPERTS, per_expert, 0)
    one_run(N_EXPERTS, FILLER_RUN_PIECES)


TAIL_PIECES = tuple(RUN_ALIGN << i for i in range((EXPERT_BLOCK // RUN_ALIGN).bit_length() - 1))


def _tail_fill(tail_ref, zeros_ref, xs_hbm, sem):
    def pieces(e, act):
        dst, cnt = tail_ref[0, e], tail_ref[1, e]
        off = jnp.int32(0)
        for piece in TAIL_PIECES:
            has = (cnt & piece) != 0

            @pl.when(has)
            def _(off=off, piece=piece):
                act(pltpu.make_async_copy(zeros_ref.at[pl.ds(0, piece)],
                                          xs_hbm.at[pl.ds(pl.multiple_of(dst + off, RUN_ALIGN), piece)], sem))
            off = off + jnp.where(has, piece, 0)

    def start(e, carry):
        pieces(e, lambda cp: cp.start())
        return carry

    def wait(e, carry):
        pieces(e, lambda cp: cp.wait())
        return carry

    lax.fori_loop(0, N_EXPERTS, start, 0)
    lax.fori_loop(0, N_EXPERTS, wait, 0)


def _unused_fill(nreal_ref, zeros_ref, xs_hbm, sem):
    eb = EXPERT_BLOCK
    n_blocks = xs_hbm.shape[0] // eb

    def copy(i):
        return pltpu.make_async_copy(zeros_ref.at[pl.ds(0, eb)], xs_hbm.at[pl.ds(pl.multiple_of(i * eb, eb), eb)], sem)

    def start(i, carry):
        copy(i).start()
        return carry

    def wait(i, carry):
        copy(i).wait()
        return carry

    lax.fori_loop(nreal_ref[0], n_blocks, start, 0)
    lax.fori_loop(nreal_ref[0], n_blocks, wait, 0)


def _dispatch_kernel(run_ref, tail_ref, nreal_ref, x_ref, lrank_ref, xs_hbm, loc, sem):
    step = pl.program_id(0) * pl.num_programs(1) + pl.program_id(1)
    n_steps = pl.num_programs(0) * pl.num_programs(1)
    slot = step % 2
    tm = x_ref.shape[0]
    n_loc = LOCAL_ROWS

    def wait_runs(sl):
        pltpu.make_async_copy(loc.at[sl], xs_hbm.at[pl.ds(0, n_loc)], sem.at[sl]).wait()

    @pl.when(step == 0)
    def _():
        loc[0, 0:EXPERT_BLOCK, :] = jnp.zeros((EXPERT_BLOCK, loc.shape[2]), loc.dtype)
        _tail_fill(tail_ref, loc.at[0], xs_hbm, sem.at[0])
        _unused_fill(nreal_ref, loc.at[0], xs_hbm, sem.at[0])

    @pl.when(step >= 2)
    def _():
        wait_runs(slot)

    lr = lrank_ref[...].astype(F32).T
    rows = lax.broadcasted_iota(jnp.int32, (n_loc, tm), 0).astype(F32)
    onehot = jnp.zeros((n_loc, tm), F32)
    for kk in range(TOP_K):
        onehot = jnp.where(rows == lr[kk:kk + 1, :], 1.0, onehot)
    loc[slot] = _pack_halves(_dot(onehot.astype(BF16), x_ref[...]))
    _run_copies(run_ref, lambda s, d, n: pltpu.make_async_copy(
        loc.at[slot, pl.ds(s, n)], xs_hbm.at[pl.ds(d, n)], sem.at[slot]))

    @pl.when(step == n_steps - 1)
    def _():
        @pl.when(n_steps > 1)
        def _():
            wait_runs(1 - slot)
        wait_runs(slot)


def _dispatch(runs, tails, n_real, x2, lrank, n_rows):
    bsz, t, d = x2.shape
    tm = ROW_TILE
    w = d // 2
    return pl.pallas_call(
        _dispatch_kernel,
        grid=(bsz, t // tm),
        in_specs=[pl.BlockSpec((None, None, 3, N_RUNS), lambda b, j: (b, j, 0, 0), memory_space=pltpu.SMEM),
                  pl.BlockSpec(memory_space=pltpu.SMEM), pl.BlockSpec(memory_space=pltpu.SMEM),
                  pl.BlockSpec((None, tm, d), lambda b, j: (b, j, 0)),
                  pl.BlockSpec((None, tm, LANE), lambda b, j: (b, j, 0))],
        out_specs=pl.BlockSpec(memory_space=pl.ANY),
        out_shape=jax.ShapeDtypeStruct((n_rows, w), jnp.uint32),
        scratch_shapes=[pltpu.VMEM((2, LOCAL_ROWS, w), jnp.uint32), pltpu.SemaphoreType.DMA((2,))],
        compiler_params=_cparams(("arbitrary", "arbitrary")),
    )(runs, tails, n_real, x2, lrank)


def _expert_kernel(be_ref, nreal_ref, xs_ref, w1_ref, perm_ref, b1_ref, w2_ref, b2_ref, y_ref, w1p):
    i = pl.program_id(0)
    half = w1_ref.shape[0] // 2
    used = i < nreal_ref[0]

    @pl.when(jnp.logical_and(used, jnp.logical_or(i == 0, be_ref[i] != be_ref[jnp.maximum(i - 1, 0)])))
    def _():
        _deinterleave(w1_ref, perm_ref[...], w1p)

    @pl.when(used)
    def _():
        lo, hi = _unpack_halves(xs_ref[...])
        hdn = (_dot(lo.astype(BF16), w1p[0:half, :]) + _dot(hi.astype(BF16), w1p[half:, :])) + b1_ref[...]
        acts = []
        for g in range(hdn.shape[1] // GLU_GROUP):
            glu = jnp.minimum(hdn[:, g * GLU_GROUP:g * GLU_GROUP + LANE], SWIGLU_LIMIT)
            lin = jnp.clip(hdn[:, g * GLU_GROUP + LANE:(g + 1) * GLU_GROUP], -SWIGLU_LIMIT, SWIGLU_LIMIT)
            acts.append((glu * _sigmoid(SWIGLU_ALPHA * glu) * (lin + 1.0)).astype(BF16))
        y_ref[...] = _pack_halves(_dot(jnp.concatenate(acts, axis=1), w2_ref[...].astype(BF16)) + b2_ref[...])

    @pl.when(i >= nreal_ref[0])
    def _():
        y_ref[...] = jnp.zeros_like(y_ref)


def _experts(block_expert, n_real, xs, expert_w1, b1, expert_w2, b2, l):
    n_blocks = block_expert.shape[0]
    f, d = expert_w2.shape[2:]
    eb = EXPERT_BLOCK
    perm = _glu_perm()
    wspec = lambda k, n: pl.BlockSpec((None, k, n), lambda i, be, nr: (be[i], 0, 0))
    lwspec = lambda k, n: pl.BlockSpec((None, None, k, n), lambda i, be, nr: (l, be[i], 0, 0))
    gs = pltpu.PrefetchScalarGridSpec(
        num_scalar_prefetch=2,
        grid=(n_blocks,),
        in_specs=[pl.BlockSpec((eb, d // 2), lambda i, be, nr: (jnp.minimum(i, nr[0] - 1), 0)),
                  lwspec(d, 2 * f), pl.BlockSpec(perm.shape, lambda i, be, nr: (0, 0)),
                  wspec(1, 2 * f), lwspec(f, d), wspec(1, d)],
        out_specs=pl.BlockSpec((eb, d // 2), lambda i, be, nr: (i, 0)),
        scratch_shapes=[pltpu.VMEM((d, 2 * f), BF16)],
    )
    return pl.pallas_call(
        _expert_kernel,
        grid_spec=gs,
        out_shape=jax.ShapeDtypeStruct((n_blocks * eb, d // 2), jnp.uint32),
        compiler_params=_cparams(("arbitrary",)),
    )(block_expert, n_real, xs, expert_w1, perm, b1, expert_w2, b2)


def _combine_kernel(final, run_ref, nrun_ref, lrank_ref, y_hbm, h_ref, mod_ref, tg_ref, fg_ref, o_ref, loc, sem):
    tm = h_ref.shape[0]
    n_loc = LOCAL_ROWS
    step = pl.program_id(0) * pl.num_programs(1) + pl.program_id(1)
    n_steps = pl.num_programs(0) * pl.num_programs(1)
    slot = step % 2

    def fetch(table_ref, sl):
        _run_copies(table_ref, lambda s, d, n: pltpu.make_async_copy(
            y_hbm.at[pl.ds(d, n)], loc.at[sl, pl.ds(s, n)], sem.at[sl]))

    @pl.when(step == 0)
    def _():
        fetch(run_ref, slot)

    @pl.when(step + 1 < n_steps)
    def _():
        fetch(nrun_ref, 1 - slot)

    lr = lrank_ref[...]
    tg = tg_ref[...]
    cols = lax.broadcasted_iota(jnp.int32, (tm, n_loc), 1)
    wsel = jnp.zeros((tm, n_loc), F32)
    for kk in range(TOP_K):
        wsel = jnp.where(cols == lr[:, kk:kk + 1], tg[:, kk:kk + 1], wsel)
    wsel = wsel.astype(BF16)
    pltpu.make_async_copy(y_hbm.at[pl.ds(0, n_loc)], loc.at[slot], sem.at[slot]).wait()
    lo, hi = _unpack_halves(loc[slot])
    y = jnp.concatenate([_dot(wsel, lo.astype(BF16)), _dot(wsel, hi.astype(BF16))], axis=1)
    hn = h_ref[...] + mod_ref[5:6, :] * y
    if final:
        hn = _rms(hn, fg_ref[...])
    o_ref[...] = hn


def _combine(runs, lrank, y_rows, h, mod, tg, fg, final, j0):
    bsz, t, d = h.shape
    tm = ROW_TILE
    nj = t // tm
    row = lambda w: pl.BlockSpec((None, tm, w), lambda b, j: (b, j, 0))
    run_spec = lambda imap: pl.BlockSpec((None, None, 3, N_RUNS), imap, memory_space=pltpu.SMEM)
    next_tile = lambda b, j: (jnp.where(j + 1 < nj, b, jnp.minimum(b + 1, bsz - 1)), jnp.where(j + 1 < nj, j + 1, 0),
                              0, 0)
    return pl.pallas_call(
        functools.partial(_combine_kernel, final),
        grid=(bsz, nj),
        in_specs=[run_spec(lambda b, j: (b, j, 0, 0)), run_spec(next_tile),
                  row(LANE), pl.BlockSpec(memory_space=pl.ANY),
                  row(d), pl.BlockSpec((None, None, 6, d), lambda b, j: (b, jnp.minimum(j + j0, 1), 0, 0)),
                  row(LANE), pl.BlockSpec(fg.shape, lambda b, j: (0, 0))],
        out_specs=row(d),
        out_shape=jax.ShapeDtypeStruct((bsz, t, d), F32),
        scratch_shapes=[pltpu.VMEM((2, LOCAL_ROWS, d // 2), jnp.uint32), pltpu.SemaphoreType.DMA((2,))],
        compiler_params=_cparams(("arbitrary", "arbitrary")),
    )(runs, runs, lrank, y_rows, h, mod, tg, fg)


def _route(info, cnt, n_assign):
    eb = EXPERT_BLOCK
    n_tiles = info.shape[0] * info.shape[1]
    max_rows = n_assign + n_tiles * N_EXPERTS * (RUN_ALIGN - 1)
    filler_blocks = -(-LOCAL_ROWS // eb)
    n_blocks = -(-max_rows // eb) + N_EXPERTS + 2 * filler_blocks
    tile_parity = (jnp.arange(n_tiles, dtype=jnp.int32) % 2).reshape(info.shape[0], info.shape[1], 1)
    filler_row = (n_blocks - 2 * filler_blocks + tile_parity * filler_blocks) * eb
    counts = cnt[0, :N_EXPERTS].astype(jnp.int32)
    padded = (counts + eb - 1) // eb * eb
    pad_end = jnp.cumsum(padded)
    pad_start = pad_end - padded
    tcnt, tstart, tcarry = (info[:, :, r, :N_EXPERTS] for r in range(3))
    used = jnp.sum(tcnt, axis=-1, keepdims=True)
    runs = jnp.stack([jnp.concatenate([tcnt, LOCAL_ROWS - used], axis=-1),
                      jnp.concatenate([tstart, used], axis=-1),
                      jnp.concatenate([pad_start + tcarry, filler_row], axis=-1)],
                     axis=2).astype(jnp.int32)
    block_row = jnp.arange(n_blocks, dtype=jnp.int32) * eb
    block_expert = jnp.minimum(jnp.sum((pad_end[None, :] <= block_row[:, None]).astype(jnp.int32), axis=1),
                               N_EXPERTS - 1)
    n_real = (pad_end[-1] // eb).astype(jnp.int32).reshape(1)
    tails = jnp.stack([pad_start + counts, padded - counts]).astype(jnp.int32)
    return runs, tails, block_expert, n_real, n_blocks


def _rope_tables(t_lat, t_ctx):
    rows = t_lat // GRID_W
    row = jnp.repeat(jnp.arange(rows, dtype=F32), GRID_W)
    col = jnp.tile(jnp.arange(GRID_W, dtype=F32), rows)
    axis_dims = MLA_ROPE // 2
    inv = ROPE_BASE ** (-jnp.arange(0, axis_dims, 2, dtype=F32) / axis_dims)
    ar, ac = row[:, None] * inv, col[:, None] * inv
    cos64 = jnp.concatenate([jnp.cos(ar), jnp.cos(ar), jnp.cos(ac), jnp.cos(ac)], axis=-1)
    sin64 = jnp.concatenate([-jnp.sin(ar), jnp.sin(ar), -jnp.sin(ac), jnp.sin(ac)], axis=-1)
    pad = jnp.zeros((t_lat, LANE - MLA_ROPE), F32)
    cos_l = jnp.concatenate([cos64, pad], axis=-1)
    sin_l = jnp.concatenate([sin64, pad], axis=-1)
    cos_c = jnp.concatenate([jnp.ones((t_ctx, MLA_ROPE), F32), jnp.zeros((t_ctx, LANE - MLA_ROPE), F32)], axis=-1)
    sin_c = jnp.zeros((t_ctx, LANE), F32)
    return jnp.concatenate([cos_c, cos_l], axis=0), jnp.concatenate([sin_c, sin_l], axis=0)


def _rope_swap_perm():
    half = MLA_ROPE // 4
    perm = []
    for a in range(2):
        base = a * 2 * half
        perm += list(range(base + half, base + 2 * half)) + list(range(base, base + half))
    return jnp.array(perm, jnp.int32)


def _pad_cols(w, width):
    return jnp.concatenate([w, jnp.zeros(w.shape[:-1] + (width - w.shape[-1],), w.dtype)], axis=-1)


def _layer_weights(l, w_in, dn_a_log, dn_dt_bias, mla_w_qb, router_w, router_b, expert_b1):
    nqk = DN_HEADS * DN_DK
    wi = w_in[l]
    o = 0
    dn_qkv = wi[:, o:o + QKV_W]; o += QKV_W
    dn_z = wi[:, o:o + Z_W]; o += Z_W
    dn_ab = wi[:, o:o + 4 * DN_HEADS]; o += 4 * DN_HEADS
    sc3 = wi[:, o:o + SC3_W]; o += SC3_W
    qa = wi[:, o:o + MLA_Q_LORA]; o += MLA_Q_LORA
    ckv = wi[:, o:o + MLA_KV_LORA]; o += MLA_KV_LORA
    kr = wi[:, o:o + MLA_ROPE]
    perm = _rope_swap_perm()
    w_cat = jnp.concatenate([qa, ckv, _pad_cols(kr, LANE), _pad_cols(kr[:, perm], LANE), dn_qkv, dn_z, sc3,
                             _pad_cols(dn_ab, LANE)], axis=-1).astype(BF16)
    dq = MLA_NOPE + MLA_ROPE
    wq = mla_w_qb[l].reshape(MLA_Q_LORA, MLA_HEADS, dq)
    wq_rope = wq[:, :, MLA_NOPE:]
    wq_cat = jnp.concatenate([wq[:, :, :MLA_NOPE], _pad_cols(wq_rope, LANE), _pad_cols(wq_rope[:, :, perm], LANE)],
                             axis=-1).reshape(MLA_Q_LORA, MLA_HEADS * QH_W).astype(BF16)
    alog_row = _pad_cols(jnp.concatenate([dn_a_log[l].reshape(1, -1), jnp.zeros((1, 2 * DN_HEADS), F32)], -1), LANE)
    dtb_row = _pad_cols(jnp.concatenate([dn_dt_bias[l].reshape(1, -1), jnp.zeros((1, 2 * DN_HEADS), F32)], -1), LANE)
    rw = _pad_cols(router_w[l], LANE)
    rw_hi = rw.astype(BF16)
    rw_lo = (rw - rw_hi.astype(F32)).astype(BF16)
    rw_cat = jnp.concatenate([rw_hi, rw_lo], axis=-1)
    rb = jnp.concatenate([router_b[l], jnp.full((LANE - N_EXPERTS,), 0.0, F32)]).reshape(1, LANE)
    n_e, f2 = expert_b1.shape[1:]
    b1 = expert_b1[l].reshape(n_e, f2 // GLU_GROUP, LANE, 2).transpose(0, 1, 3, 2).reshape(n_e, 1, f2)
    return w_cat, wq_cat, alog_row, dtb_row, rw_cat, rw_hi, rb, b1


def kernel(x, c, ctx, c_ctx, ada_w, ada_b, norm1_g, norm2_g, w_in, dn_conv_w, dn_a_log, dn_dt_bias, dn_norm_g, sc_conv_w, mla_q_norm_g, mla_w_qb, mla_kv_norm_g, mla_w_kvb, w_branch_gate, b_branch_gate, w_branch_dn, w_branch_sc, w_branch_mla, w_out, router_w, router_b, expert_w1, expert_b1, expert_w2, expert_b2, final_norm_g):
    bsz, t_lat, d = x.shape
    t_ctx = ctx.shape[1]
    depth = ada_w.shape[0]
    assert t_ctx == ROW_TILE and t_lat % ROW_TILE == 0 and t_lat % GRID_W == 0
    t = t_ctx + t_lat

    cos_t, sin_t = _rope_tables(t_lat, t_ctx)
    n_mod_rows = -(-(bsz + 1) // 8) * 8
    cvec = jnp.concatenate([c, c_ctx[None], jnp.zeros((n_mod_rows - bsz - 1, d), F32)], axis=0)
    mods = _ada_mods(cvec, ada_w, ada_b)

    h = (x, ctx)
    row2 = lambda v: v.reshape(1, -1)
    for l in range(depth):
        mod_lat = mods[l, :bsz].reshape(bsz, 1, 6, d)
        mod_ctx = jnp.broadcast_to(mods[l, bsz].reshape(1, 1, 6, d), (bsz, 1, 6, d))
        mod = jnp.concatenate([mod_ctx, mod_lat], axis=1)
        (w_cat, wq_cat, alog_row, dtb_row, rw_cat, rw_hi, rb, b1) = _layer_weights(
            l, w_in, dn_a_log, dn_dt_bias, mla_w_qb, router_w, router_b, expert_b1)

        proj = _inproj(h, mod, row2(norm1_g[l]), w_cat, row2(mla_q_norm_g[l]), wq_cat,
                       row2(mla_kv_norm_g[l]), mla_w_kvb[l].astype(BF16), cos_t, sin_t)
        if l == 0:
            h = proj[-1]
        qkv, z, sc, ab, q, k, v = proj[:7]
        qkvn, ysc, gb = _prep(qkv, sc, ab, dn_conv_w[l], sc_conv_w[l], alog_row, dtb_row)
        o_f, o_b = _deltanet(qkvn, gb, t_ctx // DN_CHUNK)
        ymla = _mla(q, k, v)
        last = l == depth - 1
        j0 = 1 if last else 0
        hn, xm2, tg, lrank, info, cnt = _merge(h, mod, row2(norm1_g[l]), o_f, o_b, z, row2(dn_norm_g[l]), ysc, ymla,
                                 w_branch_gate[l].astype(BF16), row2(b_branch_gate[l]),
                                 w_branch_dn[l].astype(BF16), w_branch_sc[l].astype(BF16),
                                 w_branch_mla[l].astype(BF16), w_out[l].astype(BF16), row2(norm2_g[l]),
                                 rw_cat, rw_hi, rb, j0)
        runs, tails, block_expert, n_real, n_blocks = _route(info, cnt, bsz * (t - j0 * ROW_TILE) * TOP_K)
        xs = _dispatch(runs, tails, n_real, xm2, lrank, n_blocks * EXPERT_BLOCK)
        y_rows = _experts(block_expert, n_real, xs, expert_w1, b1, expert_w2, expert_b2[l][:, None, :], l)
        h = _combine(runs, lrank, y_rows, hn, mod, tg, row2(final_norm_g), last, j0)
    return h
```

```python
import functools
import math

import jax
import jax.numpy as jnp
from jax import lax
from jax.experimental import pallas as pl
from jax.experimental.pallas import tpu as pltpu

F32 = jnp.float32
BF16 = jnp.bfloat16

GRID_W = 64
NORM_EPS = 1e-6
L2_EPS = 1e-6
DN_HEADS = 4
DN_DK = 128
DN_DV = 128
DN_CHUNK = 64
SC_WIDTH = 512
MLA_HEADS = 4
MLA_Q_LORA = 256
MLA_KV_LORA = 128
MLA_NOPE = 128
MLA_ROPE = 64
MLA_V = 128
MLA_SCALE = (MLA_NOPE + MLA_ROPE) ** -0.5
Q_SCALE = MLA_SCALE * math.log2(math.e)
ROPE_BASE = 10000.0
N_EXPERTS = 32
TOP_K = 4
EXPERT_FF = 1024
SWIGLU_ALPHA = 1.702
SWIGLU_LIMIT = 7.0

LANE = 128
ROW_TILE = 256
HALO = 16
EXPERT_BLOCK = 512
PROJ_BATCH = 2
MERGE_BATCH = 2
DN_BATCH = 8
RUN_ALIGN = 8
LOCAL_ROWS = TOP_K * ROW_TILE + N_EXPERTS * RUN_ALIGN
N_RUNS = N_EXPERTS + 1
ADA_COLS = 512
V7X_VMEM_BYTES = 64 * 1024 * 1024
VMEM_LIMIT = V7X_VMEM_BYTES - 8 * 1024 * 1024

QKV_W = 3 * DN_HEADS * DN_DK
Z_W = DN_HEADS * DN_DV
SC3_W = 3 * SC_WIDTH
KVA_W = 3 * LANE
AB_W = LANE
IN_W = QKV_W + Z_W + SC3_W + MLA_Q_LORA + KVA_W + AB_W
QH_W = 3 * LANE
QK_W = 2 * LANE


def _cparams(sem):
    return pltpu.CompilerParams(dimension_semantics=sem, vmem_limit_bytes=VMEM_LIMIT)


def _dot(a, b):
    return jnp.dot(a, b, preferred_element_type=F32)


def _dot_nt(a, b):
    return lax.dot_general(a, b, (((1,), (1,)), ((), ())), preferred_element_type=F32)


def _rms(x, g):
    return x * lax.rsqrt(jnp.mean(x * x, axis=-1, keepdims=True) + NORM_EPS) * g


def _sigmoid(x):
    return 1.0 / (1.0 + jnp.exp(-x))


def _ada_kernel(c_ref, w_ref, b_ref, o_ref):
    cv = c_ref[...]
    s = cv * _sigmoid(cv)
    o_ref[...] = _dot(s.astype(BF16), w_ref[...].astype(BF16)) + b_ref[...]


def _ada_mods(cvec, ada_w, ada_b):
    n_layers, d, d6 = ada_w.shape
    r = cvec.shape[0]
    tn = ADA_COLS
    return pl.pallas_call(
        _ada_kernel,
        grid=(n_layers, d6 // tn),
        in_specs=[pl.BlockSpec((r, d), lambda l, n: (0, 0)),
                  pl.BlockSpec((None, d, tn), lambda l, n: (l, 0, n)),
                  pl.BlockSpec((None, 1, tn), lambda l, n: (l, 0, n))],
        out_specs=pl.BlockSpec((None, r, tn), lambda l, n: (l, 0, n)),
        out_shape=jax.ShapeDtypeStruct((n_layers, r, d6), F32),
        compiler_params=_cparams(("arbitrary", "arbitrary")),
    )(cvec, ada_w, ada_b.reshape(n_layers, 1, d6))


def _inproj_kernel(split, *refs):
    if split:
        x_ref, ctx_ref, *refs, hout_ref = refs
    else:
        h_ref, *refs = refs
    (mod_ref, g1_ref, w_ref, gq_ref, wq_ref, gkv_ref, wkv_ref, cos_ref, sin_ref,
     qkv_ref, z_ref, sc_ref, ab_ref, q_ref, k_ref, v_ref) = refs
    nb, tm = qkv_ref.shape[0], qkv_ref.shape[1]
    if split:
        is_ctx = pl.program_id(1) == 0
        tiles = [jnp.where(is_ctx, ctx_ref[bb], x_ref[bb]) for bb in range(nb)]
        for bb in range(nb):
            hout_ref[bb] = tiles[bb]
    else:
        tiles = [h_ref[bb] for bb in range(nb)]
    xm = jnp.concatenate([_rms(tiles[bb], g1_ref[...]) * (1.0 + mod_ref[bb, 1:2, :]) + mod_ref[bb, 0:1, :]
                          for bb in range(nb)], axis=0).astype(BF16)
    p = _dot(xm, w_ref[...])
    rows = lambda bb: slice(bb * tm, (bb + 1) * tm)
    o = 0
    qa = p[:, o:o + MLA_Q_LORA]
    o += MLA_Q_LORA
    ckv = p[:, o:o + LANE]
    kr = p[:, o + LANE:o + 2 * LANE]
    krs = p[:, o + 2 * LANE:o + 3 * LANE]
    o += KVA_W
    for bb in range(nb):
        qkv_ref[bb] = p[rows(bb), o:o + QKV_W].astype(BF16)
        z_ref[bb] = p[rows(bb), o + QKV_W:o + QKV_W + Z_W].astype(BF16)
        sc_ref[bb] = p[rows(bb), o + QKV_W + Z_W:o + QKV_W + Z_W + SC3_W].astype(BF16)
        ab_ref[bb] = p[rows(bb), o + QKV_W + Z_W + SC3_W:o + QKV_W + Z_W + SC3_W + AB_W]

    cos = jnp.concatenate([cos_ref[...]] * nb, axis=0)
    sin = jnp.concatenate([sin_ref[...]] * nb, axis=0)
    k_rope = (kr * cos + krs * sin).astype(BF16)
    qn = _rms(qa, gq_ref[...]).astype(BF16)
    qf = _dot(qn, wq_ref[...])
    kvn = _rms(ckv, gkv_ref[...]).astype(BF16)
    kv = _dot(kvn, wkv_ref[...])
    for hh in range(MLA_HEADS):
        b0 = hh * QH_W
        q_nope = (qf[:, b0:b0 + LANE] * Q_SCALE).astype(BF16)
        q_rope = qf[:, b0 + LANE:b0 + 2 * LANE] * cos + qf[:, b0 + 2 * LANE:b0 + 3 * LANE] * sin
        q_rope = (q_rope * Q_SCALE).astype(BF16)
        c0 = hh * (MLA_NOPE + MLA_V)
        k_nope = kv[:, c0:c0 + MLA_NOPE].astype(BF16)
        vv = kv[:, c0 + MLA_NOPE:c0 + MLA_NOPE + MLA_V].astype(BF16)
        for bb in range(nb):
            q_ref[bb, hh, :, 0:LANE] = q_nope[rows(bb)]
            q_ref[bb, hh, :, LANE:2 * LANE] = q_rope[rows(bb)]
            k_ref[bb, hh, :, 0:LANE] = k_nope[rows(bb)]
            k_ref[bb, hh, :, LANE:2 * LANE] = k_rope[rows(bb)]
            v_ref[bb, hh] = vv[rows(bb)]


def _inproj(h, mod, g1, w_cat, gq, wq_cat, gkv, wkv, cos_t, sin_t):
    split = isinstance(h, tuple)
    tm = ROW_TILE
    if split:
        x, ctx = h
        bsz, t_lat, d = x.shape
        t = t_lat + ctx.shape[1]
    else:
        bsz, t, d = h.shape
    nj = t // tm
    nb = PROJ_BATCH if bsz % PROJ_BATCH == 0 else 1
    row = lambda w: pl.BlockSpec((nb, tm, w), lambda b, j: (b, j, 0))
    if split:
        h_specs = [pl.BlockSpec((nb, tm, d), lambda b, j: (b, jnp.maximum(j - 1, 0), 0)),
                   pl.BlockSpec((nb, tm, d), lambda b, j: (b, 0, 0))]
        h_args = (x, ctx)
    else:
        h_specs, h_args = [row(d)], (h,)
    full = lambda a: pl.BlockSpec(a.shape, lambda b, j: (0,) * a.ndim, pipeline_mode=pl.Buffered(1))
    head = lambda w: pl.BlockSpec((nb, MLA_HEADS, tm, w), lambda b, j: (b, 0, j, 0))
    sds = jax.ShapeDtypeStruct
    return pl.pallas_call(
        functools.partial(_inproj_kernel, split),
        grid=(bsz // nb, nj),
        in_specs=h_specs + [
            pl.BlockSpec((nb, None, 6, d), lambda b, j: (b, jnp.minimum(j, 1), 0, 0)),
            full(g1), full(w_cat), full(gq), full(wq_cat), full(gkv), full(wkv),
            pl.BlockSpec((tm, LANE), lambda b, j: (j, 0)),
            pl.BlockSpec((tm, LANE), lambda b, j: (j, 0))],
        out_specs=[row(QKV_W), row(Z_W), row(SC3_W), row(AB_W), head(QK_W), head(QK_W), head(MLA_V)]
        + ([row(d)] if split else []),
        out_shape=[sds((bsz, t, QKV_W), BF16), sds((bsz, t, Z_W), BF16), sds((bsz, t, SC3_W), BF16),
                   sds((bsz, t, AB_W), F32),
                   sds((bsz, MLA_HEADS, t, QK_W), BF16), sds((bsz, MLA_HEADS, t, QK_W), BF16),
                   sds((bsz, MLA_HEADS, t, MLA_V), BF16)] + ([sds((bsz, t, d), F32)] if split else []),
        compiler_params=_cparams(("parallel", "arbitrary")),
    )(*h_args, mod, g1, w_cat, gq, wq_cat, gkv, wkv, cos_t, sin_t)


def _shift_rows(x, prev_row, next_row):
    tm = x.shape[0]
    rid = lax.broadcasted_iota(jnp.int32, x.shape, 0)
    xp = jnp.where(rid == 0, prev_row, pltpu.roll(x, 1, 0))
    xn = jnp.where(rid == tm - 1, next_row, pltpu.roll(x, tm - 1, 0))
    return xp, xn


def _prep_kernel(qkv_ref, qkv_p_ref, qkv_n_ref, sc_ref, sc_p_ref, sc_n_ref, ab_ref,
                 dnw_ref, scw_ref, alog_ref, dtb_ref, qkvn_ref, ysc_ref, gb_ref):
    j = pl.program_id(1)
    nj = pl.num_programs(1)
    pv = (j >= 2).astype(F32)
    nv = jnp.logical_and(j >= 1, j <= nj - 2).astype(F32)

    x = qkv_ref[...].astype(F32)
    xp, xn = _shift_rows(x, qkv_p_ref[HALO - 1:HALO, :].astype(F32) * pv, qkv_n_ref[0:1, :].astype(F32) * nv)
    y = xp * dnw_ref[0:1, :] + x * dnw_ref[1:2, :] + xn * dnw_ref[2:3, :]
    y = y * _sigmoid(y)
    nqk = DN_HEADS * DN_DK
    for g in range(2 * DN_HEADS):
        yy = y[:, g * DN_DK:(g + 1) * DN_DK]
        yy = yy * lax.rsqrt(jnp.sum(yy * yy, axis=-1, keepdims=True) + L2_EPS)
        if g < DN_HEADS:
            yy = yy * (DN_DK ** -0.5)
        qkvn_ref[:, g * DN_DK:(g + 1) * DN_DK] = yy.astype(BF16)
    qkvn_ref[:, 2 * nqk:] = y[:, 2 * nqk:].astype(BF16)

    s = sc_ref[...].astype(F32)
    sp = sc_p_ref[HALO - 1:HALO, :].astype(F32) * pv
    sn = sc_n_ref[0:1, :].astype(F32) * nv
    w = SC_WIDTH
    u = s[:, 2 * w:3 * w] * s[:, 0:w]
    up, un = _shift_rows(u, sp[:, 2 * w:3 * w] * sp[:, 0:w], sn[:, 2 * w:3 * w] * sn[:, 0:w])
    conv = up * scw_ref[0:1, :] + u * scw_ref[1:2, :] + un * scw_ref[2:3, :]
    ysc_ref[...] = (s[:, w:2 * w] * conv).astype(BF16)

    ab = ab_ref[...]
    sp_arg = ab + dtb_ref[...]
    softplus = jnp.maximum(sp_arg, 0.0) + jnp.log1p(jnp.exp(-jnp.abs(sp_arg)))
    gval = -jnp.exp(alog_ref[...]) * softplus
    lane = lax.broadcasted_iota(jnp.int32, ab.shape, 1)
    gb_ref[...] = jnp.where(lane < 2 * DN_HEADS, gval, _sigmoid(ab))


def _prep(qkv, sc, ab, dn_conv_w, sc_conv_w, alog_row, dtb_row):
    bsz, t, _ = qkv.shape
    tm = ROW_TILE
    nj = t // tm
    hb = tm // HALO
    nh = t // HALO
    row = lambda w: pl.BlockSpec((None, tm, w), lambda b, j: (b, j, 0))
    prev = lambda w: pl.BlockSpec((None, HALO, w), lambda b, j: (b, jnp.maximum(j * hb - 1, 0), 0))
    nxt = lambda w: pl.BlockSpec((None, HALO, w), lambda b, j: (b, jnp.minimum((j + 1) * hb, nh - 1), 0))
    full = lambda a: pl.BlockSpec(a.shape, lambda b, j: (0,) * a.ndim)
    sds = jax.ShapeDtypeStruct
    return pl.pallas_call(
        _prep_kernel,
        grid=(bsz, nj),
        in_specs=[row(QKV_W), prev(QKV_W), nxt(QKV_W), row(SC3_W), prev(SC3_W), nxt(SC3_W), row(AB_W),
                  full(dn_conv_w), full(sc_conv_w), full(alog_row), full(dtb_row)],
        out_specs=[row(QKV_W), row(SC_WIDTH), row(AB_W)],
        out_shape=[sds((bsz, t, QKV_W), BF16), sds((bsz, t, SC_WIDTH), BF16), sds((bsz, t, AB_W), F32)],
        compiler_params=_cparams(("parallel", "arbitrary")),
    )(qkv, qkv, qkv, sc, sc, sc, ab, dn_conv_w, sc_conv_w, alog_row, dtb_row)


DN_GROUP = 2


def _stack_heads(x, h0):
    return jnp.concatenate([x[:, (h0 + i) * LANE:(h0 + i + 1) * LANE] for i in range(DN_GROUP)], axis=0)


def _stack_cols(x, c0):
    return jnp.concatenate([x[:, c0 + i:c0 + i + 1] for i in range(DN_GROUP)], axis=0)


def _dn_direction(d, h0, x, gbt, tri, m_causal, m_strict, eye, s_ref):
    c = DN_CHUNK
    grp = range(DN_GROUP)
    nqk = DN_HEADS * DN_DK
    qst = _stack_heads(x[:, 0:nqk], h0)
    kst = _stack_heads(x[:, nqk:2 * nqk], h0)
    vst = _stack_heads(x[:, 2 * nqk:], h0)
    gc_all = jnp.dot(tri, gbt, preferred_element_type=F32, precision=lax.Precision.HIGHEST)
    col0 = d * DN_HEADS + h0
    gc = _stack_cols(gc_all, col0)
    beta = _stack_cols(gbt, 2 * DN_HEADS + col0)
    last = c - 1 if d == 0 else 0
    gl_heads = [gc_all[last:last + 1, col0 + i:col0 + i + 1] for i in grp]
    gl = jnp.concatenate([jnp.broadcast_to(g1, (c, 1)) for g1 in gl_heads], axis=0)
    kf = kst.astype(F32)
    kb = kf * beta
    kk = _dot_nt(kb.astype(BF16), kst)
    qk = _dot_nt(qst, kst)
    yield

    hc = DN_GROUP * c
    gmat = jnp.broadcast_to(gc, (hc, hc))
    dec = jnp.exp(jnp.minimum(gmat - gmat.T, 0.0))
    xpow = -(kk * jnp.where(m_strict > 0.5, dec, 0.0))
    tinv = eye + xpow
    intra = (qk * jnp.where(m_causal > 0.5, dec, 0.0)).astype(BF16)
    eg = jnp.exp(gc)
    rhs = jnp.concatenate([vst.astype(F32) * beta, kb * eg], axis=1).astype(BF16)
    qd = qst.astype(F32) * eg
    kdt = (kf * jnp.exp(gl - gc)).T.astype(BF16)
    for _ in range(int(math.log2(c)) - 1):
        xb = xpow.astype(BF16)
        xpow = _dot(xb, xb)
        yield
        tinv = tinv + _dot(tinv.astype(BF16), xpow.astype(BF16))
        yield
    sol = _dot(tinv.astype(BF16), rhs)
    yield
    u = sol[:, 0:DN_DV]
    w = sol[:, DN_DV:]

    s_cat = s_ref[...]
    wq = jnp.concatenate([w, qd], axis=0).astype(BF16)
    full = _dot(wq, s_cat.astype(BF16))
    yield
    ws = jnp.concatenate([full[i * c:(i + 1) * c, i * DN_DV:(i + 1) * DN_DV] for i in grp], axis=0)
    qs = jnp.concatenate([full[hc + i * c:hc + (i + 1) * c, i * DN_DV:(i + 1) * DN_DV] for i in grp], axis=0)
    vnew = u - ws
    o = qs + _dot(intra, vnew.astype(BF16))
    rhead = lax.broadcasted_iota(jnp.int32, (hc, DN_DV), 0) // c
    vblk = jnp.concatenate([jnp.where(rhead == i, vnew, 0.0) for i in grp], axis=1).astype(BF16)
    gt = jnp.concatenate([jnp.broadcast_to(jnp.exp(g1), (1, DN_DV)) for g1 in gl_heads], axis=1)
    s_ref[...] = s_cat * gt + _dot(kdt, vblk)
    yield
    return jnp.concatenate([o[i * c:(i + 1) * c] for i in grp], axis=1)


def _lockstep(gens):
    outs = [None] * len(gens)
    live = list(range(len(gens)))
    while live:
        for i in list(live):
            try:
                next(gens[i])
            except StopIteration as done:
                outs[i] = done.value
                live.remove(i)
    return outs


def _dn_kernel(xf_ref, gf_ref, xb_ref, gb_ref, trif_ref, trib_ref, mcf_ref, msf_ref, mcb_ref, msb_ref, eye_ref,
               of_ref, ob_ref, s_ref):
    @pl.when(pl.program_id(1) == 0)
    def _():
        s_ref[...] = jnp.zeros_like(s_ref)

    eye = eye_ref[...]
    gw = DN_GROUP * DN_DV
    gens, dsts = [], []
    for bb in range(xf_ref.shape[0]):
        for g in range(DN_HEADS // DN_GROUP):
            gens.append(_dn_direction(0, g * DN_GROUP, xf_ref[bb], gf_ref[bb], trif_ref[...], mcf_ref[...],
                                      msf_ref[...], eye, s_ref.at[bb, 0, :, g * gw:(g + 1) * gw]))
            dsts.append((of_ref, bb, g))
            gens.append(_dn_direction(1, g * DN_GROUP, xb_ref[bb], gb_ref[bb], trib_ref[...], mcb_ref[...],
                                      msb_ref[...], eye, s_ref.at[bb, 1, :, g * gw:(g + 1) * gw]))
            dsts.append((ob_ref, bb, g))
    for (o_ref, bb, g), o in zip(dsts, _lockstep(gens)):
        o_ref[bb, :, g * gw:(g + 1) * gw] = o.astype(BF16)


def _deltanet(qkvn, gb, n_ctx_chunks):
    bsz, t, _ = qkvn.shape
    c = DN_CHUNK
    nc = t // c
    hc = DN_GROUP * c
    bmap = lambda i: jnp.where(i < n_ctx_chunks, n_ctx_chunks - 1 - i, nc + n_ctx_chunks - 1 - i)
    ii = jnp.arange(c)
    tri_f = (ii[None, :] <= ii[:, None]).astype(F32)
    tri_b = (ii[None, :] >= ii[:, None]).astype(F32)
    r = jnp.arange(hc)
    same = (r[:, None] // c) == (r[None, :] // c)
    pi, pj = r[:, None] % c, r[None, :] % c
    mc_f = (same & (pj <= pi)).astype(F32)
    ms_f = (same & (pj < pi)).astype(F32)
    mc_b = (same & (pj >= pi)).astype(F32)
    ms_b = (same & (pj > pi)).astype(F32)
    eye = jnp.eye(hc, dtype=F32)
    full = lambda a: pl.BlockSpec(a.shape, lambda b, i: (0,) * a.ndim)
    nb = DN_BATCH if bsz % DN_BATCH == 0 else 1
    fw = lambda w: pl.BlockSpec((nb, c, w), lambda b, i: (b, i, 0))
    bw = lambda w: pl.BlockSpec((nb, c, w), lambda b, i: (b, bmap(i), 0))
    ow = DN_HEADS * DN_DV
    sds = jax.ShapeDtypeStruct
    return pl.pallas_call(
        _dn_kernel,
        grid=(bsz // nb, nc),
        in_specs=[fw(QKV_W), fw(AB_W), bw(QKV_W), bw(AB_W), full(tri_f), full(tri_b),
                  full(mc_f), full(ms_f), full(mc_b), full(ms_b), full(eye)],
        out_specs=[fw(ow), bw(ow)],
        out_shape=[sds((bsz, t, ow), BF16), sds((bsz, t, ow), BF16)],
        scratch_shapes=[pltpu.VMEM((nb, 2, DN_DK, DN_HEADS * DN_DV), F32)],
        compiler_params=_cparams(("arbitrary", "arbitrary")),
    )(qkvn, gb, qkvn, gb, tri_f, tri_b, mc_f, ms_f, mc_b, ms_b, eye)


MLA_GROUP = 4


def _softmax_av(q, k, v):
    s = _dot_nt(q, k)
    yield
    m = jnp.max(s, axis=-1, keepdims=True)
    p = jnp.exp2(s - m)
    l = jnp.sum(p, axis=-1, keepdims=True)
    p = p.astype(BF16)
    yield
    return _dot(p, v) / l


def _mla_kernel(q_ref, k_ref, v_ref, o_ref):
    j = pl.program_id(2)
    tc = q_ref.shape[1]

    def attend(n_keys):
        outs = _lockstep([_softmax_av(q_ref[g], k_ref[g, 0:n_keys, :], v_ref[g, 0:n_keys, :])
                          for g in range(MLA_GROUP)])
        for g, o in enumerate(outs):
            o_ref[:, g * MLA_V:(g + 1) * MLA_V] = o.astype(BF16)

    @pl.when(j == 0)
    def _():
        attend(tc)

    @pl.when(j > 0)
    def _():
        attend(k_ref.shape[1])


def _mla(q, k, v):
    bsz, nh, t, _ = q.shape
    tq = ROW_TILE
    g = MLA_GROUP
    return pl.pallas_call(
        _mla_kernel,
        grid=(bsz, nh // g, t // tq),
        in_specs=[pl.BlockSpec((None, g, tq, QK_W), lambda b, hp, j: (b, hp, j, 0)),
                  pl.BlockSpec((None, g, t, QK_W), lambda b, hp, j: (b, hp, 0, 0), pipeline_mode=pl.Buffered(1)),
                  pl.BlockSpec((None, g, t, MLA_V), lambda b, hp, j: (b, hp, 0, 0), pipeline_mode=pl.Buffered(1))],
        out_specs=pl.BlockSpec((None, tq, g * MLA_V), lambda b, hp, j: (b, j, hp)),
        out_shape=jax.ShapeDtypeStruct((bsz, t, nh * MLA_V), BF16),
        compiler_params=_cparams(("parallel", "parallel", "arbitrary")),
    )(q, k, v)


def _pack_halves(x):
    w = x.shape[1] // 2
    lo = lax.bitcast_convert_type(x[:, :w].astype(BF16).astype(F32), jnp.uint32)
    hi = lax.bitcast_convert_type(x[:, w:].astype(BF16).astype(F32), jnp.uint32)
    return (hi & jnp.uint32(0xFFFF0000)) | (lo >> 16)


def _unpack_halves(p):
    lo = lax.bitcast_convert_type(p << 16, F32)
    hi = lax.bitcast_convert_type(p & jnp.uint32(0xFFFF0000), F32)
    return lo, hi


def _merge_kernel(h_ref, mod_ref, g1_ref, of_ref, ob_ref, z_ref, dng_ref, ysc_ref, ymla_ref,
                  wg_ref, bg_ref, wdn_ref, wsc_ref, wmla_ref, wout_ref, g2_ref, rw_ref, rwh_ref, rb_ref, tril_ref, triu_ref,
                  hn_ref, xm2_ref, tg_ref, lrank_ref, info_ref, cnt_ref, cnt_scr):
    @pl.when(jnp.logical_and(pl.program_id(0) == 0, pl.program_id(1) == 0))
    def _():
        cnt_scr[...] = jnp.zeros_like(cnt_scr)

    nb, tm, d = h_ref.shape
    rows = lambda bb: slice(bb * tm, (bb + 1) * tm)
    cat = lambda ref: jnp.concatenate([ref[bb] for bb in range(nb)], axis=0)
    xm = jnp.concatenate([_rms(h_ref[bb], g1_ref[...]) * (1.0 + mod_ref[bb, 1:2, :]) + mod_ref[bb, 0:1, :]
                          for bb in range(nb)], axis=0).astype(BF16)
    o = cat(of_ref).astype(F32) + cat(ob_ref).astype(F32)
    z = cat(z_ref).astype(F32)
    parts = []
    for hh in range(DN_HEADS):
        oh = o[:, hh * DN_DV:(hh + 1) * DN_DV]
        zh = z[:, hh * DN_DV:(hh + 1) * DN_DV]
        parts.append(_rms(oh, dng_ref[...]) * (zh * _sigmoid(zh)))
    ydn = jnp.concatenate(parts, axis=1).astype(BF16)
    gates = _sigmoid(_dot(xm, wg_ref[...]) + bg_ref[...])
    merged = (gates[:, 0:d] * _dot(ydn, wdn_ref[...]) + gates[:, d:2 * d] * _dot(cat(ysc_ref), wsc_ref[...])
              + gates[:, 2 * d:3 * d] * _dot(cat(ymla_ref), wmla_ref[...]))
    y = _dot(merged.astype(BF16), wout_ref[...])
    xm2s = []
    for bb in range(nb):
        hn = h_ref[bb] + mod_ref[bb, 2:3, :] * y[rows(bb)]
        hn_ref[bb] = hn
        xm2s.append(_rms(hn, g2_ref[...]) * (1.0 + mod_ref[bb, 4:5, :]) + mod_ref[bb, 3:4, :])
    xm2 = jnp.concatenate(xm2s, axis=0)
    xh = xm2.astype(BF16)
    for bb in range(nb):
        xm2_ref[bb] = xh[rows(bb)]

    xl = (xm2 - xh.astype(F32)).astype(BF16)
    a = _dot(xh, rw_ref[...])
    logits = a[:, 0:LANE] + a[:, LANE:2 * LANE] + _dot(xl, rwh_ref[...]) + rb_ref[...]
    lane = lax.broadcasted_iota(jnp.int32, logits.shape, 1)
    work = jnp.where(lane < N_EXPERTS, logits, -jnp.inf)
    tv = jnp.zeros(logits.shape, F32)
    vals, idxs = [], []
    for kk in range(TOP_K):
        m = jnp.max(work, axis=-1, keepdims=True)
        idx = jnp.min(jnp.where(work == m, lane, LANE), axis=-1, keepdims=True)
        work = jnp.where(lane == idx, -jnp.inf, work)
        vals.append(m)
        idxs.append(idx)
    es = [jnp.exp(vv - vals[0]) for vv in vals]
    tot = es[0] + es[1] + es[2] + es[3]
    for kk in range(TOP_K):
        tv = jnp.where(lane == kk, es[kk] / tot, tv)
    sels = [(lane == idx).astype(F32) for idx in idxs]
    sel = sels[0] + sels[1] + sels[2] + sels[3]

    lane_t = lax.broadcasted_iota(jnp.int32, (tm, LANE), 1)
    sub = lax.broadcasted_iota(jnp.int32, (8, LANE), 0)
    for bb in range(nb):
        tg_ref[bb] = tv[rows(bb)]
        sel_t = sel[rows(bb)]
        tcnt = jnp.floor((jnp.sum(sel_t, axis=0, keepdims=True) + (RUN_ALIGN - 1.0)) * (1.0 / RUN_ALIGN)) * RUN_ALIGN
        tstart = _dot(jnp.broadcast_to(tcnt, (8, LANE)).astype(BF16), triu_ref[...])[0:1]
        lpos = _dot(tril_ref[...], sel_t.astype(BF16)) + tstart
        lrank = jnp.zeros((tm, LANE), F32)
        for kk in range(TOP_K):
            lp = jnp.sum(sels[kk][rows(bb)] * lpos, axis=-1, keepdims=True)
            lrank = jnp.where(lane_t == kk, lp, lrank)
        lrank_ref[bb] = lrank.astype(jnp.int32)
        carry = cnt_scr[...]
        info = jnp.where(sub == 0, tcnt, jnp.where(sub == 1, tstart, jnp.where(sub == 2, carry, 0.0)))
        info_ref[bb] = info.astype(jnp.int32)
        cnt_scr[...] = carry + tcnt
    cnt_ref[...] = cnt_scr[...]


def _merge(h, mod, g1, o_f, o_b, z, dng, ysc, ymla, wg, bg, wdn, wsc, wmla, wout, g2, rw_cat, rw_hi, rb, j0):
    bsz, t, d = h.shape
    tm = ROW_TILE
    nj = t // tm - j0
    to = nj * tm
    nb = MERGE_BATCH if bsz % MERGE_BATCH == 0 else 1
    row = lambda w: pl.BlockSpec((nb, tm, w), lambda b, j: (b, j + j0, 0))
    orow = lambda w: pl.BlockSpec((nb, tm, w), lambda b, j: (b, j, 0))
    full = lambda a: pl.BlockSpec(a.shape, lambda b, j: (0,) * a.ndim, pipeline_mode=pl.Buffered(1))
    ow = DN_HEADS * DN_DV
    sds = jax.ShapeDtypeStruct
    ii = jnp.arange(tm)
    tril = (ii[None, :] < ii[:, None]).astype(BF16)
    ee = jnp.arange(LANE)
    triu = (ee[:, None] < ee[None, :]).astype(BF16)
    tile_i32 = pl.BlockSpec((nb, None, 8, LANE), lambda b, j: (b, j, 0, 0))
    return pl.pallas_call(
        _merge_kernel,
        grid=(bsz // nb, nj),
        in_specs=[row(d), pl.BlockSpec((nb, None, 6, d), lambda b, j: (b, jnp.minimum(j + j0, 1), 0, 0)), full(g1),
                  row(ow), row(ow), row(Z_W), full(dng), row(SC_WIDTH), row(MLA_HEADS * MLA_V),
                  full(wg), full(bg), full(wdn), full(wsc), full(wmla), full(wout), full(g2),
                  full(rw_cat), full(rw_hi), full(rb), full(tril), full(triu)],
        out_specs=[orow(d), orow(d), orow(LANE), orow(LANE), tile_i32,
                   pl.BlockSpec((1, LANE), lambda b, j: (0, 0))],
        out_shape=[sds((bsz, to, d), F32), sds((bsz, to, d), BF16), sds((bsz, to, LANE), F32),
                   sds((bsz, to, LANE), jnp.int32), sds((bsz, nj, 8, LANE), jnp.int32), sds((1, LANE), F32)],
        scratch_shapes=[pltpu.VMEM((1, LANE), F32)],
        compiler_params=_cparams(("arbitrary", "arbitrary")),
    )(h, mod, g1, o_f, o_b, z, dng, ysc, ymla, wg, bg, wdn, wsc, wmla, wout, g2, rw_cat, rw_hi, rb, tril, triu)


GLU_GROUP = 2 * LANE


def _deinterleave(w_ref, perm, o_ref):
    for g in range(w_ref.shape[1] // GLU_GROUP):
        sl = slice(g * GLU_GROUP, (g + 1) * GLU_GROUP)
        o_ref[:, sl] = _dot(w_ref[:, sl].astype(BF16), perm).astype(BF16)


def _glu_perm():
    i = jnp.arange(GLU_GROUP)
    return (jnp.where(i % 2 == 0, i // 2, LANE + i // 2)[:, None] == i[None, :]).astype(BF16)


EXPERT_RUN_PIECES = tuple(RUN_ALIGN << i for i in range((ROW_TILE // RUN_ALIGN).bit_length()))
FILLER_RUN_PIECES = tuple(RUN_ALIGN << i for i in range((LOCAL_ROWS // RUN_ALIGN).bit_length()))


def _run_copies(run_ref, make_copy):
    def one_run(e, pieces):
        cnt, src, dst = run_ref[0, e], run_ref[1, e], run_ref[2, e]
        off = jnp.int32(0)
        for piece in pieces:
            has = (cnt & piece) != 0

            @pl.when(has)
            def _(off=off, piece=piece):
                make_copy(pl.multiple_of(src + off, RUN_ALIGN), pl.multiple_of(dst + off, RUN_ALIGN), piece).start()
            off = off + jnp.where(has, piece, 0)

    def per_expert(e, carry):
        one_run(e, EXPERT_RUN_PIECES)
        return carry
    lax.fori_loop(0, N_EXPERTS, per_expert, 0)
    one_run(N_EXPERTS, FILLER_RUN_PIECES)


TAIL_PIECES = tuple(RUN_ALIGN << i for i in range((EXPERT_BLOCK // RUN_ALIGN).bit_length() - 1))


def _tail_fill(tail_ref, zeros_ref, xs_hbm, sem):
    def pieces(e, act):
        dst, cnt = tail_ref[0, e], tail_ref[1, e]
        off = jnp.int32(0)
        for piece in TAIL_PIECES:
            has = (cnt & piece) != 0

            @pl.when(has)
            def _(off=off, piece=piece):
                act(pltpu.make_async_copy(zeros_ref.at[pl.ds(0, piece)],
                                          xs_hbm.at[pl.ds(pl.multiple_of(dst + off, RUN_ALIGN), piece)], sem))
            off = off + jnp.where(has, piece, 0)

    def start(e, carry):
        pieces(e, lambda cp: cp.start())
        return carry

    def wait(e, carry):
        pieces(e, lambda cp: cp.wait())
        return carry

    lax.fori_loop(0, N_EXPERTS, start, 0)
    lax.fori_loop(0, N_EXPERTS, wait, 0)


def _unused_fill(nreal_ref, zeros_ref, xs_hbm, sem):
    eb = EXPERT_BLOCK
    n_blocks = xs_hbm.shape[0] // eb

    def copy(i):
        return pltpu.make_async_copy(zeros_ref.at[pl.ds(0, eb)], xs_hbm.at[pl.ds(pl.multiple_of(i * eb, eb), eb)], sem)

    def start(i, carry):
        copy(i).start()
        return carry

    def wait(i, carry):
        copy(i).wait()
        return carry

    lax.fori_loop(nreal_ref[0], n_blocks, start, 0)
    lax.fori_loop(nreal_ref[0], n_blocks, wait, 0)


def _dispatch_kernel(run_ref, tail_ref, nreal_ref, x_ref, lrank_ref, xs_hbm, loc, sem):
    step = pl.program_id(0) * pl.num_programs(1) + pl.program_id(1)
    n_steps = pl.num_programs(0) * pl.num_programs(1)
    slot = step % 2
    tm = x_ref.shape[0]
    n_loc = LOCAL_ROWS

    def wait_runs(sl):
        pltpu.make_async_copy(loc.at[sl], xs_hbm.at[pl.ds(0, n_loc)], sem.at[sl]).wait()

    @pl.when(step == 0)
    def _():
        loc[0, 0:EXPERT_BLOCK, :] = jnp.zeros((EXPERT_BLOCK, loc.shape[2]), loc.dtype)
        _tail_fill(tail_ref, loc.at[0], xs_hbm, sem.at[0])
        _unused_fill(nreal_ref, loc.at[0], xs_hbm, sem.at[0])

    @pl.when(step >= 2)
    def _():
        wait_runs(slot)

    lr = lrank_ref[...].astype(F32).T
    rows = lax.broadcasted_iota(jnp.int32, (n_loc, tm), 0).astype(F32)
    onehot = jnp.zeros((n_loc, tm), F32)
    for kk in range(TOP_K):
        onehot = jnp.where(rows == lr[kk:kk + 1, :], 1.0, onehot)
    loc[slot] = _pack_halves(_dot(onehot.astype(BF16), x_ref[...]))
    _run_copies(run_ref, lambda s, d, n: pltpu.make_async_copy(
        loc.at[slot, pl.ds(s, n)], xs_hbm.at[pl.ds(d, n)], sem.at[slot]))

    @pl.when(step == n_steps - 1)
    def _():
        @pl.when(n_steps > 1)
        def _():
            wait_runs(1 - slot)
        wait_runs(slot)


def _dispatch(runs, tails, n_real, x2, lrank, n_rows):
    bsz, t, d = x2.shape
    tm = ROW_TILE
    w = d // 2
    return pl.pallas_call(
        _dispatch_kernel,
        grid=(bsz, t // tm),
        in_specs=[pl.BlockSpec((None, None, 3, N_RUNS), lambda b, j: (b, j, 0, 0), memory_space=pltpu.SMEM),
                  pl.BlockSpec(memory_space=pltpu.SMEM), pl.BlockSpec(memory_space=pltpu.SMEM),
                  pl.BlockSpec((None, tm, d), lambda b, j: (b, j, 0)),
                  pl.BlockSpec((None, tm, LANE), lambda b, j: (b, j, 0))],
        out_specs=pl.BlockSpec(memory_space=pl.ANY),
        out_shape=jax.ShapeDtypeStruct((n_rows, w), jnp.uint32),
        scratch_shapes=[pltpu.VMEM((2, LOCAL_ROWS, w), jnp.uint32), pltpu.SemaphoreType.DMA((2,))],
        compiler_params=_cparams(("arbitrary", "arbitrary")),
    )(runs, tails, n_real, x2, lrank)


def _expert_kernel(be_ref, nreal_ref, xs_ref, w1_ref, perm_ref, b1_ref, w2_ref, b2_ref, y_ref, w1p):
    i = pl.program_id(0)
    used = i < nreal_ref[0]

    @pl.when(jnp.logical_and(used, jnp.logical_or(i == 0, be_ref[i] != be_ref[jnp.maximum(i - 1, 0)])))
    def _():
        _deinterleave(w1_ref, perm_ref[...], w1p)

    @pl.when(used)
    def _():
        lo, hi = _unpack_halves(xs_ref[...])
        hdn = _dot(jnp.concatenate([lo.astype(BF16), hi.astype(BF16)], axis=1), w1p[...]) + b1_ref[...]
        acts = []
        for g in range(hdn.shape[1] // GLU_GROUP):
            glu = jnp.minimum(hdn[:, g * GLU_GROUP:g * GLU_GROUP + LANE], SWIGLU_LIMIT)
            lin = jnp.clip(hdn[:, g * GLU_GROUP + LANE:(g + 1) * GLU_GROUP], -SWIGLU_LIMIT, SWIGLU_LIMIT)
            acts.append((glu * _sigmoid(SWIGLU_ALPHA * glu) * (lin + 1.0)).astype(BF16))
        y_ref[...] = _pack_halves(_dot(jnp.concatenate(acts, axis=1), w2_ref[...].astype(BF16)) + b2_ref[...])

    @pl.when(i >= nreal_ref[0])
    def _():
        y_ref[...] = jnp.zeros_like(y_ref)


def _experts(block_expert, n_real, xs, expert_w1, b1, expert_w2, b2, l):
    n_blocks = block_expert.shape[0]
    f, d = expert_w2.shape[2:]
    eb = EXPERT_BLOCK
    perm = _glu_perm()
    wspec = lambda k, n: pl.BlockSpec((None, k, n), lambda i, be, nr: (be[i], 0, 0))
    lwspec = lambda k, n: pl.BlockSpec((None, None, k, n), lambda i, be, nr: (l, be[i], 0, 0))
    gs = pltpu.PrefetchScalarGridSpec(
        num_scalar_prefetch=2,
        grid=(n_blocks,),
        in_specs=[pl.BlockSpec((eb, d // 2), lambda i, be, nr: (jnp.minimum(i, nr[0] - 1), 0)),
                  lwspec(d, 2 * f), pl.BlockSpec(perm.shape, lambda i, be, nr: (0, 0)),
                  wspec(1, 2 * f), lwspec(f, d), wspec(1, d)],
        out_specs=pl.BlockSpec((eb, d // 2), lambda i, be, nr: (i, 0)),
        scratch_shapes=[pltpu.VMEM((d, 2 * f), BF16)],
    )
    return pl.pallas_call(
        _expert_kernel,
        grid_spec=gs,
        out_shape=jax.ShapeDtypeStruct((n_blocks * eb, d // 2), jnp.uint32),
        compiler_params=_cparams(("arbitrary",)),
    )(block_expert, n_real, xs, expert_w1, perm, b1, expert_w2, b2)


def _combine_kernel(final, run_ref, nrun_ref, lrank_ref, y_hbm, h_ref, mod_ref, tg_ref, fg_ref, o_ref, loc, sem):
    tm = h_ref.shape[0]
    n_loc = LOCAL_ROWS
    step = pl.program_id(0) * pl.num_programs(1) + pl.program_id(1)
    n_steps = pl.num_programs(0) * pl.num_programs(1)
    slot = step % 2

    def fetch(table_ref, sl):
        _run_copies(table_ref, lambda s, d, n: pltpu.make_async_copy(
            y_hbm.at[pl.ds(d, n)], loc.at[sl, pl.ds(s, n)], sem.at[sl]))

    @pl.when(step == 0)
    def _():
        fetch(run_ref, slot)

    @pl.when(step + 1 < n_steps)
    def _():
        fetch(nrun_ref, 1 - slot)

    lr = lrank_ref[...]
    tg = tg_ref[...]
    cols = lax.broadcasted_iota(jnp.int32, (tm, n_loc), 1)
    wsel = jnp.zeros((tm, n_loc), F32)
    for kk in range(TOP_K):
        wsel = jnp.where(cols == lr[:, kk:kk + 1], tg[:, kk:kk + 1], wsel)
    wsel = wsel.astype(BF16)
    pltpu.make_async_copy(y_hbm.at[pl.ds(0, n_loc)], loc.at[slot], sem.at[slot]).wait()
    lo, hi = _unpack_halves(loc[slot])
    y = jnp.concatenate([_dot(wsel, lo.astype(BF16)), _dot(wsel, hi.astype(BF16))], axis=1)
    hn = h_ref[...] + mod_ref[5:6, :] * y
    if final:
        hn = _rms(hn, fg_ref[...])
    o_ref[...] = hn


def _combine(runs, lrank, y_rows, h, mod, tg, fg, final, j0):
    bsz, t, d = h.shape
    tm = ROW_TILE
    nj = t // tm
    row = lambda w: pl.BlockSpec((None, tm, w), lambda b, j: (b, j, 0))
    run_spec = lambda imap: pl.BlockSpec((None, None, 3, N_RUNS), imap, memory_space=pltpu.SMEM)
    next_tile = lambda b, j: (jnp.where(j + 1 < nj, b, jnp.minimum(b + 1, bsz - 1)), jnp.where(j + 1 < nj, j + 1, 0),
                              0, 0)
    return pl.pallas_call(
        functools.partial(_combine_kernel, final),
        grid=(bsz, nj),
        in_specs=[run_spec(lambda b, j: (b, j, 0, 0)), run_spec(next_tile),
                  row(LANE), pl.BlockSpec(memory_space=pl.ANY),
                  row(d), pl.BlockSpec((None, None, 6, d), lambda b, j: (b, jnp.minimum(j + j0, 1), 0, 0)),
                  row(LANE), pl.BlockSpec(fg.shape, lambda b, j: (0, 0))],
        out_specs=row(d),
        out_shape=jax.ShapeDtypeStruct((bsz, t, d), F32),
        scratch_shapes=[pltpu.VMEM((2, LOCAL_ROWS, d // 2), jnp.uint32), pltpu.SemaphoreType.DMA((2,))],
        compiler_params=_cparams(("arbitrary", "arbitrary")),
    )(runs, runs, lrank, y_rows, h, mod, tg, fg)


def _route(info, cnt, n_assign):
    eb = EXPERT_BLOCK
    n_tiles = info.shape[0] * info.shape[1]
    max_rows = n_assign + n_tiles * N_EXPERTS * (RUN_ALIGN - 1)
    filler_blocks = -(-LOCAL_ROWS // eb)
    n_blocks = -(-max_rows // eb) + N_EXPERTS + 2 * filler_blocks
    tile_parity = (jnp.arange(n_tiles, dtype=jnp.int32) % 2).reshape(info.shape[0], info.shape[1], 1)
    filler_row = (n_blocks - 2 * filler_blocks + tile_parity * filler_blocks) * eb
    counts = cnt[0, :N_EXPERTS].astype(jnp.int32)
    padded = (counts + eb - 1) // eb * eb
    pad_end = jnp.cumsum(padded)
    pad_start = pad_end - padded
    tcnt, tstart, tcarry = (info[:, :, r, :N_EXPERTS] for r in range(3))
    used = jnp.sum(tcnt, axis=-1, keepdims=True)
    runs = jnp.stack([jnp.concatenate([tcnt, LOCAL_ROWS - used], axis=-1),
                      jnp.concatenate([tstart, used], axis=-1),
                      jnp.concatenate([pad_start + tcarry, filler_row], axis=-1)],
                     axis=2).astype(jnp.int32)
    block_row = jnp.arange(n_blocks, dtype=jnp.int32) * eb
    block_expert = jnp.minimum(jnp.sum((pad_end[None, :] <= block_row[:, None]).astype(jnp.int32), axis=1),
                               N_EXPERTS - 1)
    n_real = (pad_end[-1] // eb).astype(jnp.int32).reshape(1)
    tails = jnp.stack([pad_start + counts, padded - counts]).astype(jnp.int32)
    return runs, tails, block_expert, n_real, n_blocks


def _rope_tables(t_lat, t_ctx):
    rows = t_lat // GRID_W
    row = jnp.repeat(jnp.arange(rows, dtype=F32), GRID_W)
    col = jnp.tile(jnp.arange(GRID_W, dtype=F32), rows)
    axis_dims = MLA_ROPE // 2
    inv = ROPE_BASE ** (-jnp.arange(0, axis_dims, 2, dtype=F32) / axis_dims)
    ar, ac = row[:, None] * inv, col[:, None] * inv
    cos64 = jnp.concatenate([jnp.cos(ar), jnp.cos(ar), jnp.cos(ac), jnp.cos(ac)], axis=-1)
    sin64 = jnp.concatenate([-jnp.sin(ar), jnp.sin(ar), -jnp.sin(ac), jnp.sin(ac)], axis=-1)
    pad = jnp.zeros((t_lat, LANE - MLA_ROPE), F32)
    cos_l = jnp.concatenate([cos64, pad], axis=-1)
    sin_l = jnp.concatenate([sin64, pad], axis=-1)
    cos_c = jnp.concatenate([jnp.ones((t_ctx, MLA_ROPE), F32), jnp.zeros((t_ctx, LANE - MLA_ROPE), F32)], axis=-1)
    sin_c = jnp.zeros((t_ctx, LANE), F32)
    return jnp.concatenate([cos_c, cos_l], axis=0), jnp.concatenate([sin_c, sin_l], axis=0)


def _rope_swap_perm():
    half = MLA_ROPE // 4
    perm = []
    for a in range(2):
        base = a * 2 * half
        perm += list(range(base + half, base + 2 * half)) + list(range(base, base + half))
    return jnp.array(perm, jnp.int32)


def _pad_cols(w, width):
    return jnp.concatenate([w, jnp.zeros(w.shape[:-1] + (width - w.shape[-1],), w.dtype)], axis=-1)


def _layer_weights(l, w_in, dn_a_log, dn_dt_bias, mla_w_qb, router_w, router_b, expert_b1):
    nqk = DN_HEADS * DN_DK
    wi = w_in[l]
    o = 0
    dn_qkv = wi[:, o:o + QKV_W]; o += QKV_W
    dn_z = wi[:, o:o + Z_W]; o += Z_W
    dn_ab = wi[:, o:o + 4 * DN_HEADS]; o += 4 * DN_HEADS
    sc3 = wi[:, o:o + SC3_W]; o += SC3_W
    qa = wi[:, o:o + MLA_Q_LORA]; o += MLA_Q_LORA
    ckv = wi[:, o:o + MLA_KV_LORA]; o += MLA_KV_LORA
    kr = wi[:, o:o + MLA_ROPE]
    perm = _rope_swap_perm()
    w_cat = jnp.concatenate([qa, ckv, _pad_cols(kr, LANE), _pad_cols(kr[:, perm], LANE), dn_qkv, dn_z, sc3,
                             _pad_cols(dn_ab, LANE)], axis=-1).astype(BF16)
    dq = MLA_NOPE + MLA_ROPE
    wq = mla_w_qb[l].reshape(MLA_Q_LORA, MLA_HEADS, dq)
    wq_rope = wq[:, :, MLA_NOPE:]
    wq_cat = jnp.concatenate([wq[:, :, :MLA_NOPE], _pad_cols(wq_rope, LANE), _pad_cols(wq_rope[:, :, perm], LANE)],
                             axis=-1).reshape(MLA_Q_LORA, MLA_HEADS * QH_W).astype(BF16)
    alog_row = _pad_cols(jnp.concatenate([dn_a_log[l].reshape(1, -1), jnp.zeros((1, 2 * DN_HEADS), F32)], -1), LANE)
    dtb_row = _pad_cols(jnp.concatenate([dn_dt_bias[l].reshape(1, -1), jnp.zeros((1, 2 * DN_HEADS), F32)], -1), LANE)
    rw = _pad_cols(router_w[l], LANE)
    rw_hi = rw.astype(BF16)
    rw_lo = (rw - rw_hi.astype(F32)).astype(BF16)
    rw_cat = jnp.concatenate([rw_hi, rw_lo], axis=-1)
    rb = jnp.concatenate([router_b[l], jnp.full((LANE - N_EXPERTS,), 0.0, F32)]).reshape(1, LANE)
    n_e, f2 = expert_b1.shape[1:]
    b1 = expert_b1[l].reshape(n_e, f2 // GLU_GROUP, LANE, 2).transpose(0, 1, 3, 2).reshape(n_e, 1, f2)
    return w_cat, wq_cat, alog_row, dtb_row, rw_cat, rw_hi, rb, b1


def kernel(x, c, ctx, c_ctx, ada_w, ada_b, norm1_g, norm2_g, w_in, dn_conv_w, dn_a_log, dn_dt_bias, dn_norm_g, sc_conv_w, mla_q_norm_g, mla_w_qb, mla_kv_norm_g, mla_w_kvb, w_branch_gate, b_branch_gate, w_branch_dn, w_branch_sc, w_branch_mla, w_out, router_w, router_b, expert_w1, expert_b1, expert_w2, expert_b2, final_norm_g):
    bsz, t_lat, d = x.shape
    t_ctx = ctx.shape[1]
    depth = ada_w.shape[0]
    assert t_ctx == ROW_TILE and t_lat % ROW_TILE == 0 and t_lat % GRID_W == 0
    t = t_ctx + t_lat

    cos_t, sin_t = _rope_tables(t_lat, t_ctx)
    n_mod_rows = -(-(bsz + 1) // 8) * 8
    cvec = jnp.concatenate([c, c_ctx[None], jnp.zeros((n_mod_rows - bsz - 1, d), F32)], axis=0)
    mods = _ada_mods(cvec, ada_w, ada_b)

    h = (x, ctx)
    row2 = lambda v: v.reshape(1, -1)
    for l in range(depth):
        mod_lat = mods[l, :bsz].reshape(bsz, 1, 6, d)
        mod_ctx = jnp.broadcast_to(mods[l, bsz].reshape(1, 1, 6, d), (bsz, 1, 6, d))
        mod = jnp.concatenate([mod_ctx, mod_lat], axis=1)
        (w_cat, wq_cat, alog_row, dtb_row, rw_cat, rw_hi, rb, b1) = _layer_weights(
            l, w_in, dn_a_log, dn_dt_bias, mla_w_qb, router_w, router_b, expert_b1)

        proj = _inproj(h, mod, row2(norm1_g[l]), w_cat, row2(mla_q_norm_g[l]), wq_cat,
                       row2(mla_kv_norm_g[l]), mla_w_kvb[l].astype(BF16), cos_t, sin_t)
        if l == 0:
            h = proj[-1]
        qkv, z, sc, ab, q, k, v = proj[:7]
        qkvn, ysc, gb = _prep(qkv, sc, ab, dn_conv_w[l], sc_conv_w[l], alog_row, dtb_row)
        o_f, o_b = _deltanet(qkvn, gb, t_ctx // DN_CHUNK)
        ymla = _mla(q, k, v)
        last = l == depth - 1
        j0 = 1 if last else 0
        hn, xm2, tg, lrank, info, cnt = _merge(h, mod, row2(norm1_g[l]), o_f, o_b, z, row2(dn_norm_g[l]), ysc, ymla,
                                 w_branch_gate[l].astype(BF16), row2(b_branch_gate[l]),
                                 w_branch_dn[l].astype(BF16), w_branch_sc[l].astype(BF16),
                                 w_branch_mla[l].astype(BF16), w_out[l].astype(BF16), row2(norm2_g[l]),
                                 rw_cat, rw_hi, rb, j0)
        runs, tails, block_expert, n_real, n_blocks = _route(info, cnt, bsz * (t - j0 * ROW_TILE) * TOP_K)
        xs = _dispatch(runs, tails, n_real, xm2, lrank, n_blocks * EXPERT_BLOCK)
        y_rows = _experts(block_expert, n_real, xs, expert_w1, b1, expert_w2, expert_b2[l][:, None, :], l)
        h = _combine(runs, lrank, y_rows, hn, mod, tg, row2(final_norm_g), last, j0)
    return h
```

```python
import functools
import math

import jax
import jax.numpy as jnp
from jax import lax
from jax.experimental import pallas as pl
from jax.experimental.pallas import tpu as pltpu

F32 = jnp.float32
BF16 = jnp.bfloat16

GRID_W = 64
NORM_EPS = 1e-6
L2_EPS = 1e-6
DN_HEADS = 4
DN_DK = 128
DN_DV = 128
DN_CHUNK = 64
SC_WIDTH = 512
MLA_HEADS = 4
MLA_Q_LORA = 256
MLA_KV_LORA = 128
MLA_NOPE = 128
MLA_ROPE = 64
MLA_V = 128
MLA_SCALE = (MLA_NOPE + MLA_ROPE) ** -0.5
Q_SCALE = MLA_SCALE * math.log2(math.e)
ROPE_BASE = 10000.0
N_EXPERTS = 32
TOP_K = 4
EXPERT_FF = 1024
SWIGLU_ALPHA = 1.702
SWIGLU_LIMIT = 7.0

LANE = 128
ROW_TILE = 256
HALO = 16
EXPERT_BLOCK = 512
PROJ_BATCH = 2
MERGE_BATCH = 2
DN_BATCH = 8
RUN_ALIGN = 8
LOCAL_ROWS = TOP_K * ROW_TILE + N_EXPERTS * RUN_ALIGN
N_RUNS = N_EXPERTS + 1
ADA_COLS = 512
V7X_VMEM_BYTES = 64 * 1024 * 1024
VMEM_LIMIT = V7X_VMEM_BYTES - 8 * 1024 * 1024

QKV_W = 3 * DN_HEADS * DN_DK
Z_W = DN_HEADS * DN_DV
SC3_W = 3 * SC_WIDTH
KVA_W = 3 * LANE
AB_W = LANE
IN_W = QKV_W + Z_W + SC3_W + MLA_Q_LORA + KVA_W + AB_W
QH_W = 3 * LANE
QK_W = 2 * LANE


def _cparams(sem):
    return pltpu.CompilerParams(dimension_semantics=sem, vmem_limit_bytes=VMEM_LIMIT)


def _dot(a, b):
    return jnp.dot(a, b, preferred_element_type=F32)


def _dot_nt(a, b):
    return lax.dot_general(a, b, (((1,), (1,)), ((), ())), preferred_element_type=F32)


def _rms(x, g):
    return x * lax.rsqrt(jnp.mean(x * x, axis=-1, keepdims=True) + NORM_EPS) * g


def _sigmoid(x):
    return 1.0 / (1.0 + jnp.exp(-x))


def _ada_kernel(c_ref, w_ref, b_ref, o_ref):
    cv = c_ref[...]
    s = cv * _sigmoid(cv)
    o_ref[...] = _dot(s.astype(BF16), w_ref[...].astype(BF16)) + b_ref[...]


def _ada_mods(cvec, ada_w, ada_b):
    n_layers, d, d6 = ada_w.shape
    r = cvec.shape[0]
    tn = ADA_COLS
    return pl.pallas_call(
        _ada_kernel,
        grid=(n_layers, d6 // tn),
        in_specs=[pl.BlockSpec((r, d), lambda l, n: (0, 0)),
                  pl.BlockSpec((None, d, tn), lambda l, n: (l, 0, n)),
                  pl.BlockSpec((None, 1, tn), lambda l, n: (l, 0, n))],
        out_specs=pl.BlockSpec((None, r, tn), lambda l, n: (l, 0, n)),
        out_shape=jax.ShapeDtypeStruct((n_layers, r, d6), F32),
        compiler_params=_cparams(("arbitrary", "arbitrary")),
    )(cvec, ada_w, ada_b.reshape(n_layers, 1, d6))


def _inproj_kernel(split, *refs):
    if split:
        x_ref, ctx_ref, *refs, hout_ref = refs
    else:
        h_ref, *refs = refs
    (mod_ref, g1_ref, w_ref, gq_ref, wq_ref, gkv_ref, wkv_ref, cos_ref, sin_ref,
     qkv_ref, z_ref, sc_ref, ab_ref, q_ref, k_ref, v_ref) = refs
    nb, tm = qkv_ref.shape[0], qkv_ref.shape[1]
    if split:
        is_ctx = pl.program_id(1) == 0
        tiles = [jnp.where(is_ctx, ctx_ref[bb], x_ref[bb]) for bb in range(nb)]
        for bb in range(nb):
            hout_ref[bb] = tiles[bb]
    else:
        tiles = [h_ref[bb] for bb in range(nb)]
    xm = jnp.concatenate([_rms(tiles[bb], g1_ref[...]) * (1.0 + mod_ref[bb, 1:2, :]) + mod_ref[bb, 0:1, :]
                          for bb in range(nb)], axis=0).astype(BF16)
    p = _dot(xm, w_ref[...])
    rows = lambda bb: slice(bb * tm, (bb + 1) * tm)
    o = 0
    qa = p[:, o:o + MLA_Q_LORA]
    o += MLA_Q_LORA
    ckv = p[:, o:o + LANE]
    kr = p[:, o + LANE:o + 2 * LANE]
    krs = p[:, o + 2 * LANE:o + 3 * LANE]
    o += KVA_W
    for bb in range(nb):
        qkv_ref[bb] = p[rows(bb), o:o + QKV_W].astype(BF16)
        z_ref[bb] = p[rows(bb), o + QKV_W:o + QKV_W + Z_W].astype(BF16)
        sc_ref[bb] = p[rows(bb), o + QKV_W + Z_W:o + QKV_W + Z_W + SC3_W].astype(BF16)
        ab_ref[bb] = p[rows(bb), o + QKV_W + Z_W + SC3_W:o + QKV_W + Z_W + SC3_W + AB_W]

    cos = jnp.concatenate([cos_ref[...]] * nb, axis=0)
    sin = jnp.concatenate([sin_ref[...]] * nb, axis=0)
    k_rope = (kr * cos + krs * sin).astype(BF16)
    qn = _rms(qa, gq_ref[...]).astype(BF16)
    qf = _dot(qn, wq_ref[...])
    kvn = _rms(ckv, gkv_ref[...]).astype(BF16)
    kv = _dot(kvn, wkv_ref[...])
    for hh in range(MLA_HEADS):
        b0 = hh * QH_W
        q_nope = (qf[:, b0:b0 + LANE] * Q_SCALE).astype(BF16)
        q_rope = qf[:, b0 + LANE:b0 + 2 * LANE] * cos + qf[:, b0 + 2 * LANE:b0 + 3 * LANE] * sin
        q_rope = (q_rope * Q_SCALE).astype(BF16)
        c0 = hh * (MLA_NOPE + MLA_V)
        k_nope = kv[:, c0:c0 + MLA_NOPE].astype(BF16)
        vv = kv[:, c0 + MLA_NOPE:c0 + MLA_NOPE + MLA_V].astype(BF16)
        for bb in range(nb):
            q_ref[bb, hh, :, 0:LANE] = q_nope[rows(bb)]
            q_ref[bb, hh, :, LANE:2 * LANE] = q_rope[rows(bb)]
            k_ref[bb, hh, :, 0:LANE] = k_nope[rows(bb)]
            k_ref[bb, hh, :, LANE:2 * LANE] = k_rope[rows(bb)]
            v_ref[bb, hh] = vv[rows(bb)]


def _inproj(h, mod, g1, w_cat, gq, wq_cat, gkv, wkv, cos_t, sin_t):
    split = isinstance(h, tuple)
    tm = ROW_TILE
    if split:
        x, ctx = h
        bsz, t_lat, d = x.shape
        t = t_lat + ctx.shape[1]
    else:
        bsz, t, d = h.shape
    nj = t // tm
    nb = PROJ_BATCH if bsz % PROJ_BATCH == 0 else 1
    row = lambda w: pl.BlockSpec((nb, tm, w), lambda b, j: (b, j, 0))
    if split:
        h_specs = [pl.BlockSpec((nb, tm, d), lambda b, j: (b, jnp.maximum(j - 1, 0), 0)),
                   pl.BlockSpec((nb, tm, d), lambda b, j: (b, 0, 0))]
        h_args = (x, ctx)
    else:
        h_specs, h_args = [row(d)], (h,)
    full = lambda a: pl.BlockSpec(a.shape, lambda b, j: (0,) * a.ndim, pipeline_mode=pl.Buffered(1))
    head = lambda w: pl.BlockSpec((nb, MLA_HEADS, tm, w), lambda b, j: (b, 0, j, 0))
    sds = jax.ShapeDtypeStruct
    return pl.pallas_call(
        functools.partial(_inproj_kernel, split),
        grid=(bsz // nb, nj),
        in_specs=h_specs + [
            pl.BlockSpec((nb, None, 6, d), lambda b, j: (b, jnp.minimum(j, 1), 0, 0)),
            full(g1), full(w_cat), full(gq), full(wq_cat), full(gkv), full(wkv),
            pl.BlockSpec((tm, LANE), lambda b, j: (j, 0)),
            pl.BlockSpec((tm, LANE), lambda b, j: (j, 0))],
        out_specs=[row(QKV_W), row(Z_W), row(SC3_W), row(AB_W), head(QK_W), head(QK_W), head(MLA_V)]
        + ([row(d)] if split else []),
        out_shape=[sds((bsz, t, QKV_W), BF16), sds((bsz, t, Z_W), BF16), sds((bsz, t, SC3_W), BF16),
                   sds((bsz, t, AB_W), F32),
                   sds((bsz, MLA_HEADS, t, QK_W), BF16), sds((bsz, MLA_HEADS, t, QK_W), BF16),
                   sds((bsz, MLA_HEADS, t, MLA_V), BF16)] + ([sds((bsz, t, d), F32)] if split else []),
        compiler_params=_cparams(("parallel", "arbitrary")),
    )(*h_args, mod, g1, w_cat, gq, wq_cat, gkv, wkv, cos_t, sin_t)


def _shift_rows(x, prev_row, next_row):
    tm = x.shape[0]
    rid = lax.broadcasted_iota(jnp.int32, x.shape, 0)
    xp = jnp.where(rid == 0, prev_row, pltpu.roll(x, 1, 0))
    xn = jnp.where(rid == tm - 1, next_row, pltpu.roll(x, tm - 1, 0))
    return xp, xn


def _prep_kernel(qkv_ref, qkv_p_ref, qkv_n_ref, sc_ref, sc_p_ref, sc_n_ref, ab_ref,
                 dnw_ref, scw_ref, alog_ref, dtb_ref, qkvn_ref, ysc_ref, gb_ref):
    j = pl.program_id(1)
    nj = pl.num_programs(1)
    pv = (j >= 2).astype(F32)
    nv = jnp.logical_and(j >= 1, j <= nj - 2).astype(F32)

    x = qkv_ref[...].astype(F32)
    xp, xn = _shift_rows(x, qkv_p_ref[HALO - 1:HALO, :].astype(F32) * pv, qkv_n_ref[0:1, :].astype(F32) * nv)
    y = xp * dnw_ref[0:1, :] + x * dnw_ref[1:2, :] + xn * dnw_ref[2:3, :]
    y = y * _sigmoid(y)
    nqk = DN_HEADS * DN_DK
    for g in range(2 * DN_HEADS):
        yy = y[:, g * DN_DK:(g + 1) * DN_DK]
        yy = yy * lax.rsqrt(jnp.sum(yy * yy, axis=-1, keepdims=True) + L2_EPS)
        if g < DN_HEADS:
            yy = yy * (DN_DK ** -0.5)
        qkvn_ref[:, g * DN_DK:(g + 1) * DN_DK] = yy.astype(BF16)
    qkvn_ref[:, 2 * nqk:] = y[:, 2 * nqk:].astype(BF16)

    s = sc_ref[...].astype(F32)
    sp = sc_p_ref[HALO - 1:HALO, :].astype(F32) * pv
    sn = sc_n_ref[0:1, :].astype(F32) * nv
    w = SC_WIDTH
    u = s[:, 2 * w:3 * w] * s[:, 0:w]
    up, un = _shift_rows(u, sp[:, 2 * w:3 * w] * sp[:, 0:w], sn[:, 2 * w:3 * w] * sn[:, 0:w])
    conv = up * scw_ref[0:1, :] + u * scw_ref[1:2, :] + un * scw_ref[2:3, :]
    ysc_ref[...] = (s[:, w:2 * w] * conv).astype(BF16)

    ab = ab_ref[...]
    sp_arg = ab + dtb_ref[...]
    softplus = jnp.maximum(sp_arg, 0.0) + jnp.log1p(jnp.exp(-jnp.abs(sp_arg)))
    gval = -jnp.exp(alog_ref[...]) * softplus
    lane = lax.broadcasted_iota(jnp.int32, ab.shape, 1)
    gb_ref[...] = jnp.where(lane < 2 * DN_HEADS, gval, _sigmoid(ab))


def _prep(qkv, sc, ab, dn_conv_w, sc_conv_w, alog_row, dtb_row):
    bsz, t, _ = qkv.shape
    tm = ROW_TILE
    nj = t // tm
    hb = tm // HALO
    nh = t // HALO
    row = lambda w: pl.BlockSpec((None, tm, w), lambda b, j: (b, j, 0))
    prev = lambda w: pl.BlockSpec((None, HALO, w), lambda b, j: (b, jnp.maximum(j * hb - 1, 0), 0))
    nxt = lambda w: pl.BlockSpec((None, HALO, w), lambda b, j: (b, jnp.minimum((j + 1) * hb, nh - 1), 0))
    full = lambda a: pl.BlockSpec(a.shape, lambda b, j: (0,) * a.ndim)
    sds = jax.ShapeDtypeStruct
    return pl.pallas_call(
        _prep_kernel,
        grid=(bsz, nj),
        in_specs=[row(QKV_W), prev(QKV_W), nxt(QKV_W), row(SC3_W), prev(SC3_W), nxt(SC3_W), row(AB_W),
                  full(dn_conv_w), full(sc_conv_w), full(alog_row), full(dtb_row)],
        out_specs=[row(QKV_W), row(SC_WIDTH), row(AB_W)],
        out_shape=[sds((bsz, t, QKV_W), BF16), sds((bsz, t, SC_WIDTH), BF16), sds((bsz, t, AB_W), F32)],
        compiler_params=_cparams(("parallel", "arbitrary")),
    )(qkv, qkv, qkv, sc, sc, sc, ab, dn_conv_w, sc_conv_w, alog_row, dtb_row)


DN_GROUP = 2


def _stack_heads(x, h0):
    return jnp.concatenate([x[:, (h0 + i) * LANE:(h0 + i + 1) * LANE] for i in range(DN_GROUP)], axis=0)


def _stack_cols(x, c0):
    return jnp.concatenate([x[:, c0 + i:c0 + i + 1] for i in range(DN_GROUP)], axis=0)


def _dn_direction(d, h0, x, gbt, tri, m_causal, m_strict, eye, s_ref):
    c = DN_CHUNK
    grp = range(DN_GROUP)
    nqk = DN_HEADS * DN_DK
    qst = _stack_heads(x[:, 0:nqk], h0)
    kst = _stack_heads(x[:, nqk:2 * nqk], h0)
    vst = _stack_heads(x[:, 2 * nqk:], h0)
    gc_all = jnp.dot(tri, gbt, preferred_element_type=F32, precision=lax.Precision.HIGHEST)
    col0 = d * DN_HEADS + h0
    gc = _stack_cols(gc_all, col0)
    beta = _stack_cols(gbt, 2 * DN_HEADS + col0)
    last = c - 1 if d == 0 else 0
    gl_heads = [gc_all[last:last + 1, col0 + i:col0 + i + 1] for i in grp]
    gl = jnp.concatenate([jnp.broadcast_to(g1, (c, 1)) for g1 in gl_heads], axis=0)
    kf = kst.astype(F32)
    kb = kf * beta
    kk = _dot_nt(kb.astype(BF16), kst)
    qk = _dot_nt(qst, kst)
    yield

    hc = DN_GROUP * c
    gmat = jnp.broadcast_to(gc, (hc, hc))
    dec = jnp.exp(jnp.minimum(gmat - gmat.T, 0.0))
    xpow = -(kk * jnp.where(m_strict > 0.5, dec, 0.0))
    tinv = eye + xpow
    intra = (qk * jnp.where(m_causal > 0.5, dec, 0.0)).astype(BF16)
    eg = jnp.exp(gc)
    rhs = jnp.concatenate([vst.astype(F32) * beta, kb * eg], axis=1).astype(BF16)
    qd = qst.astype(F32) * eg
    kdt = (kf * jnp.exp(gl - gc)).T.astype(BF16)
    for _ in range(int(math.log2(c)) - 1):
        xb = xpow.astype(BF16)
        xpow = _dot(xb, xb)
        yield
        tinv = tinv + _dot(tinv.astype(BF16), xpow.astype(BF16))
        yield
    sol = _dot(tinv.astype(BF16), rhs)
    yield
    u = sol[:, 0:DN_DV]
    w = sol[:, DN_DV:]

    s_cat = s_ref[...]
    wq = jnp.concatenate([w, qd], axis=0).astype(BF16)
    full = _dot(wq, s_cat.astype(BF16))
    yield
    ws = jnp.concatenate([full[i * c:(i + 1) * c, i * DN_DV:(i + 1) * DN_DV] for i in grp], axis=0)
    qs = jnp.concatenate([full[hc + i * c:hc + (i + 1) * c, i * DN_DV:(i + 1) * DN_DV] for i in grp], axis=0)
    vnew = u - ws
    o = qs + _dot(intra, vnew.astype(BF16))
    rhead = lax.broadcasted_iota(jnp.int32, (hc, DN_DV), 0) // c
    vblk = jnp.concatenate([jnp.where(rhead == i, vnew, 0.0) for i in grp], axis=1).astype(BF16)
    gt = jnp.concatenate([jnp.broadcast_to(jnp.exp(g1), (1, DN_DV)) for g1 in gl_heads], axis=1)
    s_ref[...] = s_cat * gt + _dot(kdt, vblk)
    yield
    return jnp.concatenate([o[i * c:(i + 1) * c] for i in grp], axis=1)


def _lockstep(gens):
    outs = [None] * len(gens)
    live = list(range(len(gens)))
    while live:
        for i in list(live):
            try:
                next(gens[i])
            except StopIteration as done:
                outs[i] = done.value
                live.remove(i)
    return outs


def _dn_kernel(xf_ref, gf_ref, xb_ref, gb_ref, trif_ref, trib_ref, mcf_ref, msf_ref, mcb_ref, msb_ref, eye_ref,
               of_ref, ob_ref, s_ref):
    @pl.when(pl.program_id(1) == 0)
    def _():
        s_ref[...] = jnp.zeros_like(s_ref)

    eye = eye_ref[...]
    gw = DN_GROUP * DN_DV
    gens, dsts = [], []
    for bb in range(xf_ref.shape[0]):
        for g in range(DN_HEADS // DN_GROUP):
            gens.append(_dn_direction(0, g * DN_GROUP, xf_ref[bb], gf_ref[bb], trif_ref[...], mcf_ref[...],
                                      msf_ref[...], eye, s_ref.at[bb, 0, :, g * gw:(g + 1) * gw]))
            dsts.append((of_ref, bb, g))
            gens.append(_dn_direction(1, g * DN_GROUP, xb_ref[bb], gb_ref[bb], trib_ref[...], mcb_ref[...],
                                      msb_ref[...], eye, s_ref.at[bb, 1, :, g * gw:(g + 1) * gw]))
            dsts.append((ob_ref, bb, g))
    for (o_ref, bb, g), o in zip(dsts, _lockstep(gens)):
        o_ref[bb, :, g * gw:(g + 1) * gw] = o.astype(BF16)


def _deltanet(qkvn, gb, n_ctx_chunks):
    bsz, t, _ = qkvn.shape
    c = DN_CHUNK
    nc = t // c
    hc = DN_GROUP * c
    bmap = lambda i: jnp.where(i < n_ctx_chunks, n_ctx_chunks - 1 - i, nc + n_ctx_chunks - 1 - i)
    ii = jnp.arange(c)
    tri_f = (ii[None, :] <= ii[:, None]).astype(F32)
    tri_b = (ii[None, :] >= ii[:, None]).astype(F32)
    r = jnp.arange(hc)
    same = (r[:, None] // c) == (r[None, :] // c)
    pi, pj = r[:, None] % c, r[None, :] % c
    mc_f = (same & (pj <= pi)).astype(F32)
    ms_f = (same & (pj < pi)).astype(F32)
    mc_b = (same & (pj >= pi)).astype(F32)
    ms_b = (same & (pj > pi)).astype(F32)
    eye = jnp.eye(hc, dtype=F32)
    full = lambda a: pl.BlockSpec(a.shape, lambda b, i: (0,) * a.ndim)
    nb = DN_BATCH if bsz % DN_BATCH == 0 else 1
    fw = lambda w: pl.BlockSpec((nb, c, w), lambda b, i: (b, i, 0))
    bw = lambda w: pl.BlockSpec((nb, c, w), lambda b, i: (b, bmap(i), 0))
    ow = DN_HEADS * DN_DV
    sds = jax.ShapeDtypeStruct
    return pl.pallas_call(
        _dn_kernel,
        grid=(bsz // nb, nc),
        in_specs=[fw(QKV_W), fw(AB_W), bw(QKV_W), bw(AB_W), full(tri_f), full(tri_b),
                  full(mc_f), full(ms_f), full(mc_b), full(ms_b), full(eye)],
        out_specs=[fw(ow), bw(ow)],
        out_shape=[sds((bsz, t, ow), BF16), sds((bsz, t, ow), BF16)],
        scratch_shapes=[pltpu.VMEM((nb, 2, DN_DK, DN_HEADS * DN_DV), F32)],
        compiler_params=_cparams(("arbitrary", "arbitrary")),
    )(qkvn, gb, qkvn, gb, tri_f, tri_b, mc_f, ms_f, mc_b, ms_b, eye)


MLA_GROUP = 4


def _softmax_av(q, k, v):
    s = _dot_nt(q, k)
    yield
    m = jnp.max(s, axis=-1, keepdims=True)
    p = jnp.exp2(s - m)
    l = jnp.sum(p, axis=-1, keepdims=True)
    p = p.astype(BF16)
    yield
    return _dot(p, v) / l


def _mla_kernel(q_ref, k_ref, v_ref, o_ref):
    j = pl.program_id(2)
    tc = q_ref.shape[1]

    def attend(n_keys):
        outs = _lockstep([_softmax_av(q_ref[g], k_ref[g, 0:n_keys, :], v_ref[g, 0:n_keys, :])
                          for g in range(MLA_GROUP)])
        for g, o in enumerate(outs):
            o_ref[:, g * MLA_V:(g + 1) * MLA_V] = o.astype(BF16)

    @pl.when(j == 0)
    def _():
        attend(tc)

    @pl.when(j > 0)
    def _():
        attend(k_ref.shape[1])


def _mla(q, k, v):
    bsz, nh, t, _ = q.shape
    tq = ROW_TILE
    g = MLA_GROUP
    return pl.pallas_call(
        _mla_kernel,
        grid=(bsz, nh // g, t // tq),
        in_specs=[pl.BlockSpec((None, g, tq, QK_W), lambda b, hp, j: (b, hp, j, 0)),
                  pl.BlockSpec((None, g, t, QK_W), lambda b, hp, j: (b, hp, 0, 0), pipeline_mode=pl.Buffered(1)),
                  pl.BlockSpec((None, g, t, MLA_V), lambda b, hp, j: (b, hp, 0, 0), pipeline_mode=pl.Buffered(1))],
        out_specs=pl.BlockSpec((None, tq, g * MLA_V), lambda b, hp, j: (b, j, hp)),
        out_shape=jax.ShapeDtypeStruct((bsz, t, nh * MLA_V), BF16),
        compiler_params=_cparams(("parallel", "parallel", "arbitrary")),
    )(q, k, v)


def _pack_halves(x):
    w = x.shape[1] // 2
    lo = lax.bitcast_convert_type(x[:, :w].astype(BF16).astype(F32), jnp.uint32)
    hi = lax.bitcast_convert_type(x[:, w:].astype(BF16).astype(F32), jnp.uint32)
    return (hi & jnp.uint32(0xFFFF0000)) | (lo >> 16)


def _unpack_halves(p):
    lo = lax.bitcast_convert_type(p << 16, F32)
    hi = lax.bitcast_convert_type(p & jnp.uint32(0xFFFF0000), F32)
    return lo, hi


def _merge_kernel(h_ref, mod_ref, g1_ref, of_ref, ob_ref, z_ref, dng_ref, ysc_ref, ymla_ref,
                  wg_ref, bg_ref, wdn_ref, wsc_ref, wmla_ref, wout_ref, g2_ref, rw_ref, rwh_ref, rb_ref, tril_ref, triu_ref,
                  hn_ref, xm2_ref, tg_ref, lrank_ref, info_ref, cnt_ref, cnt_scr):
    @pl.when(jnp.logical_and(pl.program_id(0) == 0, pl.program_id(1) == 0))
    def _():
        cnt_scr[...] = jnp.zeros_like(cnt_scr)

    nb, tm, d = h_ref.shape
    rows = lambda bb: slice(bb * tm, (bb + 1) * tm)
    cat = lambda ref: jnp.concatenate([ref[bb] for bb in range(nb)], axis=0)
    xm = jnp.concatenate([_rms(h_ref[bb], g1_ref[...]) * (1.0 + mod_ref[bb, 1:2, :]) + mod_ref[bb, 0:1, :]
                          for bb in range(nb)], axis=0).astype(BF16)
    o = cat(of_ref).astype(F32) + cat(ob_ref).astype(F32)
    z = cat(z_ref).astype(F32)
    parts = []
    for hh in range(DN_HEADS):
        oh = o[:, hh * DN_DV:(hh + 1) * DN_DV]
        zh = z[:, hh * DN_DV:(hh + 1) * DN_DV]
        parts.append(_rms(oh, dng_ref[...]) * (zh * _sigmoid(zh)))
    ydn = jnp.concatenate(parts, axis=1).astype(BF16)
    gates = _sigmoid(_dot(xm, wg_ref[...]) + bg_ref[...])
    merged = (gates[:, 0:d] * _dot(ydn, wdn_ref[...]) + gates[:, d:2 * d] * _dot(cat(ysc_ref), wsc_ref[...])
              + gates[:, 2 * d:3 * d] * _dot(cat(ymla_ref), wmla_ref[...]))
    y = _dot(merged.astype(BF16), wout_ref[...])
    xm2s = []
    for bb in range(nb):
        hn = h_ref[bb] + mod_ref[bb, 2:3, :] * y[rows(bb)]
        hn_ref[bb] = hn
        xm2s.append(_rms(hn, g2_ref[...]) * (1.0 + mod_ref[bb, 4:5, :]) + mod_ref[bb, 3:4, :])
    xm2 = jnp.concatenate(xm2s, axis=0)
    xh = xm2.astype(BF16)
    for bb in range(nb):
        xm2_ref[bb] = xh[rows(bb)]

    xl = (xm2 - xh.astype(F32)).astype(BF16)
    a = _dot(xh, rw_ref[...])
    logits = a[:, 0:LANE] + a[:, LANE:2 * LANE] + _dot(xl, rwh_ref[...]) + rb_ref[...]
    lane = lax.broadcasted_iota(jnp.int32, logits.shape, 1)
    work = jnp.where(lane < N_EXPERTS, logits, -jnp.inf)
    tv = jnp.zeros(logits.shape, F32)
    vals, idxs = [], []
    for kk in range(TOP_K):
        m = jnp.max(work, axis=-1, keepdims=True)
        idx = jnp.min(jnp.where(work == m, lane, LANE), axis=-1, keepdims=True)
        work = jnp.where(lane == idx, -jnp.inf, work)
        vals.append(m)
        idxs.append(idx)
    es = [jnp.exp(vv - vals[0]) for vv in vals]
    tot = es[0] + es[1] + es[2] + es[3]
    for kk in range(TOP_K):
        tv = jnp.where(lane == kk, es[kk] / tot, tv)
    sels = [(lane == idx).astype(F32) for idx in idxs]
    sel = sels[0] + sels[1] + sels[2] + sels[3]

    lane_t = lax.broadcasted_iota(jnp.int32, (tm, LANE), 1)
    sub = lax.broadcasted_iota(jnp.int32, (8, LANE), 0)
    for bb in range(nb):
        tg_ref[bb] = tv[rows(bb)]
        sel_t = sel[rows(bb)]
        tcnt = jnp.floor((jnp.sum(sel_t, axis=0, keepdims=True) + (RUN_ALIGN - 1.0)) * (1.0 / RUN_ALIGN)) * RUN_ALIGN
        tstart = _dot(jnp.broadcast_to(tcnt, (8, LANE)).astype(BF16), triu_ref[...])[0:1]
        lpos = _dot(tril_ref[...], sel_t.astype(BF16)) + tstart
        lrank = jnp.zeros((tm, LANE), F32)
        for kk in range(TOP_K):
            lp = jnp.sum(sels[kk][rows(bb)] * lpos, axis=-1, keepdims=True)
            lrank = jnp.where(lane_t == kk, lp, lrank)
        lrank_ref[bb] = lrank.astype(jnp.int32)
        carry = cnt_scr[...]
        info = jnp.where(sub == 0, tcnt, jnp.where(sub == 1, tstart, jnp.where(sub == 2, carry, 0.0)))
        info_ref[bb] = info.astype(jnp.int32)
        cnt_scr[...] = carry + tcnt
    cnt_ref[...] = cnt_scr[...]


def _merge(h, mod, g1, o_f, o_b, z, dng, ysc, ymla, wg, bg, wdn, wsc, wmla, wout, g2, rw_cat, rw_hi, rb, j0):
    bsz, t, d = h.shape
    tm = ROW_TILE
    nj = t // tm - j0
    to = nj * tm
    nb = MERGE_BATCH if bsz % MERGE_BATCH == 0 else 1
    row = lambda w: pl.BlockSpec((nb, tm, w), lambda b, j: (b, j + j0, 0))
    orow = lambda w: pl.BlockSpec((nb, tm, w), lambda b, j: (b, j, 0))
    full = lambda a: pl.BlockSpec(a.shape, lambda b, j: (0,) * a.ndim, pipeline_mode=pl.Buffered(1))
    ow = DN_HEADS * DN_DV
    sds = jax.ShapeDtypeStruct
    ii = jnp.arange(tm)
    tril = (ii[None, :] < ii[:, None]).astype(BF16)
    ee = jnp.arange(LANE)
    triu = (ee[:, None] < ee[None, :]).astype(BF16)
    tile_i32 = pl.BlockSpec((nb, None, 8, LANE), lambda b, j: (b, j, 0, 0))
    return pl.pallas_call(
        _merge_kernel,
        grid=(bsz // nb, nj),
        in_specs=[row(d), pl.BlockSpec((nb, None, 6, d), lambda b, j: (b, jnp.minimum(j + j0, 1), 0, 0)), full(g1),
                  row(ow), row(ow), row(Z_W), full(dng), row(SC_WIDTH), row(MLA_HEADS * MLA_V),
                  full(wg), full(bg), full(wdn), full(wsc), full(wmla), full(wout), full(g2),
                  full(rw_cat), full(rw_hi), full(rb), full(tril), full(triu)],
        out_specs=[orow(d), orow(d), orow(LANE), orow(LANE), tile_i32,
                   pl.BlockSpec((1, LANE), lambda b, j: (0, 0))],
        out_shape=[sds((bsz, to, d), F32), sds((bsz, to, d), BF16), sds((bsz, to, LANE), F32),
                   sds((bsz, to, LANE), jnp.int32), sds((bsz, nj, 8, LANE), jnp.int32), sds((1, LANE), F32)],
        scratch_shapes=[pltpu.VMEM((1, LANE), F32)],
        compiler_params=_cparams(("arbitrary", "arbitrary")),
    )(h, mod, g1, o_f, o_b, z, dng, ysc, ymla, wg, bg, wdn, wsc, wmla, wout, g2, rw_cat, rw_hi, rb, tril, triu)


GLU_GROUP = 2 * LANE


def _deinterleave(w_ref, perm, o_ref):
    for g in range(w_ref.shape[1] // GLU_GROUP):
        sl = slice(g * GLU_GROUP, (g + 1) * GLU_GROUP)
        o_ref[:, sl] = _dot(w_ref[:, sl].astype(BF16), perm).astype(BF16)


def _glu_perm():
    i = jnp.arange(GLU_GROUP)
    return (jnp.where(i % 2 == 0, i // 2, LANE + i // 2)[:, None] == i[None, :]).astype(BF16)


EXPERT_RUN_PIECES = tuple(RUN_ALIGN << i for i in range((ROW_TILE // RUN_ALIGN).bit_length()))
FILLER_RUN_PIECES = tuple(RUN_ALIGN << i for i in range((LOCAL_ROWS // RUN_ALIGN).bit_length()))


def _run_copies(run_ref, make_copy):
    def one_run(e, pieces):
        cnt, src, dst = run_ref[0, e], run_ref[1, e], run_ref[2, e]
        off = jnp.int32(0)
        for pi, piece in enumerate(pieces):
            has = (cnt & piece) != 0

            @pl.when(has)
            def _(off=off, piece=piece, pi=pi):
                make_copy(pl.multiple_of(src + off, RUN_ALIGN), pl.multiple_of(dst + off, RUN_ALIGN),
                          piece).start(priority=pi % 2)
            off = off + jnp.where(has, piece, 0)

    def per_expert(e, carry):
        one_run(e, EXPERT_RUN_PIECES)
        return carry
    lax.fori_loop(0, N_EXPERTS, per_expert, 0)
    one_run(N_EXPERTS, FILLER_RUN_PIECES)


TAIL_PIECES = tuple(RUN_ALIGN << i for i in range((EXPERT_BLOCK // RUN_ALIGN).bit_length() - 1))


def _tail_fill(tail_ref, zeros_ref, xs_hbm, sem):
    def pieces(e, act):
        dst, cnt = tail_ref[0, e], tail_ref[1, e]
        off = jnp.int32(0)
        for piece in TAIL_PIECES:
            has = (cnt & piece) != 0

            @pl.when(has)
            def _(off=off, piece=piece):
                act(pltpu.make_async_copy(zeros_ref.at[pl.ds(0, piece)],
                                          xs_hbm.at[pl.ds(pl.multiple_of(dst + off, RUN_ALIGN), piece)], sem))
            off = off + jnp.where(has, piece, 0)

    def start(e, carry):
        pieces(e, lambda cp: cp.start())
        return carry

    def wait(e, carry):
        pieces(e, lambda cp: cp.wait())
        return carry

    lax.fori_loop(0, N_EXPERTS, start, 0)
    lax.fori_loop(0, N_EXPERTS, wait, 0)


def _unused_fill(nreal_ref, zeros_ref, xs_hbm, sem):
    eb = EXPERT_BLOCK
    n_blocks = xs_hbm.shape[0] // eb

    def copy(i):
        return pltpu.make_async_copy(zeros_ref.at[pl.ds(0, eb)], xs_hbm.at[pl.ds(pl.multiple_of(i * eb, eb), eb)], sem)

    def start(i, carry):
        copy(i).start()
        return carry

    def wait(i, carry):
        copy(i).wait()
        return carry

    lax.fori_loop(nreal_ref[0], n_blocks, start, 0)
    lax.fori_loop(nreal_ref[0], n_blocks, wait, 0)


def _dispatch_kernel(run_ref, tail_ref, nreal_ref, x_ref, lrank_ref, xs_hbm, loc, sem):
    step = pl.program_id(0) * pl.num_programs(1) + pl.program_id(1)
    n_steps = pl.num_programs(0) * pl.num_programs(1)
    slot = step % 2
    tm = x_ref.shape[0]
    n_loc = LOCAL_ROWS

    def wait_runs(sl):
        pltpu.make_async_copy(loc.at[sl], xs_hbm.at[pl.ds(0, n_loc)], sem.at[sl]).wait()

    @pl.when(step == 0)
    def _():
        loc[0, 0:EXPERT_BLOCK, :] = jnp.zeros((EXPERT_BLOCK, loc.shape[2]), loc.dtype)
        _tail_fill(tail_ref, loc.at[0], xs_hbm, sem.at[0])
        _unused_fill(nreal_ref, loc.at[0], xs_hbm, sem.at[0])

    @pl.when(step >= 2)
    def _():
        wait_runs(slot)

    lr = lrank_ref[...].astype(F32).T
    rows = lax.broadcasted_iota(jnp.int32, (n_loc, tm), 0).astype(F32)
    onehot = jnp.zeros((n_loc, tm), F32)
    for kk in range(TOP_K):
        onehot = jnp.where(rows == lr[kk:kk + 1, :], 1.0, onehot)
    loc[slot] = _pack_halves(_dot(onehot.astype(BF16), x_ref[...]))
    _run_copies(run_ref, lambda s, d, n: pltpu.make_async_copy(
        loc.at[slot, pl.ds(s, n)], xs_hbm.at[pl.ds(d, n)], sem.at[slot]))

    @pl.when(step == n_steps - 1)
    def _():
        @pl.when(n_steps > 1)
        def _():
            wait_runs(1 - slot)
        wait_runs(slot)


def _dispatch(runs, tails, n_real, x2, lrank, n_rows):
    bsz, t, d = x2.shape
    tm = ROW_TILE
    w = d // 2
    return pl.pallas_call(
        _dispatch_kernel,
        grid=(bsz, t // tm),
        in_specs=[pl.BlockSpec((None, None, 3, N_RUNS), lambda b, j: (b, j, 0, 0), memory_space=pltpu.SMEM),
                  pl.BlockSpec(memory_space=pltpu.SMEM), pl.BlockSpec(memory_space=pltpu.SMEM),
                  pl.BlockSpec((None, tm, d), lambda b, j: (b, j, 0)),
                  pl.BlockSpec((None, tm, LANE), lambda b, j: (b, j, 0))],
        out_specs=pl.BlockSpec(memory_space=pl.ANY),
        out_shape=jax.ShapeDtypeStruct((n_rows, w), jnp.uint32),
        scratch_shapes=[pltpu.VMEM((2, LOCAL_ROWS, w), jnp.uint32), pltpu.SemaphoreType.DMA((2,))],
        compiler_params=_cparams(("arbitrary", "arbitrary")),
    )(runs, tails, n_real, x2, lrank)


def _expert_kernel(be_ref, nreal_ref, xs_ref, w1_ref, perm_ref, b1_ref, w2_ref, b2_ref, y_ref, w1p):
    i = pl.program_id(0)
    used = i < nreal_ref[0]

    @pl.when(jnp.logical_and(used, jnp.logical_or(i == 0, be_ref[i] != be_ref[jnp.maximum(i - 1, 0)])))
    def _():
        _deinterleave(w1_ref, perm_ref[...], w1p)

    @pl.when(used)
    def _():
        lo, hi = _unpack_halves(xs_ref[...])
        hdn = _dot(jnp.concatenate([lo.astype(BF16), hi.astype(BF16)], axis=1), w1p[...]) + b1_ref[...]
        acts = []
        for g in range(hdn.shape[1] // GLU_GROUP):
            glu = jnp.minimum(hdn[:, g * GLU_GROUP:g * GLU_GROUP + LANE], SWIGLU_LIMIT)
            lin = jnp.clip(hdn[:, g * GLU_GROUP + LANE:(g + 1) * GLU_GROUP], -SWIGLU_LIMIT, SWIGLU_LIMIT)
            acts.append((glu * _sigmoid(SWIGLU_ALPHA * glu) * (lin + 1.0)).astype(BF16))
        y_ref[...] = _pack_halves(_dot(jnp.concatenate(acts, axis=1), w2_ref[...].astype(BF16)) + b2_ref[...])

    @pl.when(i >= nreal_ref[0])
    def _():
        y_ref[...] = jnp.zeros_like(y_ref)


def _experts(block_expert, n_real, xs, expert_w1, b1, expert_w2, b2, l):
    n_blocks = block_expert.shape[0]
    f, d = expert_w2.shape[2:]
    eb = EXPERT_BLOCK
    perm = _glu_perm()
    wspec = lambda k, n: pl.BlockSpec((None, k, n), lambda i, be, nr: (be[i], 0, 0))
    lwspec = lambda k, n: pl.BlockSpec((None, None, k, n), lambda i, be, nr: (l, be[i], 0, 0))
    gs = pltpu.PrefetchScalarGridSpec(
        num_scalar_prefetch=2,
        grid=(n_blocks,),
        in_specs=[pl.BlockSpec((eb, d // 2), lambda i, be, nr: (jnp.minimum(i, nr[0] - 1), 0)),
                  lwspec(d, 2 * f), pl.BlockSpec(perm.shape, lambda i, be, nr: (0, 0)),
                  wspec(1, 2 * f), lwspec(f, d), wspec(1, d)],
        out_specs=pl.BlockSpec((eb, d // 2), lambda i, be, nr: (i, 0)),
        scratch_shapes=[pltpu.VMEM((d, 2 * f), BF16)],
    )
    return pl.pallas_call(
        _expert_kernel,
        grid_spec=gs,
        out_shape=jax.ShapeDtypeStruct((n_blocks * eb, d // 2), jnp.uint32),
        compiler_params=_cparams(("arbitrary",)),
    )(block_expert, n_real, xs, expert_w1, perm, b1, expert_w2, b2)


def _combine_kernel(final, run_ref, nrun_ref, lrank_ref, y_hbm, h_ref, mod_ref, tg_ref, fg_ref, o_ref, loc, sem):
    tm = h_ref.shape[0]
    n_loc = LOCAL_ROWS
    step = pl.program_id(0) * pl.num_programs(1) + pl.program_id(1)
    n_steps = pl.num_programs(0) * pl.num_programs(1)
    slot = step % 2

    def fetch(table_ref, sl):
        _run_copies(table_ref, lambda s, d, n: pltpu.make_async_copy(
            y_hbm.at[pl.ds(d, n)], loc.at[sl, pl.ds(s, n)], sem.at[sl]))

    @pl.when(step == 0)
    def _():
        fetch(run_ref, slot)

    @pl.when(step + 1 < n_steps)
    def _():
        fetch(nrun_ref, 1 - slot)

    lr = lrank_ref[...]
    tg = tg_ref[...]
    cols = lax.broadcasted_iota(jnp.int32, (tm, n_loc), 1)
    wsel = jnp.zeros((tm, n_loc), F32)
    for kk in range(TOP_K):
        wsel = jnp.where(cols == lr[:, kk:kk + 1], tg[:, kk:kk + 1], wsel)
    wsel = wsel.astype(BF16)
    pltpu.make_async_copy(y_hbm.at[pl.ds(0, n_loc)], loc.at[slot], sem.at[slot]).wait()
    lo, hi = _unpack_halves(loc[slot])
    y = jnp.concatenate([_dot(wsel, lo.astype(BF16)), _dot(wsel, hi.astype(BF16))], axis=1)
    hn = h_ref[...] + mod_ref[5:6, :] * y
    if final:
        hn = _rms(hn, fg_ref[...])
    o_ref[...] = hn


def _combine(runs, lrank, y_rows, h, mod, tg, fg, final, j0):
    bsz, t, d = h.shape
    tm = ROW_TILE
    nj = t // tm
    row = lambda w: pl.BlockSpec((None, tm, w), lambda b, j: (b, j, 0))
    run_spec = lambda imap: pl.BlockSpec((None, None, 3, N_RUNS), imap, memory_space=pltpu.SMEM)
    next_tile = lambda b, j: (jnp.where(j + 1 < nj, b, jnp.minimum(b + 1, bsz - 1)), jnp.where(j + 1 < nj, j + 1, 0),
                              0, 0)
    return pl.pallas_call(
        functools.partial(_combine_kernel, final),
        grid=(bsz, nj),
        in_specs=[run_spec(lambda b, j: (b, j, 0, 0)), run_spec(next_tile),
                  row(LANE), pl.BlockSpec(memory_space=pl.ANY),
                  row(d), pl.BlockSpec((None, None, 6, d), lambda b, j: (b, jnp.minimum(j + j0, 1), 0, 0)),
                  row(LANE), pl.BlockSpec(fg.shape, lambda b, j: (0, 0))],
        out_specs=row(d),
        out_shape=jax.ShapeDtypeStruct((bsz, t, d), F32),
        scratch_shapes=[pltpu.VMEM((2, LOCAL_ROWS, d // 2), jnp.uint32), pltpu.SemaphoreType.DMA((2,))],
        compiler_params=_cparams(("arbitrary", "arbitrary")),
    )(runs, runs, lrank, y_rows, h, mod, tg, fg)


def _route(info, cnt, n_assign):
    eb = EXPERT_BLOCK
    n_tiles = info.shape[0] * info.shape[1]
    max_rows = n_assign + n_tiles * N_EXPERTS * (RUN_ALIGN - 1)
    filler_blocks = -(-LOCAL_ROWS // eb)
    n_blocks = -(-max_rows // eb) + N_EXPERTS + 2 * filler_blocks
    tile_parity = (jnp.arange(n_tiles, dtype=jnp.int32) % 2).reshape(info.shape[0], info.shape[1], 1)
    filler_row = (n_blocks - 2 * filler_blocks + tile_parity * filler_blocks) * eb
    counts = cnt[0, :N_EXPERTS].astype(jnp.int32)
    padded = (counts + eb - 1) // eb * eb
    pad_end = jnp.cumsum(padded)
    pad_start = pad_end - padded
    tcnt, tstart, tcarry = (info[:, :, r, :N_EXPERTS] for r in range(3))
    used = jnp.sum(tcnt, axis=-1, keepdims=True)
    runs = jnp.stack([jnp.concatenate([tcnt, LOCAL_ROWS - used], axis=-1),
                      jnp.concatenate([tstart, used], axis=-1),
                      jnp.concatenate([pad_start + tcarry, filler_row], axis=-1)],
                     axis=2).astype(jnp.int32)
    block_row = jnp.arange(n_blocks, dtype=jnp.int32) * eb
    block_expert = jnp.minimum(jnp.sum((pad_end[None, :] <= block_row[:, None]).astype(jnp.int32), axis=1),
                               N_EXPERTS - 1)
    n_real = (pad_end[-1] // eb).astype(jnp.int32).reshape(1)
    tails = jnp.stack([pad_start + counts, padded - counts]).astype(jnp.int32)
    return runs, tails, block_expert, n_real, n_blocks


def _rope_tables(t_lat, t_ctx):
    rows = t_lat // GRID_W
    row = jnp.repeat(jnp.arange(rows, dtype=F32), GRID_W)
    col = jnp.tile(jnp.arange(GRID_W, dtype=F32), rows)
    axis_dims = MLA_ROPE // 2
    inv = ROPE_BASE ** (-jnp.arange(0, axis_dims, 2, dtype=F32) / axis_dims)
    ar, ac = row[:, None] * inv, col[:, None] * inv
    cos64 = jnp.concatenate([jnp.cos(ar), jnp.cos(ar), jnp.cos(ac), jnp.cos(ac)], axis=-1)
    sin64 = jnp.concatenate([-jnp.sin(ar), jnp.sin(ar), -jnp.sin(ac), jnp.sin(ac)], axis=-1)
    pad = jnp.zeros((t_lat, LANE - MLA_ROPE), F32)
    cos_l = jnp.concatenate([cos64, pad], axis=-1)
    sin_l = jnp.concatenate([sin64, pad], axis=-1)
    cos_c = jnp.concatenate([jnp.ones((t_ctx, MLA_ROPE), F32), jnp.zeros((t_ctx, LANE - MLA_ROPE), F32)], axis=-1)
    sin_c = jnp.zeros((t_ctx, LANE), F32)
    return jnp.concatenate([cos_c, cos_l], axis=0), jnp.concatenate([sin_c, sin_l], axis=0)


def _rope_swap_perm():
    half = MLA_ROPE // 4
    perm = []
    for a in range(2):
        base = a * 2 * half
        perm += list(range(base + half, base + 2 * half)) + list(range(base, base + half))
    return jnp.array(perm, jnp.int32)


def _pad_cols(w, width):
    return jnp.concatenate([w, jnp.zeros(w.shape[:-1] + (width - w.shape[-1],), w.dtype)], axis=-1)


def _layer_weights(l, w_in, dn_a_log, dn_dt_bias, mla_w_qb, router_w, router_b, expert_b1):
    nqk = DN_HEADS * DN_DK
    wi = w_in[l]
    o = 0
    dn_qkv = wi[:, o:o + QKV_W]; o += QKV_W
    dn_z = wi[:, o:o + Z_W]; o += Z_W
    dn_ab = wi[:, o:o + 4 * DN_HEADS]; o += 4 * DN_HEADS
    sc3 = wi[:, o:o + SC3_W]; o += SC3_W
    qa = wi[:, o:o + MLA_Q_LORA]; o += MLA_Q_LORA
    ckv = wi[:, o:o + MLA_KV_LORA]; o += MLA_KV_LORA
    kr = wi[:, o:o + MLA_ROPE]
    perm = _rope_swap_perm()
    w_cat = jnp.concatenate([qa, ckv, _pad_cols(kr, LANE), _pad_cols(kr[:, perm], LANE), dn_qkv, dn_z, sc3,
                             _pad_cols(dn_ab, LANE)], axis=-1).astype(BF16)
    dq = MLA_NOPE + MLA_ROPE
    wq = mla_w_qb[l].reshape(MLA_Q_LORA, MLA_HEADS, dq)
    wq_rope = wq[:, :, MLA_NOPE:]
    wq_cat = jnp.concatenate([wq[:, :, :MLA_NOPE], _pad_cols(wq_rope, LANE), _pad_cols(wq_rope[:, :, perm], LANE)],
                             axis=-1).reshape(MLA_Q_LORA, MLA_HEADS * QH_W).astype(BF16)
    alog_row = _pad_cols(jnp.concatenate([dn_a_log[l].reshape(1, -1), jnp.zeros((1, 2 * DN_HEADS), F32)], -1), LANE)
    dtb_row = _pad_cols(jnp.concatenate([dn_dt_bias[l].reshape(1, -1), jnp.zeros((1, 2 * DN_HEADS), F32)], -1), LANE)
    rw = _pad_cols(router_w[l], LANE)
    rw_hi = rw.astype(BF16)
    rw_lo = (rw - rw_hi.astype(F32)).astype(BF16)
    rw_cat = jnp.concatenate([rw_hi, rw_lo], axis=-1)
    rb = jnp.concatenate([router_b[l], jnp.full((LANE - N_EXPERTS,), 0.0, F32)]).reshape(1, LANE)
    n_e, f2 = expert_b1.shape[1:]
    b1 = expert_b1[l].reshape(n_e, f2 // GLU_GROUP, LANE, 2).transpose(0, 1, 3, 2).reshape(n_e, 1, f2)
    return w_cat, wq_cat, alog_row, dtb_row, rw_cat, rw_hi, rb, b1


def kernel(x, c, ctx, c_ctx, ada_w, ada_b, norm1_g, norm2_g, w_in, dn_conv_w, dn_a_log, dn_dt_bias, dn_norm_g, sc_conv_w, mla_q_norm_g, mla_w_qb, mla_kv_norm_g, mla_w_kvb, w_branch_gate, b_branch_gate, w_branch_dn, w_branch_sc, w_branch_mla, w_out, router_w, router_b, expert_w1, expert_b1, expert_w2, expert_b2, final_norm_g):
    bsz, t_lat, d = x.shape
    t_ctx = ctx.shape[1]
    depth = ada_w.shape[0]
    assert t_ctx == ROW_TILE and t_lat % ROW_TILE == 0 and t_lat % GRID_W == 0
    t = t_ctx + t_lat

    cos_t, sin_t = _rope_tables(t_lat, t_ctx)
    n_mod_rows = -(-(bsz + 1) // 8) * 8
    cvec = jnp.concatenate([c, c_ctx[None], jnp.zeros((n_mod_rows - bsz - 1, d), F32)], axis=0)
    mods = _ada_mods(cvec, ada_w, ada_b)

    h = (x, ctx)
    row2 = lambda v: v.reshape(1, -1)
    for l in range(depth):
        mod_lat = mods[l, :bsz].reshape(bsz, 1, 6, d)
        mod_ctx = jnp.broadcast_to(mods[l, bsz].reshape(1, 1, 6, d), (bsz, 1, 6, d))
        mod = jnp.concatenate([mod_ctx, mod_lat], axis=1)
        (w_cat, wq_cat, alog_row, dtb_row, rw_cat, rw_hi, rb, b1) = _layer_weights(
            l, w_in, dn_a_log, dn_dt_bias, mla_w_qb, router_w, router_b, expert_b1)

        proj = _inproj(h, mod, row2(norm1_g[l]), w_cat, row2(mla_q_norm_g[l]), wq_cat,
                       row2(mla_kv_norm_g[l]), mla_w_kvb[l].astype(BF16), cos_t, sin_t)
        if l == 0:
            h = proj[-1]
        qkv, z, sc, ab, q, k, v = proj[:7]
        qkvn, ysc, gb = _prep(qkv, sc, ab, dn_conv_w[l], sc_conv_w[l], alog_row, dtb_row)
        o_f, o_b = _deltanet(qkvn, gb, t_ctx // DN_CHUNK)
        ymla = _mla(q, k, v)
        last = l == depth - 1
        j0 = 1 if last else 0
        hn, xm2, tg, lrank, info, cnt = _merge(h, mod, row2(norm1_g[l]), o_f, o_b, z, row2(dn_norm_g[l]), ysc, ymla,
                                 w_branch_gate[l].astype(BF16), row2(b_branch_gate[l]),
                                 w_branch_dn[l].astype(BF16), w_branch_sc[l].astype(BF16),
                                 w_branch_mla[l].astype(BF16), w_out[l].astype(BF16), row2(norm2_g[l]),
                                 rw_cat, rw_hi, rb, j0)
        runs, tails, block_expert, n_real, n_blocks = _route(info, cnt, bsz * (t - j0 * ROW_TILE) * TOP_K)
        xs = _dispatch(runs, tails, n_real, xm2, lrank, n_blocks * EXPERT_BLOCK)
        y_rows = _experts(block_expert, n_real, xs, expert_w1, b1, expert_w2, expert_b2[l][:, None, :], l)
        h = _combine(runs, lrank, y_rows, hn, mod, tg, row2(final_norm_g), last, j0)
    return h
```
